```python
import math
import jax, jax.numpy as jnp
from jax import lax
import numpy as np

D_MODEL = 2048
BATCH = 4
SEQ = 2048
DEPTH = 2
DEC_BATCH = 128
DEC_SEQ = 1
PAST_LEN = 16384
PAGE_SIZE = 128

N_MIXERS = 2
N_DELTA_LAYERS = (DEPTH + 1) // 2
N_SSM_LAYERS = DEPTH // 2
RMS_EPS = 1e-6

GDN_QK_HEADS = 16
GDN_V_HEADS = 32
GDN_DK = 128
GDN_DV = 128
GDN_KEY_DIM = GDN_QK_HEADS * GDN_DK
GDN_VAL_DIM = GDN_V_HEADS * GDN_DV
GDN_CONV_DIM = 2 * GDN_KEY_DIM + GDN_VAL_DIM
GDN_CONV_W = 4
GDN_CHUNK = 64
GDN_IN_DIM = GDN_CONV_DIM + GDN_VAL_DIM + 2 * GDN_V_HEADS

SSM_EXPAND = 2
SSM_WIDTH = SSM_EXPAND * D_MODEL
SSM_GROUP = 16
SSM_GROUPS = SSM_WIDTH // SSM_GROUP
SSM_STATE = 64
SSM_BLOCK = 256
DT_MIN = 1e-3
DT_MAX = 1e-1

kernel_name = "gdn_s5_hybrid_step"


def rms_norm(x, g):
    xf = x.astype(jnp.float32)
    y = xf * lax.rsqrt(jnp.mean(xf * xf, axis=-1, keepdims=True) + RMS_EPS)
    return (y * g.astype(jnp.float32)).astype(x.dtype)


def l2_normalize(x):
    xf = x.astype(jnp.float32)
    return xf * lax.rsqrt(jnp.sum(xf * xf, axis=-1, keepdims=True) + 1e-6)


def causal_conv_silu(x, buf, w):
    T = x.shape[1]
    xp = jnp.concatenate([buf.astype(x.dtype), x], axis=1)
    y = xp[:, 0:T] * w[0]
    for j in range(1, GDN_CONV_W):
        y = y + xp[:, j:j + T] * w[j]
    return jax.nn.silu(y), xp[:, -(GDN_CONV_W - 1):]


def gated_delta_rule(q, k, v, g, beta, s0):
    Bsz, T, H, _ = q.shape
    C = min(GDN_CHUNK, T)
    n = -(-T // C)
    pad = n * C - T

    def prep(a):
        a = jnp.pad(a.astype(jnp.float32), [(0, 0), (0, pad)] + [(0, 0)] * (a.ndim - 2))
        a = a.reshape((Bsz, n, C) + a.shape[2:])
        return jnp.moveaxis(a, 3, 1)

    q, k, v, g, beta = prep(q), prep(k), prep(v), prep(g), prep(beta)
    dv = v.shape[-1]
    gc = jnp.cumsum(g, axis=-1)
    causal = jnp.tril(jnp.ones((C, C), bool))
    strict = jnp.tril(jnp.ones((C, C), bool), -1)
    decay = jnp.exp(jnp.where(causal, gc[..., :, None] - gc[..., None, :], -jnp.inf))
    kb = k * beta[..., None]
    lower = jnp.where(strict, jnp.einsum('bhncd,bhnsd->bhncs', kb, k) * decay, 0.0)
    eye = jnp.eye(C, dtype=jnp.float32)
    rhs = jnp.concatenate([v * beta[..., None], kb * jnp.exp(gc)[..., None]], axis=-1)
    sol = lax.linalg.triangular_solve(lower + eye, rhs, left_side=True, lower=True)
    u, w = sol[..., :dv], sol[..., dv:]
    attn = jnp.einsum('bhncd,bhnsd->bhncs', q, k) * decay
    q_dec = q * jnp.exp(gc)[..., None]
    k_dec = k * jnp.exp(gc[..., -1:] - gc)[..., None]
    g_last = jnp.exp(gc[..., -1])
    xs = tuple(jnp.moveaxis(a, 2, 0) for a in (u, w, attn, q_dec, k_dec, g_last))

    def step(S, inp):
        u_c, w_c, a_c, qd_c, kd_c, gl_c = inp
        v_new = u_c - jnp.einsum('bhcd,bhde->bhce', w_c, S)
        o = jnp.einsum('bhcd,bhde->bhce', qd_c, S) + jnp.einsum('bhcs,bhse->bhce', a_c, v_new)
        S = S * gl_c[..., None, None] + jnp.einsum('bhcd,bhce->bhde', kd_c, v_new)
        return S, o

    S, o = lax.scan(step, s0.astype(jnp.float32), xs)
    o = jnp.transpose(o, (1, 0, 3, 2, 4)).reshape(Bsz, n * C, H, dv)[:, :T]
    return o, S


def gdn_branch(h, conv_buf, s0, w_in, conv_w, a_log, dt_bias, o_gain, w_out):
    Bsz, T, _ = h.shape
    f32 = jnp.float32
    proj = h @ w_in
    i1 = GDN_CONV_DIM
    i2 = i1 + GDN_VAL_DIM
    i3 = i2 + GDN_V_HEADS
    qkv, z, b, a = jnp.split(proj, [i1, i2, i3], axis=-1)
    qkv, new_buf = causal_conv_silu(qkv, conv_buf, conv_w)
    q, k, v = jnp.split(qkv, [GDN_KEY_DIM, 2 * GDN_KEY_DIM], axis=-1)
    rep = GDN_V_HEADS // GDN_QK_HEADS
    q = jnp.repeat(l2_normalize(q.reshape(Bsz, T, GDN_QK_HEADS, GDN_DK)), rep, axis=2) * (GDN_DK ** -0.5)
    k = jnp.repeat(l2_normalize(k.reshape(Bsz, T, GDN_QK_HEADS, GDN_DK)), rep, axis=2)
    v = v.reshape(Bsz, T, GDN_V_HEADS, GDN_DV)
    beta = jax.nn.sigmoid(b.astype(f32))
    g = -jnp.exp(a_log.astype(f32)) * jax.nn.softplus(a.astype(f32) + dt_bias.astype(f32))
    o, s_new = gated_delta_rule(q, k, v, g, beta, s0)
    zg = z.reshape(Bsz, T, GDN_V_HEADS, GDN_DV).astype(f32)
    o = o * lax.rsqrt(jnp.mean(o * o, axis=-1, keepdims=True) + RMS_EPS) * o_gain.astype(f32) * jax.nn.silu(zg)
    out = o.reshape(Bsz, T, GDN_VAL_DIM).astype(h.dtype) @ w_out
    return out, new_buf, s_new


def s5_branch(h, h0_re, h0_im, w_in, lam_re, lam_im, b_re, b_im, c_re, c_im, d_skip, log_dt, w_glu, b_glu, w_out):
    Bsz, T, _ = h.shape
    f32 = jnp.float32
    u, z = jnp.split(h @ w_in, 2, axis=-1)
    lam = lax.complex(jnp.minimum(lam_re.astype(f32), -1e-4), lam_im.astype(f32))
    dt = jnp.exp(log_dt.astype(f32))[:, None]
    a_bar = jnp.exp(lam * dt)
    b_bar = ((a_bar - 1.0) / lam)[..., None] * lax.complex(b_re.astype(f32), b_im.astype(f32))
    cmat = lax.complex(c_re.astype(f32), c_im.astype(f32))
    blk = math.gcd(T, SSM_BLOCK)
    nb = T // blk
    ub = jnp.moveaxis(u.astype(f32).reshape(Bsz, nb, blk, SSM_GROUPS, SSM_GROUP), 1, 0)

    def combine(e1, e2):
        a1, b1 = e1
        a2, b2 = e2
        return a2 * a1, a2 * b1 + b2

    def block(hc, u_blk):
        bu = jnp.einsum('gpc,btgc->btgp', b_bar, u_blk)
        bu = bu.at[:, 0].add(a_bar * hc)
        _, hs = lax.associative_scan(combine, (jnp.broadcast_to(a_bar, bu.shape), bu), axis=1)
        y = jnp.real(jnp.einsum('gcp,btgp->btgc', cmat, hs))
        return hs[:, -1], y

    hc0 = lax.complex(h0_re.astype(f32), h0_im.astype(f32))
    h_last, y = lax.scan(block, hc0, ub)
    y = jnp.moveaxis(y, 0, 1).reshape(Bsz, T, SSM_WIDTH) + d_skip.astype(f32) * u.astype(f32)
    y = jax.nn.gelu(y)
    y = y * jax.nn.sigmoid((y.astype(h.dtype) @ w_glu + b_glu).astype(f32))
    y = y * jax.nn.silu(z.astype(f32))
    out = y.astype(h.dtype) @ w_out
    return out, jnp.real(h_last), jnp.imag(h_last)


def trunk(x, conv0, delta0, re0, im0, gdn_w, ssm_w, norm_final):
    (norm_gdn, w_in_gdn, conv_gdn, a_log_gdn, dt_bias_gdn, onorm_gdn, w_out_gdn) = gdn_w
    (norm_ssm, w_in_ssm, lam_re, lam_im, b_re, b_im, c_re, c_im, d_ssm, log_dt_ssm,
     w_glu_ssm, b_glu_ssm, w_out_ssm) = ssm_w
    convs, deltas, res, ims = [], [], [], []
    for i in range(DEPTH):
        j = i // N_MIXERS
        if i % N_MIXERS == 0:
            h = rms_norm(x, norm_gdn[j])
            out, cb, ds = gdn_branch(h, conv0[j], delta0[j], w_in_gdn[j], conv_gdn[j], a_log_gdn[j],
                                     dt_bias_gdn[j], onorm_gdn[j], w_out_gdn[j])
            convs.append(cb)
            deltas.append(ds)
        else:
            h = rms_norm(x, norm_ssm[j])
            out, hr, hi = s5_branch(h, re0[j], im0[j], w_in_ssm[j], lam_re[j], lam_im[j], b_re[j], b_im[j],
                                    c_re[j], c_im[j], d_ssm[j], log_dt_ssm[j], w_glu_ssm[j], b_glu_ssm[j],
                                    w_out_ssm[j])
            res.append(hr)
            ims.append(hi)
        x = x + out.astype(x.dtype)
    y = rms_norm(x, norm_final)
    return y, jnp.stack(convs), jnp.stack(deltas), jnp.stack(res), jnp.stack(ims)


def setup_inputs(seed: int = 0) -> dict:
    key = jax.random.key(seed)
    ks = jax.random.split(key, 32)
    f32 = jnp.float32
    nrm = lambda k, s, sc: jax.random.normal(k, s, f32) * sc
    NA, NB = N_DELTA_LAYERS, N_SSM_LAYERS
    dt0 = jnp.exp(jax.random.uniform(ks[10], (NA, GDN_V_HEADS), f32, math.log(DT_MIN), math.log(DT_MAX)))
    lam_im = jnp.pi * jnp.arange(SSM_STATE, dtype=f32) + nrm(ks[16], (NB, SSM_GROUPS, SSM_STATE), 0.01)
    return {
        "x_prompt": nrm(ks[0], (BATCH, SEQ, D_MODEL), 1.0),
        "x_sample": nrm(ks[1], (DEC_BATCH, DEC_SEQ, D_MODEL), 1.0),
        "state_gdn_conv": nrm(ks[2], (NA, DEC_BATCH, GDN_CONV_W - 1, GDN_CONV_DIM), 1.0),
        "state_gdn_delta": nrm(ks[3], (NA, DEC_BATCH, GDN_V_HEADS, GDN_DK, GDN_DV), GDN_DK ** -0.5),
        "state_ssm_re": nrm(ks[4], (NB, DEC_BATCH, SSM_GROUPS, SSM_STATE), 0.5),
        "state_ssm_im": nrm(ks[5], (NB, DEC_BATCH, SSM_GROUPS, SSM_STATE), 0.5),
        "norm_gdn": 1.0 + nrm(ks[6], (NA, D_MODEL), 0.02),
        "w_in_gdn": nrm(ks[7], (NA, D_MODEL, GDN_IN_DIM), D_MODEL ** -0.5),
        "conv_gdn": nrm(ks[8], (NA, GDN_CONV_W, GDN_CONV_DIM), GDN_CONV_W ** -0.5),
        "a_log_gdn": jnp.log(jax.random.uniform(ks[9], (NA, GDN_V_HEADS), f32, 1.0, 16.0)),
        "dt_bias_gdn": dt0 + jnp.log(-jnp.expm1(-dt0)),
        "onorm_gdn": 1.0 + nrm(ks[11], (NA, GDN_DV), 0.02),
        "w_out_gdn": nrm(ks[12], (NA, GDN_VAL_DIM, D_MODEL), GDN_VAL_DIM ** -0.5),
        "norm_ssm": 1.0 + nrm(ks[13], (NB, D_MODEL), 0.02),
        "w_in_ssm": nrm(ks[14], (NB, D_MODEL, 2 * SSM_WIDTH), D_MODEL ** -0.5),
        "lam_re": -0.5 + nrm(ks[15], (NB, SSM_GROUPS, SSM_STATE), 0.01),
        "lam_im": lam_im,
        "b_re": nrm(ks[17], (NB, SSM_GROUPS, SSM_STATE, SSM_GROUP), (2 * SSM_GROUP) ** -0.5),
        "b_im": nrm(ks[18], (NB, SSM_GROUPS, SSM_STATE, SSM_GROUP), (2 * SSM_GROUP) ** -0.5),
        "c_re": nrm(ks[19], (NB, SSM_GROUPS, SSM_GROUP, SSM_STATE), SSM_STATE ** -0.5),
        "c_im": nrm(ks[20], (NB, SSM_GROUPS, SSM_GROUP, SSM_STATE), SSM_STATE ** -0.5),
        "d_ssm": nrm(ks[21], (NB, SSM_WIDTH), 1.0),
        "log_dt_ssm": jax.random.uniform(ks[22], (NB, SSM_GROUPS), f32, math.log(DT_MIN), math.log(DT_MAX)),
        "w_glu_ssm": nrm(ks[23], (NB, SSM_WIDTH, SSM_WIDTH), SSM_WIDTH ** -0.5),
        "b_glu_ssm": nrm(ks[24], (NB, SSM_WIDTH), 0.01),
        "w_out_ssm": nrm(ks[25], (NB, SSM_WIDTH, D_MODEL), SSM_WIDTH ** -0.5),
        "norm_final": 1.0 + nrm(ks[26], (D_MODEL,), 0.02),
    }


def reference(x_prompt, x_sample, state_gdn_conv, state_gdn_delta, state_ssm_re, state_ssm_im,
              norm_gdn, w_in_gdn, conv_gdn, a_log_gdn, dt_bias_gdn, onorm_gdn, w_out_gdn,
              norm_ssm, w_in_ssm, lam_re, lam_im, b_re, b_im, c_re, c_im, d_ssm, log_dt_ssm,
              w_glu_ssm, b_glu_ssm, w_out_ssm, norm_final):
    gdn_w = (norm_gdn, w_in_gdn, conv_gdn, a_log_gdn, dt_bias_gdn, onorm_gdn, w_out_gdn)
    ssm_w = (norm_ssm, w_in_ssm, lam_re, lam_im, b_re, b_im, c_re, c_im, d_ssm, log_dt_ssm,
             w_glu_ssm, b_glu_ssm, w_out_ssm)
    Bp = x_prompt.shape[0]
    conv_p0 = jnp.zeros((N_DELTA_LAYERS, Bp, GDN_CONV_W - 1, GDN_CONV_DIM), x_prompt.dtype)
    delta_p0 = jnp.zeros((N_DELTA_LAYERS, Bp, GDN_V_HEADS, GDN_DK, GDN_DV), jnp.float32)
    ssm_p0 = jnp.zeros((N_SSM_LAYERS, Bp, SSM_GROUPS, SSM_STATE), jnp.float32)
    y_prompt, conv_p, delta_p, re_p, im_p = trunk(x_prompt, conv_p0, delta_p0, ssm_p0, ssm_p0,
                                                  gdn_w, ssm_w, norm_final)
    y_sample, conv_s, delta_s, re_s, im_s = trunk(x_sample, state_gdn_conv, state_gdn_delta, state_ssm_re,
                                                  state_ssm_im, gdn_w, ssm_w, norm_final)
    return (y_prompt, y_sample, conv_p, delta_p, re_p, im_p, conv_s, delta_s, re_s, im_s)
```

```python
import functools
import math

import jax
import jax.numpy as jnp
from jax import lax
from jax.experimental import pallas as pl
from jax.experimental.pallas import tpu as pltpu

F32 = jnp.float32
BF16 = jnp.bfloat16

RMS_EPS = 1e-6
L2_EPS = 1e-6

D_MODEL = 2048
QK_HEADS = 16
V_HEADS = 32
DK = 128
DV = 128
KEY_DIM = QK_HEADS * DK
VAL_DIM = V_HEADS * DV
CONV_DIM = 2 * KEY_DIM + VAL_DIM
CONV_W = 4
MAIN_DIM = CONV_DIM + VAL_DIM

SSM_WIDTH = 4096
SSM_GROUP = 16
SSM_GROUPS = SSM_WIDTH // SSM_GROUP
SSM_STATE = 64
GROUPS_PER_STEP = 16
CH_PER_STEP = GROUPS_PER_STEP * SSM_GROUP
ST_PER_STEP = GROUPS_PER_STEP * SSM_STATE
N_GSTEPS = SSM_GROUPS // GROUPS_PER_STEP

CHUNK = 128
SCAN_BLOCK = 256
SCAN_SEG = SCAN_BLOCK // 8

VMEM_LIMIT = 56 * 1024 * 1024


def _params(sem):
    return pltpu.CompilerParams(dimension_semantics=sem, vmem_limit_bytes=VMEM_LIMIT)


def _mm(a, b):
    return jnp.dot(a.astype(BF16), b.astype(BF16), preferred_element_type=F32)


def _mm_hi(a, b):
    return jnp.dot(a, b, precision=lax.Precision.HIGHEST, preferred_element_type=F32)


def _silu(x):
    return x * jax.nn.sigmoid(x)


def _rms(x, g):
    ms = jnp.mean(x * x, axis=-1, keepdims=True)
    return x * lax.rsqrt(ms + RMS_EPS) * g


def _norm_matmul_body(x_ref, g_ref, w_ref, *rest, has_aux):
    if has_aux:
        waux_ref, o_ref, aux_ref, h_scr = rest
    else:
        o_ref, h_scr = rest

    @pl.when(pl.program_id(1) == 0)
    def _():
        h = _rms(x_ref[...], g_ref[...]).astype(BF16)
        h_scr[...] = h
        if has_aux:
            aux_ref[...] = jnp.dot(h, waux_ref[...], preferred_element_type=F32)

    o_ref[...] = jnp.dot(h_scr[...], w_ref[...], preferred_element_type=F32)


def _norm_matmul(x, g, w, w_aux=None, *, tm, tn, name):
    m, d = x.shape
    n = w.shape[1]
    has_aux = w_aux is not None
    in_specs = [
        pl.BlockSpec((tm, d), lambda i, j: (i, 0)),
        pl.BlockSpec((1, d), lambda i, j: (0, 0)),
        pl.BlockSpec((d, tn), lambda i, j: (0, j)),
    ]
    out_shape = [jax.ShapeDtypeStruct((m, n), F32)]
    out_specs = [pl.BlockSpec((tm, tn), lambda i, j: (i, j))]
    args = [x, g.reshape(1, d), w]
    if has_aux:
        na = w_aux.shape[1]
        in_specs.append(pl.BlockSpec((d, na), lambda i, j: (0, 0)))
        out_shape.append(jax.ShapeDtypeStruct((m, na), F32))
        out_specs.append(pl.BlockSpec((tm, na), lambda i, j: (i, 0)))
        args.append(w_aux)
    res = pl.pallas_call(
        functools.partial(_norm_matmul_body, has_aux=has_aux),
        out_shape=out_shape,
        grid=(m // tm, n // tn),
        in_specs=in_specs,
        out_specs=out_specs,
        scratch_shapes=[pltpu.VMEM((tm, d), BF16)],
        compiler_params=_params(("parallel", "arbitrary")),
        name=name,
    )(*args)
    return res if has_aux else res[0]


def _outproj_body(a_ref, w_ref, x_ref, *rest, nk, final_norm):
    if final_norm:
        g_ref, o_ref = rest
    else:
        (o_ref,) = rest
    k = pl.program_id(1)
    p = jnp.dot(a_ref[...], w_ref[...], preferred_element_type=F32)

    @pl.when(k == 0)
    def _():
        o_ref[...] = x_ref[...] + p

    @pl.when(k > 0)
    def _():
        o_ref[...] += p

    if final_norm:
        @pl.when(k == nk - 1)
        def _():
            o_ref[...] = _rms(o_ref[...], g_ref[...])


def _outproj(a, w, x, g=None, *, tm, tk, name):
    m, kd = a.shape
    n = w.shape[1]
    nk = kd // tk
    final_norm = g is not None
    in_specs = [
        pl.BlockSpec((tm, tk), lambda i, k: (i, k)),
        pl.BlockSpec((tk, n), lambda i, k: (k, 0)),
        pl.BlockSpec((tm, n), lambda i, k: (i, 0)),
    ]
    args = [a, w, x]
    if final_norm:
        in_specs.append(pl.BlockSpec((1, n), lambda i, k: (0, 0)))
        args.append(g.reshape(1, n))
    return pl.pallas_call(
        functools.partial(_outproj_body, nk=nk, final_norm=final_norm),
        out_shape=jax.ShapeDtypeStruct((m, n), F32),
        grid=(m // tm, nk),
        in_specs=in_specs,
        out_specs=pl.BlockSpec((tm, n), lambda i, k: (i, 0)),
        compiler_params=_params(("parallel", "arbitrary")),
        name=name,
    )(*args)


def _conv_silu_chunk(x_ref, cw, n):
    cc = x_ref.shape[1]
    r0 = pl.multiple_of(n * CHUNK, CHUNK)
    cur = x_ref[pl.ds(r0, CHUNK), :]
    p0 = pl.multiple_of(jnp.maximum(r0 - 8, 0), 8)
    prev = jnp.where(n > 0, x_ref[pl.ds(p0, 8), :], 0.0)
    row8 = lax.broadcasted_iota(jnp.int32, (8, cc), 0)
    y = cur * cw[CONV_W - 1:CONV_W, :]
    for s in range(1, CONV_W):
        sh = pltpu.roll(cur, s, 0)
        top = jnp.where(row8 < s, pltpu.roll(prev, s, 0), sh[0:8, :])
        sh = jnp.concatenate([top, sh[8:, :]], axis=0)
        y = y + sh * cw[CONV_W - 1 - s:CONV_W - s, :]
    return _silu(y)


def _inv_unit_lower(a, row, col):
    eye = jnp.where(row == col, 1.0, 0.0).astype(F32)

    def same_block(shift):
        return (row >> shift) == (col >> shift)

    d = jnp.where(same_block(4), a, 0.0)
    t = eye - d
    p = d
    for _ in range(3):
        p = _mm_hi(p, p)
        t = t + _mm_hi(t, p)
    shift = 4
    while (1 << shift) < CHUNK:
        nxt = jnp.where(same_block(shift + 1) & jnp.logical_not(same_block(shift)), a, 0.0)
        t = t - _mm_hi(_mm_hi(t, nxt), t)
        shift += 1
    return t


def _gdn_prompt_body(q_ref, k_ref, v_ref, z_ref, ba_ref, cwq_ref, cwk_ref, cwv_ref, gp_ref, on_ref,
                     og_ref, s_ref, u_s, w_s, qd_s, at_s, kdt_s, gl_s):
    hq = pl.program_id(1)
    t_len = q_ref.shape[0]
    n_chunks = t_len // CHUNK
    row = lax.broadcasted_iota(jnp.int32, (CHUNK, CHUNK), 0)
    col = lax.broadcasted_iota(jnp.int32, (CHUNK, CHUNK), 1)
    causal = row >= col
    strict = row > col
    tril = jnp.where(causal, 1.0, 0.0).astype(F32)
    cwq = cwq_ref[...]
    cwk = cwk_ref[...]
    cwv = cwv_ref[...]
    neg_a = -jnp.exp(gp_ref[0:1, :])
    dt_bias = gp_ref[1:2, :]

    def phase1(n, carry):
        r0 = pl.multiple_of(n * CHUNK, CHUNK)
        qc = _conv_silu_chunk(q_ref, cwq, n)
        kc = _conv_silu_chunk(k_ref, cwk, n)
        vc = _conv_silu_chunk(v_ref, cwv, n)
        qn = qc * lax.rsqrt(jnp.sum(qc * qc, axis=-1, keepdims=True) + L2_EPS) * (DK ** -0.5)
        kn = kc * lax.rsqrt(jnp.sum(kc * kc, axis=-1, keepdims=True) + L2_EPS)
        ba = ba_ref[pl.ds(r0, CHUNK), :]
        beta_all = jax.nn.sigmoid(ba)
        xa = ba + dt_bias
        g_all = neg_a * (jnp.maximum(xa, 0.0) + jnp.log1p(jnp.exp(-jnp.abs(xa))))
        gcum = _mm_hi(tril, g_all)
        gcum_t = gcum.T
        kn_t = kn.T
        kk = _mm(kn, kn_t)
        qk = _mm(qn, kn_t)
        for j in range(2):
            lane_b = 2 * hq + j
            lane_a = V_HEADS + lane_b
            gcol = jnp.sum(jnp.where(col == lane_a, gcum, 0.0), axis=1, keepdims=True)
            bcol = jnp.sum(jnp.where(col == lane_b, beta_all, 0.0), axis=1, keepdims=True)
            grow = jnp.sum(jnp.where(row == lane_a, gcum_t, 0.0), axis=0, keepdims=True)
            glast = gcol[CHUNK - 1:CHUNK, :]
            decay = jnp.exp(jnp.where(causal, gcol - grow, -jnp.inf))
            a_mat = jnp.where(strict, kk * decay, 0.0) * bcol
            t_mat = _inv_unit_lower(a_mat, row, col)
            eg = jnp.exp(gcol)
            rhs = jnp.concatenate([vc[:, j * DV:(j + 1) * DV] * bcol, kn * (bcol * eg)], axis=1)
            sol = _mm_hi(t_mat, rhs)
            u_s[j, pl.ds(r0, CHUNK), :] = sol[:, :DV]
            w_s[j, pl.ds(r0, CHUNK), :] = sol[:, DV:].astype(BF16)
            qd_s[j, pl.ds(r0, CHUNK), :] = (qn * eg).astype(BF16)
            at_s[j, pl.ds(r0, CHUNK), :] = (qk * decay).astype(BF16)
            kdt_s[j, pl.ds(r0, CHUNK), :] = (kn_t * jnp.exp(glast - grow)).astype(BF16)
            gl_s[j, pl.ds(n, 1), :] = jnp.broadcast_to(jnp.exp(glast), (1, DV))
        return carry

    lax.fori_loop(0, n_chunks, phase1, 0)

    gain = on_ref[...]

    def phase2(n, states):
        r0 = pl.multiple_of(n * CHUNK, CHUNK)
        new_states = []
        for j in range(2):
            s_mat = states[j]
            wq = jnp.concatenate([w_s[j, pl.ds(r0, CHUNK), :], qd_s[j, pl.ds(r0, CHUNK), :]], axis=0)
            ws_qs = jnp.dot(wq, s_mat.astype(BF16), preferred_element_type=F32)
            v_new = u_s[j, pl.ds(r0, CHUNK), :] - ws_qs[:CHUNK]
            v_new_b = v_new.astype(BF16)
            o = ws_qs[CHUNK:] + jnp.dot(at_s[j, pl.ds(r0, CHUNK), :], v_new_b, preferred_element_type=F32)
            s_new = s_mat * gl_s[j, pl.ds(n, 1), :] + jnp.dot(kdt_s[j, pl.ds(r0, CHUNK), :], v_new_b,
                                                               preferred_element_type=F32)
            zg = z_ref[pl.ds(r0, CHUNK), j * DV:(j + 1) * DV]
            og = o * lax.rsqrt(jnp.mean(o * o, axis=-1, keepdims=True) + RMS_EPS) * gain * _silu(zg)
            og_ref[pl.ds(r0, CHUNK), j * DV:(j + 1) * DV] = og.astype(BF16)
            new_states.append(s_new)
        return tuple(new_states)

    zero = jnp.zeros((DK, DV), F32)
    final = lax.fori_loop(0, n_chunks, phase2, (zero, zero))
    s_ref[0, 0] = final[0]
    s_ref[0, 1] = final[1]


def _gdn_prompt(proj, ba, conv_w, gate_params, o_gain, *, batch, t_len):
    kq = KEY_DIM // DK
    in_specs = [
        pl.BlockSpec((t_len, DK), lambda b, h: (b, h)),
        pl.BlockSpec((t_len, DK), lambda b, h: (b, kq + h)),
        pl.BlockSpec((t_len, 2 * DV), lambda b, h: (b, (2 * KEY_DIM) // (2 * DV) + h)),
        pl.BlockSpec((t_len, 2 * DV), lambda b, h: (b, CONV_DIM // (2 * DV) + h)),
        pl.BlockSpec((t_len, 128), lambda b, h: (b, 0)),
        pl.BlockSpec((CONV_W, DK), lambda b, h: (0, h)),
        pl.BlockSpec((CONV_W, DK), lambda b, h: (0, kq + h)),
        pl.BlockSpec((CONV_W, 2 * DV), lambda b, h: (0, (2 * KEY_DIM) // (2 * DV) + h)),
        pl.BlockSpec((2, 128), lambda b, h: (0, 0)),
        pl.BlockSpec((1, DV), lambda b, h: (0, 0)),
    ]
    out_shape = [
        jax.ShapeDtypeStruct((batch * t_len, VAL_DIM), BF16),
        jax.ShapeDtypeStruct((batch, V_HEADS, DK, DV), F32),
    ]
    out_specs = [
        pl.BlockSpec((t_len, 2 * DV), lambda b, h: (b, h)),
        pl.BlockSpec((1, 2, DK, DV), lambda b, h: (b, h, 0, 0)),
    ]
    scratch = [
        pltpu.VMEM((2, t_len, DV), F32),
        pltpu.VMEM((2, t_len, DK), BF16),
        pltpu.VMEM((2, t_len, DK), BF16),
        pltpu.VMEM((2, t_len, CHUNK), BF16),
        pltpu.VMEM((2, t_len, CHUNK), BF16),
        pltpu.VMEM((2, t_len // CHUNK, DV), F32),
    ]
    return pl.pallas_call(
        _gdn_prompt_body,
        out_shape=out_shape,
        grid=(batch, QK_HEADS),
        in_specs=in_specs,
        out_specs=out_specs,
        scratch_shapes=scratch,
        compiler_params=_params(("parallel", "arbitrary")),
        name="gdn_prompt",
    )(proj, proj, proj, proj, ba, conv_w, conv_w, conv_w, gate_params, o_gain.reshape(1, DV))


def _gdn_gates_body(ba_ref, gp_ref, beta_ref, eg_ref):
    ba = ba_ref[...]
    beta_ref[...] = jax.nn.sigmoid(ba)
    xa = ba + gp_ref[1:2, :]
    g = -jnp.exp(gp_ref[0:1, :]) * (jnp.maximum(xa, 0.0) + jnp.log1p(jnp.exp(-jnp.abs(xa))))
    eg_ref[...] = jnp.exp(g)


def _gdn_gates(ba, gate_params):
    m = ba.shape[0]
    return pl.pallas_call(
        _gdn_gates_body,
        out_shape=[jax.ShapeDtypeStruct((m, 128), F32)] * 2,
        name="gdn_gates",
    )(ba, gate_params)


def _gdn_decode_body(beta_ref, eg_ref, xs_ref, cw_ref, z_ref, on_ref, s_ref, so_ref, og_ref):
    b = pl.program_id(0)
    y = xs_ref[0, 0] * cw_ref[0]
    for j in range(1, CONV_W):
        y = y + xs_ref[0, j] * cw_ref[j]
    y = _silu(y)
    q = y[0:QK_HEADS]
    k = y[QK_HEADS:2 * QK_HEADS]
    v = y[2 * QK_HEADS:]
    qn = q * lax.rsqrt(jnp.sum(q * q, axis=-1, keepdims=True) + L2_EPS) * (DK ** -0.5)
    kn = k * lax.rsqrt(jnp.sum(k * k, axis=-1, keepdims=True) + L2_EPS)
    qk_t = jnp.concatenate([qn, kn, jnp.zeros((DK - 2 * QK_HEADS, DK), F32)], axis=0).T
    gain = on_ref[...]
    z = z_ref[0]
    for hv in range(V_HEADS):
        hq = hv // (V_HEADS // QK_HEADS)
        qcol = qk_t[:, hq:hq + 1]
        kcol = qk_t[:, QK_HEADS + hq:QK_HEADS + hq + 1]
        sd = s_ref[0, hv] * eg_ref[b, hv]
        ks = jnp.sum(sd * kcol, axis=0, keepdims=True)
        v_new = beta_ref[b, hv] * (v[hv:hv + 1, :] - ks)
        sn = sd + kcol * v_new
        so_ref[0, hv] = sn
        o = jnp.sum(sn * qcol, axis=0, keepdims=True)
        og = o * lax.rsqrt(jnp.mean(o * o, axis=-1, keepdims=True) + RMS_EPS) * gain * _silu(z[hv:hv + 1, :])
        og_ref[0, hv:hv + 1, :] = og


def _gdn_decode(beta, eg, xs, conv_w3, z3, o_gain, state):
    nb = xs.shape[0]
    slots = CONV_DIM // DK
    smem = pl.BlockSpec(memory_space=pltpu.SMEM)
    return pl.pallas_call(
        _gdn_decode_body,
        out_shape=[
            jax.ShapeDtypeStruct((nb, V_HEADS, DK, DV), F32),
            jax.ShapeDtypeStruct((nb, V_HEADS, DV), F32),
        ],
        grid=(nb,),
        in_specs=[
            smem,
            smem,
            pl.BlockSpec((1, CONV_W, slots, DK), lambda b: (b, 0, 0, 0)),
            pl.BlockSpec((CONV_W, slots, DK), lambda b: (0, 0, 0)),
            pl.BlockSpec((1, V_HEADS, DV), lambda b: (b, 0, 0)),
            pl.BlockSpec((1, DV), lambda b: (0, 0)),
            pl.BlockSpec((1, V_HEADS, DK, DV), lambda b: (b, 0, 0, 0)),
        ],
        out_specs=[
            pl.BlockSpec((1, V_HEADS, DK, DV), lambda b: (b, 0, 0, 0)),
            pl.BlockSpec((1, V_HEADS, DV), lambda b: (b, 0, 0)),
        ],
        compiler_params=_params(("parallel",)),
        name="gdn_decode",
    )(beta, eg, xs, conv_w3, z3, o_gain.reshape(1, DV), state)


def _ssm_pow_body(lr_ref, li_ref, ldt_ref, pr_ref, pi_ref):
    n = (lax.broadcasted_iota(jnp.int32, pr_ref.shape, 0) + 1).astype(F32)
    dt = jnp.exp(ldt_ref[...])
    lr = jnp.minimum(lr_ref[...], -1e-4) * dt
    li = li_ref[...] * dt
    mag = jnp.exp(n * lr)
    pr_ref[...] = mag * jnp.cos(n * li)
    pi_ref[...] = mag * jnp.sin(n * li)


def _ssm_pow(lr, li, ldt):
    gp = lr.shape[1]
    tl = 2048
    spec = pl.BlockSpec((1, tl), lambda i: (0, i))
    ospec = pl.BlockSpec((SCAN_SEG, tl), lambda i: (0, i))
    return pl.pallas_call(
        _ssm_pow_body,
        out_shape=[jax.ShapeDtypeStruct((SCAN_SEG, gp), F32)] * 2,
        grid=(gp // tl,),
        in_specs=[spec, spec, spec],
        out_specs=[ospec, ospec],
        name="ssm_pow",
    )(lr, li, ldt)


def _ssm_bbar_body(lr_ref, li_ref, ldt_ref, br_ref, bi_ref, or_ref, oi_ref):
    dt = jnp.exp(ldt_ref[...])
    lr = jnp.minimum(lr_ref[...], -1e-4)
    li = li_ref[...]
    mag = jnp.exp(lr * dt)
    xr = mag * jnp.cos(li * dt) - 1.0
    xi = mag * jnp.sin(li * dt)
    den = 1.0 / (lr * lr + li * li)
    fr = ((xr * lr + xi * li) * den)[:, None, :]
    fi = ((xi * lr - xr * li) * den)[:, None, :]
    br = br_ref[...]
    bi = bi_ref[...]
    or_ref[...] = fr * br - fi * bi
    oi_ref[...] = fr * bi + fi * br


def _ssm_bbar(lr, li, ldt, bt_re, bt_im):
    g, p = lr.shape
    tg = 64
    s2 = pl.BlockSpec((tg, p), lambda i: (i, 0))
    s1 = pl.BlockSpec((tg, 1), lambda i: (i, 0))
    s3 = pl.BlockSpec((tg, SSM_GROUP, p), lambda i: (i, 0, 0))
    return pl.pallas_call(
        _ssm_bbar_body,
        out_shape=[jax.ShapeDtypeStruct((g, SSM_GROUP, p), F32)] * 2,
        grid=(g // tg,),
        in_specs=[s2, s2, s1, s3, s3],
        out_specs=[s3, s3],
        name="ssm_bbar",
    )(lr, li, ldt, bt_re, bt_im)


def _block_diag(w):
    g, r, c = w.shape
    k = GROUPS_PER_STEP
    w = w.reshape(g // k, k, r, c)
    eye = jnp.eye(k, dtype=w.dtype)
    out = w[:, :, :, None, :] * eye[None, :, None, :, None]
    return out.reshape(g // k, k * r, k * c)


def _gelu_tanh(x):
    return 0.5 * x * (1.0 + jnp.tanh(math.sqrt(2.0 / math.pi) * (x + 0.044715 * (x * x * x))))


def _ssm_scan_body(u_ref, b_ref, cr_ref, ci_ref, pr_ref, pi_ref, d_ref, y_ref, hr_ref, hi_ref,
                   h_s, cr_s, ci_s):
    tb = pl.program_id(2)
    nt = ST_PER_STEP // 128
    u = u_ref[...]
    bu = jnp.dot(u.astype(BF16), b_ref[...], preferred_element_type=F32)
    for c in range(2 * nt):
        h_s[c] = bu[:, c * 128:(c + 1) * 128]

    @pl.when(tb == 0)
    def _():
        cr_s[...] = jnp.zeros_like(cr_s)
        ci_s[...] = jnp.zeros_like(ci_s)

    row8 = lax.broadcasted_iota(jnp.int32, (8, 128), 0)
    for c in range(nt):
        lanes = slice(c * 128, (c + 1) * 128)
        ar = jnp.broadcast_to(pr_ref[0:1, lanes], (8, 128))
        ai = jnp.broadcast_to(pi_ref[0:1, lanes], (8, 128))
        hr = jnp.zeros((8, 128), F32)
        hi = jnp.zeros((8, 128), F32)
        for i in range(SCAN_SEG):
            rows = pl.ds(i, 8, stride=SCAN_SEG)
            xr = h_s[c, rows, :]
            xi = h_s[nt + c, rows, :]
            hr, hi = ar * hr - ai * hi + xr, ar * hi + ai * hr + xi
            h_s[c, rows, :] = hr
            h_s[nt + c, rows, :] = hi

        a32r = pr_ref[SCAN_SEG - 1:SCAN_SEG, lanes]
        a32i = pi_ref[SCAN_SEG - 1:SCAN_SEG, lanes]
        c_r = cr_s[:, lanes]
        c_i = ci_s[:, lanes]
        cin_r = jnp.zeros((8, 128), F32)
        cin_i = jnp.zeros((8, 128), F32)
        for s in range(8):
            cin_r = jnp.where(row8 == s, c_r, cin_r)
            cin_i = jnp.where(row8 == s, c_i, cin_i)
            e_r = hr[s:s + 1, :]
            e_i = hi[s:s + 1, :]
            c_r, c_i = e_r + a32r * c_r - a32i * c_i, e_i + a32r * c_i + a32i * c_r
        cr_s[:, lanes] = c_r
        ci_s[:, lanes] = c_i

        for i in range(SCAN_SEG):
            rows = pl.ds(i, 8, stride=SCAN_SEG)
            p_r = pr_ref[i:i + 1, lanes]
            p_i = pi_ref[i:i + 1, lanes]
            h_s[c, rows, :] = h_s[c, rows, :] + (p_r * cin_r - p_i * cin_i)
            h_s[nt + c, rows, :] = h_s[nt + c, rows, :] + (p_r * cin_i + p_i * cin_r)

    h_re = jnp.concatenate([h_s[c] for c in range(nt)], axis=1)
    h_im = jnp.concatenate([h_s[nt + c] for c in range(nt)], axis=1)
    y = (jnp.dot(h_re.astype(BF16), cr_ref[...], preferred_element_type=F32)
         - jnp.dot(h_im.astype(BF16), ci_ref[...], preferred_element_type=F32))
    y_ref[...] = _gelu_tanh(y + d_ref[...] * u)

    @pl.when(tb == pl.num_programs(2) - 1)
    def _():
        hr_ref[...] = cr_s[...]
        hi_ref[...] = ci_s[...]


def _ssm_scan(uz, bblk, cre, cim, pw_r, pw_i, d_skip, *, batch, t_len):
    nt = t_len // SCAN_BLOCK
    return pl.pallas_call(
        _ssm_scan_body,
        out_shape=[
            jax.ShapeDtypeStruct((batch * t_len, SSM_WIDTH), F32),
            jax.ShapeDtypeStruct((batch, 1, SSM_GROUPS * SSM_STATE), F32),
            jax.ShapeDtypeStruct((batch, 1, SSM_GROUPS * SSM_STATE), F32),
        ],
        grid=(batch, N_GSTEPS, nt),
        in_specs=[
            pl.BlockSpec((SCAN_BLOCK, CH_PER_STEP), lambda b, g, t: (b * nt + t, g)),
            pl.BlockSpec((None, CH_PER_STEP, 2 * ST_PER_STEP), lambda b, g, t: (g, 0, 0)),
            pl.BlockSpec((None, ST_PER_STEP, CH_PER_STEP), lambda b, g, t: (g, 0, 0)),
            pl.BlockSpec((None, ST_PER_STEP, CH_PER_STEP), lambda b, g, t: (g, 0, 0)),
            pl.BlockSpec((SCAN_SEG, ST_PER_STEP), lambda b, g, t: (0, g)),
            pl.BlockSpec((SCAN_SEG, ST_PER_STEP), lambda b, g, t: (0, g)),
            pl.BlockSpec((1, CH_PER_STEP), lambda b, g, t: (0, g)),
        ],
        out_specs=[
            pl.BlockSpec((SCAN_BLOCK, CH_PER_STEP), lambda b, g, t: (b * nt + t, g)),
            pl.BlockSpec((None, 1, ST_PER_STEP), lambda b, g, t: (b, 0, g)),
            pl.BlockSpec((None, 1, ST_PER_STEP), lambda b, g, t: (b, 0, g)),
        ],
        scratch_shapes=[
            pltpu.VMEM((2 * ST_PER_STEP // 128, SCAN_BLOCK, 128), F32),
            pltpu.VMEM((1, ST_PER_STEP), F32),
            pltpu.VMEM((1, ST_PER_STEP), F32),
        ],
        compiler_params=_params(("parallel", "parallel", "arbitrary")),
        name="ssm_scan",
    )(uz, bblk, cre, cim, pw_r, pw_i, d_skip.reshape(1, SSM_WIDTH))


def _ssm_decode_body(u_ref, b_ref, cr_ref, ci_ref, pr_ref, pi_ref, d_ref, h0r_ref, h0i_ref,
                     y_ref, hr_ref, hi_ref):
    nst = ST_PER_STEP
    u = u_ref[...]
    bu = jnp.dot(u.astype(BF16), b_ref[...], preferred_element_type=F32)
    ar = pr_ref[0:1, :]
    ai = pi_ref[0:1, :]
    h0r = h0r_ref[...]
    h0i = h0i_ref[...]
    hr = bu[:, 0:nst] + ar * h0r - ai * h0i
    hi = bu[:, nst:2 * nst] + ar * h0i + ai * h0r
    hr_ref[...] = hr
    hi_ref[...] = hi
    y = (jnp.dot(hr.astype(BF16), cr_ref[...], preferred_element_type=F32)
         - jnp.dot(hi.astype(BF16), ci_ref[...], preferred_element_type=F32))
    y_ref[...] = _gelu_tanh(y + d_ref[...] * u)


def _ssm_decode(uz, bblk, cre, cim, pw_r, pw_i, d_skip, h0r, h0i):
    nb = uz.shape[0]
    return pl.pallas_call(
        _ssm_decode_body,
        out_shape=[
            jax.ShapeDtypeStruct((nb, SSM_WIDTH), F32),
            jax.ShapeDtypeStruct((nb, SSM_GROUPS * SSM_STATE), F32),
            jax.ShapeDtypeStruct((nb, SSM_GROUPS * SSM_STATE), F32),
        ],
        grid=(N_GSTEPS,),
        in_specs=[
            pl.BlockSpec((nb, CH_PER_STEP), lambda g: (0, g)),
            pl.BlockSpec((None, CH_PER_STEP, 2 * ST_PER_STEP), lambda g: (g, 0, 0)),
            pl.BlockSpec((None, ST_PER_STEP, CH_PER_STEP), lambda g: (g, 0, 0)),
            pl.BlockSpec((None, ST_PER_STEP, CH_PER_STEP), lambda g: (g, 0, 0)),
            pl.BlockSpec((SCAN_SEG, ST_PER_STEP), lambda g: (0, g)),
            pl.BlockSpec((SCAN_SEG, ST_PER_STEP), lambda g: (0, g)),
            pl.BlockSpec((1, CH_PER_STEP), lambda g: (0, g)),
            pl.BlockSpec((nb, ST_PER_STEP), lambda g: (0, g)),
            pl.BlockSpec((nb, ST_PER_STEP), lambda g: (0, g)),
        ],
        out_specs=[
            pl.BlockSpec((nb, CH_PER_STEP), lambda g: (0, g)),
            pl.BlockSpec((nb, ST_PER_STEP), lambda g: (0, g)),
            pl.BlockSpec((nb, ST_PER_STEP), lambda g: (0, g)),
        ],
        compiler_params=_params(("parallel",)),
        name="ssm_decode",
    )(uz, bblk, cre, cim, pw_r, pw_i, d_skip.reshape(1, SSM_WIDTH), h0r, h0i)


def _glu_body(y_ref, w_ref, b_ref, z_ref, o_ref, yb_scr, *, tn):
    j = pl.program_id(1)

    @pl.when(j == 0)
    def _():
        yb_scr[...] = y_ref[...].astype(BF16)

    t = jnp.dot(yb_scr[...], w_ref[...], preferred_element_type=F32) + b_ref[...]
    y = y_ref[:, pl.ds(pl.multiple_of(j * tn, tn), tn)]
    o_ref[...] = (y * jax.nn.sigmoid(t) * _silu(z_ref[...])).astype(BF16)


def _glu(y, w, b, uz, *, tm, tn):
    m, d = y.shape
    zoff = SSM_WIDTH // tn
    return pl.pallas_call(
        functools.partial(_glu_body, tn=tn),
        out_shape=jax.ShapeDtypeStruct((m, d), BF16),
        grid=(m // tm, d // tn),
        in_specs=[
            pl.BlockSpec((tm, d), lambda i, j: (i, 0)),
            pl.BlockSpec((d, tn), lambda i, j: (0, j)),
            pl.BlockSpec((1, tn), lambda i, j: (0, j)),
            pl.BlockSpec((tm, tn), lambda i, j: (i, zoff + j)),
        ],
        out_specs=pl.BlockSpec((tm, tn), lambda i, j: (i, j)),
        scratch_shapes=[pltpu.VMEM((tm, d), BF16)],
        compiler_params=_params(("parallel", "arbitrary")),
        name="ssm_glu",
    )(y, w, b.reshape(1, d), uz)


def _prep_weights(norm_gdn, w_in_gdn, conv_gdn, a_log_gdn, dt_bias_gdn, onorm_gdn, w_out_gdn,
                  norm_ssm, w_in_ssm, lam_re, lam_im, b_re, b_im, c_re, c_im, d_ssm, log_dt_ssm,
                  w_glu_ssm, b_glu_ssm, w_out_ssm, norm_final):
    w_in = w_in_gdn[0]
    w_main = w_in[:, :MAIN_DIM].astype(BF16)
    w_ba = jnp.pad(w_in[:, MAIN_DIM:], ((0, 0), (0, 128 - 2 * V_HEADS))).astype(BF16)
    zeros = jnp.zeros((V_HEADS,), F32)
    pad = jnp.zeros((128 - 2 * V_HEADS,), F32)
    gate_params = jnp.stack([jnp.concatenate([zeros, a_log_gdn[0], pad]),
                             jnp.concatenate([zeros, dt_bias_gdn[0], pad])])

    lr, li, ldt = lam_re[0], lam_im[0], log_dt_ssm[0]
    gp = SSM_GROUPS * SSM_STATE
    pw_r, pw_i = _ssm_pow(lr.reshape(1, gp), li.reshape(1, gp),
                          jnp.repeat(ldt, SSM_STATE).reshape(1, gp))
    bb_r, bb_i = _ssm_bbar(lr, li, ldt.reshape(SSM_GROUPS, 1),
                           jnp.swapaxes(b_re[0], 1, 2), jnp.swapaxes(b_im[0], 1, 2))
    bblk = jnp.concatenate([_block_diag(bb_r), _block_diag(bb_i)], axis=-1).astype(BF16)
    cre = _block_diag(jnp.swapaxes(c_re[0], 1, 2)).astype(BF16)
    cim = _block_diag(jnp.swapaxes(c_im[0], 1, 2)).astype(BF16)
    return dict(
        norm_gdn=norm_gdn[0], w_main=w_main, w_ba=w_ba, conv_w=conv_gdn[0], gate_params=gate_params,
        o_gain=onorm_gdn[0], w_out_gdn=w_out_gdn[0].astype(BF16),
        norm_ssm=norm_ssm[0], w_in_ssm=w_in_ssm[0].astype(BF16), pw_r=pw_r, pw_i=pw_i, bblk=bblk, cre=cre, cim=cim,
        d_skip=d_ssm[0], w_glu=w_glu_ssm[0].astype(BF16), b_glu=b_glu_ssm[0], w_out_ssm=w_out_ssm[0].astype(BF16),
        norm_final=norm_final)


def _ssm_tail(p, x1, y, uz, *, tm, tm_glu):
    y3 = _glu(y, p["w_glu"], p["b_glu"], uz, tm=tm_glu, tn=512)
    return _outproj(y3, p["w_out_ssm"], x1, p["norm_final"], tm=tm, tk=512, name="ssm_out")


def _prompt_path(p, x_prompt):
    batch, t_len, d = x_prompt.shape
    x = x_prompt.reshape(batch * t_len, d)
    proj, ba = _norm_matmul(x, p["norm_gdn"], p["w_main"], p["w_ba"], tm=1024, tn=512, name="gdn_in")
    og, delta = _gdn_prompt(proj, ba, p["conv_w"], p["gate_params"], p["o_gain"], batch=batch, t_len=t_len)
    conv_state = proj.reshape(batch, t_len, MAIN_DIM)[:, t_len - (CONV_W - 1):, :CONV_DIM]
    x1 = _outproj(og, p["w_out_gdn"], x, tm=1024, tk=512, name="gdn_out")
    uz = _norm_matmul(x1, p["norm_ssm"], p["w_in_ssm"], tm=1024, tn=512, name="ssm_in")
    y, h_re, h_im = _ssm_scan(uz, p["bblk"], p["cre"], p["cim"], p["pw_r"], p["pw_i"], p["d_skip"],
                              batch=batch, t_len=t_len)
    out = _ssm_tail(p, x1, y, uz, tm=1024, tm_glu=512)
    return (out.reshape(batch, t_len, d), conv_state[None], delta[None],
            h_re.reshape(1, batch, SSM_GROUPS, SSM_STATE), h_im.reshape(1, batch, SSM_GROUPS, SSM_STATE))


def _sample_path(p, x_sample, state_conv, state_delta, state_re, state_im):
    nb, _, d = x_sample.shape
    x = x_sample.reshape(nb, d)
    proj, ba = _norm_matmul(x, p["norm_gdn"], p["w_main"], p["w_ba"], tm=nb, tn=512, name="gdn_in_s")
    beta, eg = _gdn_gates(ba, p["gate_params"])
    xs = jnp.concatenate([state_conv[0], proj[:, None, :CONV_DIM]], axis=1)
    slots = CONV_DIM // DK
    delta, og = _gdn_decode(beta[:, :V_HEADS], eg[:, V_HEADS:2 * V_HEADS],
                            xs.reshape(nb, CONV_W, slots, DK), p["conv_w"].reshape(CONV_W, slots, DK),
                            proj[:, CONV_DIM:].reshape(nb, V_HEADS, DV), p["o_gain"], state_delta[0])
    x1 = _outproj(og.reshape(nb, VAL_DIM).astype(BF16), p["w_out_gdn"], x, tm=nb, tk=512, name="gdn_out_s")
    uz = _norm_matmul(x1, p["norm_ssm"], p["w_in_ssm"], tm=nb, tn=512, name="ssm_in_s")
    gp = SSM_GROUPS * SSM_STATE
    y, h_re, h_im = _ssm_decode(uz, p["bblk"], p["cre"], p["cim"], p["pw_r"], p["pw_i"], p["d_skip"],
                                state_re[0].reshape(nb, gp), state_im[0].reshape(nb, gp))
    out = _ssm_tail(p, x1, y, uz, tm=nb, tm_glu=nb)
    return (out.reshape(nb, 1, d), xs[:, 1:][None], delta[None],
            h_re.reshape(1, nb, SSM_GROUPS, SSM_STATE), h_im.reshape(1, nb, SSM_GROUPS, SSM_STATE))


def kernel(x_prompt, x_sample, state_gdn_conv, state_gdn_delta, state_ssm_re, state_ssm_im, norm_gdn, w_in_gdn, conv_gdn, a_log_gdn, dt_bias_gdn, onorm_gdn, w_out_gdn, norm_ssm, w_in_ssm, lam_re, lam_im, b_re, b_im, c_re, c_im, d_ssm, log_dt_ssm, w_glu_ssm, b_glu_ssm, w_out_ssm, norm_final):
    p = _prep_weights(norm_gdn, w_in_gdn, conv_gdn, a_log_gdn, dt_bias_gdn, onorm_gdn, w_out_gdn,
                      norm_ssm, w_in_ssm, lam_re, lam_im, b_re, b_im, c_re, c_im, d_ssm, log_dt_ssm,
                      w_glu_ssm, b_glu_ssm, w_out_ssm, norm_final)
    y_p, conv_p, delta_p, re_p, im_p = _prompt_path(p, x_prompt)
    y_s, conv_s, delta_s, re_s, im_s = _sample_path(p, x_sample, state_gdn_conv, state_gdn_delta,
                                                    state_ssm_re, state_ssm_im)
    return (y_p, y_s, conv_p, delta_p, re_p, im_p, conv_s, delta_s, re_s, im_s)
```

```python
import functools
import math

import jax
import jax.numpy as jnp
from jax import lax
from jax.experimental import pallas as pl
from jax.experimental.pallas import tpu as pltpu

F32 = jnp.float32
BF16 = jnp.bfloat16

RMS_EPS = 1e-6
L2_EPS = 1e-6

D_MODEL = 2048
QK_HEADS = 16
V_HEADS = 32
DK = 128
DV = 128
KEY_DIM = QK_HEADS * DK
VAL_DIM = V_HEADS * DV
CONV_DIM = 2 * KEY_DIM + VAL_DIM
CONV_W = 4
MAIN_DIM = CONV_DIM + VAL_DIM

SSM_WIDTH = 4096
SSM_GROUP = 16
SSM_GROUPS = SSM_WIDTH // SSM_GROUP
SSM_STATE = 64
GROUPS_PER_STEP = 16
CH_PER_STEP = GROUPS_PER_STEP * SSM_GROUP
ST_PER_STEP = GROUPS_PER_STEP * SSM_STATE
N_GSTEPS = SSM_GROUPS // GROUPS_PER_STEP

CHUNK = 128
SCAN_BLOCK = 256
SCAN_SEG = SCAN_BLOCK // 8
POW_ROWS = 8

VMEM_LIMIT = 56 * 1024 * 1024


def _params(sem):
    return pltpu.CompilerParams(dimension_semantics=sem, vmem_limit_bytes=VMEM_LIMIT)


def _mm(a, b):
    return jnp.dot(a.astype(BF16), b.astype(BF16), preferred_element_type=F32)


def _mm_hi(a, b):
    return jnp.dot(a, b, precision=lax.Precision.HIGHEST, preferred_element_type=F32)


def _silu(x):
    return x * jax.nn.sigmoid(x)


def _rms(x, g):
    ms = jnp.mean(x * x, axis=-1, keepdims=True)
    return x * lax.rsqrt(ms + RMS_EPS) * g


def _norm_matmul_body(x_ref, g_ref, w_ref, *rest, has_aux):
    if has_aux:
        waux_ref, o_ref, aux_ref, h_scr = rest
    else:
        o_ref, h_scr = rest

    @pl.when(pl.program_id(1) == 0)
    def _():
        h = _rms(x_ref[...], g_ref[...]).astype(BF16)
        h_scr[...] = h
        if has_aux:
            aux_ref[...] = jnp.dot(h, waux_ref[...], preferred_element_type=F32)

    o_ref[...] = jnp.dot(h_scr[...], w_ref[...].astype(BF16), preferred_element_type=F32)


def _norm_matmul(x, g, w, w_aux=None, *, n, tm, tn, name):
    m, d = x.shape
    has_aux = w_aux is not None
    in_specs = [
        pl.BlockSpec((tm, d), lambda i, j: (i, 0)),
        pl.BlockSpec((1, d), lambda i, j: (0, 0)),
        pl.BlockSpec((d, tn), lambda i, j: (0, j)),
    ]
    out_shape = [jax.ShapeDtypeStruct((m, n), F32)]
    out_specs = [pl.BlockSpec((tm, tn), lambda i, j: (i, j))]
    args = [x, g.reshape(1, d), w]
    if has_aux:
        na = w_aux.shape[1]
        in_specs.append(pl.BlockSpec((d, na), lambda i, j: (0, 0)))
        out_shape.append(jax.ShapeDtypeStruct((m, na), F32))
        out_specs.append(pl.BlockSpec((tm, na), lambda i, j: (i, 0)))
        args.append(w_aux)
    res = pl.pallas_call(
        functools.partial(_norm_matmul_body, has_aux=has_aux),
        out_shape=out_shape,
        grid=(m // tm, n // tn),
        in_specs=in_specs,
        out_specs=out_specs,
        scratch_shapes=[pltpu.VMEM((tm, d), BF16)],
        compiler_params=_params(("parallel", "arbitrary")),
        name=name,
    )(*args)
    return res if has_aux else res[0]


def _outproj_body(a_ref, w_ref, x_ref, *rest, final_norm):
    if final_norm:
        g_ref, o_ref = rest
    else:
        (o_ref,) = rest
    y = x_ref[...] + jnp.dot(a_ref[...], w_ref[...], preferred_element_type=F32)
    o_ref[...] = _rms(y, g_ref[...]) if final_norm else y


def _outproj(a, w, x, g=None, *, tm, name):
    m, kd = a.shape
    n = w.shape[1]
    final_norm = g is not None
    in_specs = [
        pl.BlockSpec((tm, kd), lambda i: (i, 0)),
        pl.BlockSpec((kd, n), lambda i: (0, 0), pipeline_mode=pl.Buffered(1)),
        pl.BlockSpec((tm, n), lambda i: (i, 0)),
    ]
    args = [a, w, x]
    if final_norm:
        in_specs.append(pl.BlockSpec((1, n), lambda i: (0, 0)))
        args.append(g.reshape(1, n))
    return pl.pallas_call(
        functools.partial(_outproj_body, final_norm=final_norm),
        out_shape=jax.ShapeDtypeStruct((m, n), F32),
        grid=(m // tm,),
        in_specs=in_specs,
        out_specs=pl.BlockSpec((tm, n), lambda i: (i, 0)),
        compiler_params=_params(("parallel",)),
        name=name,
    )(*args)


def _conv_silu_chunk(x_ref, cw, n):
    cc = x_ref.shape[1]
    r0 = pl.multiple_of(n * CHUNK, CHUNK)
    cur = x_ref[pl.ds(r0, CHUNK), :]
    p0 = pl.multiple_of(jnp.maximum(r0 - 8, 0), 8)
    prev = jnp.where(n > 0, x_ref[pl.ds(p0, 8), :], 0.0)
    row8 = lax.broadcasted_iota(jnp.int32, (8, cc), 0)
    y = cur * cw[CONV_W - 1:CONV_W, :]
    for s in range(1, CONV_W):
        sh = pltpu.roll(cur, s, 0)
        top = jnp.where(row8 < s, pltpu.roll(prev, s, 0), sh[0:8, :])
        sh = jnp.concatenate([top, sh[8:, :]], axis=0)
        y = y + sh * cw[CONV_W - 1 - s:CONV_W - s, :]
    return _silu(y)


INV_BASE_SHIFT = 3
V_PER_QK = V_HEADS // QK_HEADS
HQ_PER_STEP = 2
PH1_CHUNKS = 2


def _split2(x):
    hi = x.astype(BF16)
    return hi, (x - hi.astype(F32)).astype(BF16)


def _split3(x):
    hi = x.astype(BF16)
    r = x - hi.astype(F32)
    mid = r.astype(BF16)
    return hi, mid, (r - mid.astype(F32)).astype(BF16)


def _dotb(a, b):
    return jnp.dot(a, b, preferred_element_type=F32)


def _level_masks(row, col):
    same = [(row >> s) == (col >> s) for s in range(INV_BASE_SHIFT, CHUNK.bit_length())]
    masks = [jnp.where(same[0], 1.0, 0.0).astype(F32)]
    for lo, hi in zip(same[:-1], same[1:]):
        masks.append(jnp.where(hi, 1.0, 0.0).astype(F32) - jnp.where(lo, 1.0, 0.0).astype(F32))
    return masks


def _inv_unit_lower(mats, eye, masks):
    n = range(len(mats))
    d = [mats[i] * masks[0] for i in n]
    db = [d[i].astype(BF16) for i in n]
    t = [eye - d[i] for i in n]
    p = [_dotb(db[i], db[i]) for i in n]
    t = [t[i] + _dotb(t[i].astype(BF16), p[i].astype(BF16)) for i in n]
    p = [_dotb(p[i].astype(BF16), p[i].astype(BF16)) for i in n]
    t = [t[i] + _dotb(t[i].astype(BF16), p[i].astype(BF16)) for i in n]
    for m in masks[1:]:
        tb = [t[i].astype(BF16) for i in n]
        x = [_dotb(tb[i], (mats[i] * m).astype(BF16)) for i in n]
        t = [t[i] - _dotb(x[i].astype(BF16), tb[i]) for i in n]
    a_sp = [_split2(mats[i]) for i in n]
    t_sp = [_split2(t[i]) for i in n]
    at = [_dotb(a_sp[i][0], t_sp[i][0]) for i in n]
    at = [at[i] + _dotb(a_sp[i][0], t_sp[i][1]) for i in n]
    at = [at[i] + _dotb(a_sp[i][1], t_sp[i][0]) for i in n]
    resid = [(eye - t[i]) - at[i] for i in n]
    return [t[i] + _dotb(t_sp[i][0], resid[i].astype(BF16)) for i in n]


def _gdn_prompt_body(q_ref, k_ref, v_ref, z_ref, ba_ref, cwq_ref, cwk_ref, cwv_ref, gp_ref, on_ref,
                     og_ref, s_ref, u_s, w_s, qd_s, at_s, kdt_s, gl_s):
    hq = pl.program_id(1)
    t_len = q_ref.shape[0]
    n_chunks = t_len // CHUNK
    row = lax.broadcasted_iota(jnp.int32, (CHUNK, CHUNK), 0)
    col = lax.broadcasted_iota(jnp.int32, (CHUNK, CHUNK), 1)
    causal = row >= col
    strict = row > col
    tril = jnp.where(causal, 1.0, 0.0).astype(BF16)
    eye = jnp.where(row == col, 1.0, 0.0).astype(F32)
    masks = _level_masks(row, col)
    cwq = cwq_ref[...]
    cwk = cwk_ref[...]
    cwv = cwv_ref[...]
    neg_a = -jnp.exp(gp_ref[0:1, :])
    dt_bias = gp_ref[1:2, :]

    def chunk_prep(n):
        r0 = pl.multiple_of(n * CHUNK, CHUNK)
        rows = pl.ds(r0, CHUNK)
        qc = _conv_silu_chunk(q_ref, cwq, n)
        kc = _conv_silu_chunk(k_ref, cwk, n)
        vc = _conv_silu_chunk(v_ref, cwv, n)
        ba = ba_ref[rows, :]
        beta_all = jax.nn.sigmoid(ba)
        xa = ba + dt_bias
        g_all = neg_a * (jnp.maximum(xa, 0.0) + jnp.log1p(jnp.exp(-jnp.abs(xa))))
        g_hi, g_mid, g_lo = _split3(g_all)
        gcum = _dotb(tril, g_hi) + (_dotb(tril, g_mid) + _dotb(tril, g_lo))
        gcum_t = gcum.T
        problems = []
        for q in range(HQ_PER_STEP):
            qh = qc[:, q * DK:(q + 1) * DK]
            kh = kc[:, q * DK:(q + 1) * DK]
            qn = qh * lax.rsqrt(jnp.sum(qh * qh, axis=-1, keepdims=True) + L2_EPS) * (DK ** -0.5)
            kn = kh * lax.rsqrt(jnp.sum(kh * kh, axis=-1, keepdims=True) + L2_EPS)
            kn_t = kn.T
            kn_tb = kn_t.astype(BF16)
            kk = _dotb(kn.astype(BF16), kn_tb)
            qk = _dotb(qn.astype(BF16), kn_tb)
            for j in range(V_PER_QK):
                h = q * V_PER_QK + j
                lane_b = (hq * HQ_PER_STEP + q) * V_PER_QK + j
                lane_a = V_HEADS + lane_b
                gcol = jnp.sum(jnp.where(col == lane_a, gcum, 0.0), axis=1, keepdims=True)
                bcol = jnp.sum(jnp.where(col == lane_b, beta_all, 0.0), axis=1, keepdims=True)
                grow = jnp.sum(jnp.where(row == lane_a, gcum_t, 0.0), axis=0, keepdims=True)
                glast = gcol[CHUNK - 1:CHUNK, :]
                decay = jnp.exp(jnp.where(causal, gcol - grow, -jnp.inf))
                a_mat = jnp.where(strict, kk * decay, 0.0) * bcol
                eg = jnp.exp(gcol)
                rhs = jnp.concatenate([vc[:, h * DV:(h + 1) * DV] * bcol, kn * (bcol * eg)], axis=1).astype(BF16)
                qd_s[h, rows, :] = (qn * eg).astype(BF16)
                at_s[h, rows, :] = (qk * decay).astype(BF16)
                kdt_s[h, rows, :] = (kn_t * jnp.exp(glast - grow)).astype(BF16)
                gl_s[h, pl.ds(n, 1), :] = jnp.broadcast_to(jnp.exp(glast), (1, DV))
                problems.append((h, rows, a_mat, rhs))
        return problems

    def phase1(n2, carry):
        problems = []
        for c in range(PH1_CHUNKS):
            problems += chunk_prep(PH1_CHUNKS * n2 + c)
        t_mats = _inv_unit_lower([pr[2] for pr in problems], eye, masks)
        sols = [_dotb(t_mats[i].astype(BF16), problems[i][3]) for i in range(len(problems))]
        for (h, rows, _, _), sol in zip(problems, sols):
            u_s[h, rows, :] = sol[:, :DV]
            w_s[h, rows, :] = sol[:, DV:].astype(BF16)
        return carry

    lax.fori_loop(0, n_chunks // PH1_CHUNKS, phase1, 0)

    gain = on_ref[...]
    heads = range(HQ_PER_STEP * V_PER_QK)

    def phase2(n, states):
        rows = pl.ds(pl.multiple_of(n * CHUNK, CHUNK), CHUNK)
        wq = [jnp.concatenate([w_s[h, rows, :], qd_s[h, rows, :]], axis=0) for h in heads]
        ws_qs = [_dotb(wq[h], states[h].astype(BF16)) for h in heads]
        v_new = [(u_s[h, rows, :] - ws_qs[h][:CHUNK]).astype(BF16) for h in heads]
        o = [ws_qs[h][CHUNK:] + _dotb(at_s[h, rows, :], v_new[h]) for h in heads]
        s_new = [states[h] * gl_s[h, pl.ds(n, 1), :] + _dotb(kdt_s[h, rows, :], v_new[h]) for h in heads]
        for h in heads:
            zg = z_ref[rows, h * DV:(h + 1) * DV]
            og = o[h] * lax.rsqrt(jnp.mean(o[h] * o[h], axis=-1, keepdims=True) + RMS_EPS) * gain * _silu(zg)
            og_ref[rows, h * DV:(h + 1) * DV] = og.astype(BF16)
        return tuple(s_new)

    zero = jnp.zeros((DK, DV), F32)
    final = lax.fori_loop(0, n_chunks, phase2, tuple(zero for _ in heads))
    for h in heads:
        s_ref[0, h] = final[h]


def _gdn_prompt(proj, ba, conv_w, gate_params, o_gain, *, batch, t_len):
    wq = HQ_PER_STEP * DK
    wv = HQ_PER_STEP * V_PER_QK * DV
    nv = HQ_PER_STEP * V_PER_QK
    in_specs = [
        pl.BlockSpec((t_len, wq), lambda b, h: (b, h)),
        pl.BlockSpec((t_len, wq), lambda b, h: (b, KEY_DIM // wq + h)),
        pl.BlockSpec((t_len, wv), lambda b, h: (b, (2 * KEY_DIM) // wv + h)),
        pl.BlockSpec((t_len, wv), lambda b, h: (b, CONV_DIM // wv + h)),
        pl.BlockSpec((t_len, 128), lambda b, h: (b, 0)),
        pl.BlockSpec((CONV_W, wq), lambda b, h: (0, h)),
        pl.BlockSpec((CONV_W, wq), lambda b, h: (0, KEY_DIM // wq + h)),
        pl.BlockSpec((CONV_W, wv), lambda b, h: (0, (2 * KEY_DIM) // wv + h)),
        pl.BlockSpec((2, 128), lambda b, h: (0, 0)),
        pl.BlockSpec((1, DV), lambda b, h: (0, 0)),
    ]
    out_shape = [
        jax.ShapeDtypeStruct((batch * t_len, VAL_DIM), BF16),
        jax.ShapeDtypeStruct((batch, V_HEADS, DK, DV), F32),
    ]
    out_specs = [
        pl.BlockSpec((t_len, wv), lambda b, h: (b, h)),
        pl.BlockSpec((1, nv, DK, DV), lambda b, h: (b, h, 0, 0)),
    ]
    scratch = [
        pltpu.VMEM((nv, t_len, DV), F32),
        pltpu.VMEM((nv, t_len, DK), BF16),
        pltpu.VMEM((nv, t_len, DK), BF16),
        pltpu.VMEM((nv, t_len, CHUNK), BF16),
        pltpu.VMEM((nv, t_len, CHUNK), BF16),
        pltpu.VMEM((nv, t_len // CHUNK, DV), F32),
    ]
    return pl.pallas_call(
        _gdn_prompt_body,
        out_shape=out_shape,
        grid=(batch, QK_HEADS // HQ_PER_STEP),
        in_specs=in_specs,
        out_specs=out_specs,
        scratch_shapes=scratch,
        compiler_params=_params(("parallel", "arbitrary")),
        name="gdn_prompt",
    )(proj, proj, proj, proj, ba, conv_w, conv_w, conv_w, gate_params, o_gain.reshape(1, DV))


def _gdn_gates_body(ba_ref, gp_ref, beta_ref, eg_ref):
    ba = ba_ref[...]
    beta_ref[...] = jax.nn.sigmoid(ba)
    xa = ba + gp_ref[1:2, :]
    g = -jnp.exp(gp_ref[0:1, :]) * (jnp.maximum(xa, 0.0) + jnp.log1p(jnp.exp(-jnp.abs(xa))))
    eg_ref[...] = jnp.exp(g)


def _gdn_gates(ba, gate_params):
    m = ba.shape[0]
    return pl.pallas_call(
        _gdn_gates_body,
        out_shape=[jax.ShapeDtypeStruct((m, 128), F32)] * 2,
        name="gdn_gates",
    )(ba, gate_params)


def _gdn_decode_body(beta_ref, eg_ref, xs_ref, cw_ref, z_ref, on_ref, s_ref, so_ref, og_ref):
    b = pl.program_id(0)
    y = xs_ref[0, 0] * cw_ref[0]
    for j in range(1, CONV_W):
        y = y + xs_ref[0, j] * cw_ref[j]
    y = _silu(y)
    q = y[0:QK_HEADS]
    k = y[QK_HEADS:2 * QK_HEADS]
    v = y[2 * QK_HEADS:]
    qn = q * lax.rsqrt(jnp.sum(q * q, axis=-1, keepdims=True) + L2_EPS) * (DK ** -0.5)
    kn = k * lax.rsqrt(jnp.sum(k * k, axis=-1, keepdims=True) + L2_EPS)
    qk_t = jnp.concatenate([qn, kn, jnp.zeros((DK - 2 * QK_HEADS, DK), F32)], axis=0).T
    gain = on_ref[...]
    z = z_ref[0]
    for hv in range(V_HEADS):
        hq = hv // (V_HEADS // QK_HEADS)
        qcol = qk_t[:, hq:hq + 1]
        kcol = qk_t[:, QK_HEADS + hq:QK_HEADS + hq + 1]
        sd = s_ref[0, hv] * eg_ref[b, hv]
        ks = jnp.sum(sd * kcol, axis=0, keepdims=True)
        v_new = beta_ref[b, hv] * (v[hv:hv + 1, :] - ks)
        sn = sd + kcol * v_new
        so_ref[0, hv] = sn
        o = jnp.sum(sn * qcol, axis=0, keepdims=True)
        og = o * lax.rsqrt(jnp.mean(o * o, axis=-1, keepdims=True) + RMS_EPS) * gain * _silu(z[hv:hv + 1, :])
        og_ref[0, hv:hv + 1, :] = og


def _gdn_decode(beta, eg, xs, conv_w3, z3, o_gain, state):
    nb = xs.shape[0]
    slots = CONV_DIM // DK
    smem = pl.BlockSpec(memory_space=pltpu.SMEM)
    return pl.pallas_call(
        _gdn_decode_body,
        out_shape=[
            jax.ShapeDtypeStruct((nb, V_HEADS, DK, DV), F32),
            jax.ShapeDtypeStruct((nb, V_HEADS, DV), F32),
        ],
        grid=(nb,),
        in_specs=[
            smem,
            smem,
            pl.BlockSpec((1, CONV_W, slots, DK), lambda b: (b, 0, 0, 0)),
            pl.BlockSpec((CONV_W, slots, DK), lambda b: (0, 0, 0)),
            pl.BlockSpec((1, V_HEADS, DV), lambda b: (b, 0, 0)),
            pl.BlockSpec((1, DV), lambda b: (0, 0)),
            pl.BlockSpec((1, V_HEADS, DK, DV), lambda b: (b, 0, 0, 0)),
        ],
        out_specs=[
            pl.BlockSpec((1, V_HEADS, DK, DV), lambda b: (b, 0, 0, 0)),
            pl.BlockSpec((1, V_HEADS, DV), lambda b: (b, 0, 0)),
        ],
        compiler_params=_params(("parallel",)),
        name="gdn_decode",
    )(beta, eg, xs, conv_w3, z3, o_gain.reshape(1, DV), state)


def _ssm_pow_body(lr_ref, li_ref, ldt_ref, pr_ref, pi_ref):
    n = jnp.where(lax.broadcasted_iota(jnp.int32, pr_ref.shape, 0) == 0, 1.0, float(SCAN_SEG))
    dt = jnp.exp(ldt_ref[...])
    lr = jnp.minimum(lr_ref[...], -1e-4) * dt
    li = li_ref[...] * dt
    mag = jnp.exp(n * lr)
    pr_ref[...] = mag * jnp.cos(n * li)
    pi_ref[...] = mag * jnp.sin(n * li)


def _ssm_pow(lr, li, ldt):
    gp = lr.shape[1]
    tl = 2048
    spec = pl.BlockSpec((1, tl), lambda i: (0, i))
    ospec = pl.BlockSpec((POW_ROWS, tl), lambda i: (0, i))
    return pl.pallas_call(
        _ssm_pow_body,
        out_shape=[jax.ShapeDtypeStruct((POW_ROWS, gp), F32)] * 2,
        grid=(gp // tl,),
        in_specs=[spec, spec, spec],
        out_specs=[ospec, ospec],
        name="ssm_pow",
    )(lr, li, ldt)


def _ssm_blocks_body(lr_ref, li_ref, ldt_ref, br_ref, bi_ref, cr_ref, ci_ref, bblk_ref, cre_ref, cim_ref):
    dt = jnp.exp(ldt_ref[...])
    lr = jnp.minimum(lr_ref[...], -1e-4)
    li = li_ref[...]
    mag = jnp.exp(lr * dt)
    xr = mag * jnp.cos(li * dt) - 1.0
    xi = mag * jnp.sin(li * dt)
    den = 1.0 / (lr * lr + li * li)
    fr = (xr * lr + xi * li) * den
    fi = (xi * lr - xr * li) * den
    bblk_ref[...] = jnp.zeros_like(bblk_ref)
    cre_ref[...] = jnp.zeros_like(cre_ref)
    cim_ref[...] = jnp.zeros_like(cim_ref)
    for g in range(GROUPS_PER_STEP):
        rows = slice(g * SSM_GROUP, (g + 1) * SSM_GROUP)
        cols = slice(g * SSM_STATE, (g + 1) * SSM_STATE)
        cols_im = slice(ST_PER_STEP + g * SSM_STATE, ST_PER_STEP + (g + 1) * SSM_STATE)
        br = br_ref[g]
        bi = bi_ref[g]
        frg = fr[g:g + 1, :]
        fig = fi[g:g + 1, :]
        bblk_ref[rows, cols] = (frg * br - fig * bi).astype(BF16)
        bblk_ref[rows, cols_im] = (frg * bi + fig * br).astype(BF16)
        cre_ref[rows, cols] = cr_ref[g].astype(BF16)
        cim_ref[rows, cols] = ci_ref[g].astype(BF16)


def _ssm_blocks(lr, li, ldt, bt_re, bt_im, c_re, c_im):
    g, p = lr.shape
    k = GROUPS_PER_STEP
    s2 = pl.BlockSpec((k, p), lambda i: (i, 0))
    s1 = pl.BlockSpec((k, 1), lambda i: (i, 0))
    s3 = pl.BlockSpec((k, SSM_GROUP, p), lambda i: (i, 0, 0))
    return pl.pallas_call(
        _ssm_blocks_body,
        out_shape=[jax.ShapeDtypeStruct((g // k, CH_PER_STEP, 2 * ST_PER_STEP), BF16),
                   jax.ShapeDtypeStruct((g // k, CH_PER_STEP, ST_PER_STEP), BF16),
                   jax.ShapeDtypeStruct((g // k, CH_PER_STEP, ST_PER_STEP), BF16)],
        grid=(g // k,),
        in_specs=[s2, s2, s1, s3, s3, s3, s3],
        out_specs=[pl.BlockSpec((None, CH_PER_STEP, 2 * ST_PER_STEP), lambda i: (i, 0, 0)),
                   pl.BlockSpec((None, CH_PER_STEP, ST_PER_STEP), lambda i: (i, 0, 0)),
                   pl.BlockSpec((None, CH_PER_STEP, ST_PER_STEP), lambda i: (i, 0, 0))],
        name="ssm_blocks",
    )(lr, li, ldt, bt_re, bt_im, c_re, c_im)


def _dot_nt(a, b):
    return lax.dot_general(a, b, (((1,), (1,)), ((), ())), preferred_element_type=F32)


def _gelu_tanh(x):
    return 0.5 * x * (1.0 + jnp.tanh(math.sqrt(2.0 / math.pi) * (x + 0.044715 * (x * x * x))))


def _ssm_scan_body(ua_ref, ub_ref, b_ref, cr_ref, ci_ref, pr_ref, pi_ref, d_ref, y_ref, hr_ref, hi_ref,
                   h_s, y_s, cr_s, ci_s):
    tb = pl.program_id(2)
    nt = ST_PER_STEP // 128
    u_perm = jnp.concatenate(
        [jnp.concatenate([ua_ref[pl.ds(i, 8, stride=SCAN_SEG), :], ub_ref[pl.ds(i, 8, stride=SCAN_SEG), :]], axis=1)
         for i in range(SCAN_SEG)], axis=0)
    bu = jnp.dot(u_perm.astype(BF16), b_ref[...], preferred_element_type=F32)
    for c in range(2 * nt):
        h_s[c] = bu[:, c * 128:(c + 1) * 128]

    @pl.when(tb == 0)
    def _():
        cr_s[...] = jnp.zeros_like(cr_s)
        ci_s[...] = jnp.zeros_like(ci_s)

    row8 = lax.broadcasted_iota(jnp.int32, (8, 128), 0)
    for c in range(nt):
        lanes = slice(c * 128, (c + 1) * 128)
        ar = jnp.broadcast_to(pr_ref[0:1, lanes], (8, 128))
        ai = jnp.broadcast_to(pi_ref[0:1, lanes], (8, 128))
        hr = jnp.zeros((8, 128), F32)
        hi = jnp.zeros((8, 128), F32)
        for i in range(SCAN_SEG):
            xr = h_s[c, i * 8:(i + 1) * 8, :]
            xi = h_s[nt + c, i * 8:(i + 1) * 8, :]
            hr, hi = ar * hr - ai * hi + xr, ar * hi + ai * hr + xi

        asr = pr_ref[1:2, lanes]
        asi = pi_ref[1:2, lanes]
        c_r = cr_s[:, lanes]
        c_i = ci_s[:, lanes]
        cin_r = jnp.zeros((8, 128), F32)
        cin_i = jnp.zeros((8, 128), F32)
        for s in range(8):
            cin_r = jnp.where(row8 == s, c_r, cin_r)
            cin_i = jnp.where(row8 == s, c_i, cin_i)
            e_r = hr[s:s + 1, :]
            e_i = hi[s:s + 1, :]
            c_r, c_i = e_r + asr * c_r - asi * c_i, e_i + asr * c_i + asi * c_r
        cr_s[:, lanes] = c_r
        ci_s[:, lanes] = c_i

        hr, hi = cin_r, cin_i
        for i in range(SCAN_SEG):
            xr = h_s[c, i * 8:(i + 1) * 8, :]
            xi = h_s[nt + c, i * 8:(i + 1) * 8, :]
            hr, hi = ar * hr - ai * hi + xr, ar * hi + ai * hr + xi
            h_s[c, i * 8:(i + 1) * 8, :] = hr
            h_s[nt + c, i * 8:(i + 1) * 8, :] = hi

    h_re = jnp.concatenate([h_s[c] for c in range(nt)], axis=1)
    h_im = jnp.concatenate([h_s[nt + c] for c in range(nt)], axis=1)
    y = _dot_nt(h_re.astype(BF16), cr_ref[...]) - _dot_nt(h_im.astype(BF16), ci_ref[...])
    y_s[0] = y[:, :128]
    y_s[1] = y[:, 128:]
    per_seg = SCAN_SEG // 8
    y_nat = jnp.concatenate(
        [jnp.concatenate([y_s[h, pl.ds((j % per_seg) * 64 + j // per_seg, 8, stride=8), :] for h in range(2)], axis=1)
         for j in range(SCAN_BLOCK // 8)], axis=0)
    u = jnp.concatenate([ua_ref[...], ub_ref[...]], axis=1)
    y_ref[...] = _gelu_tanh(y_nat + d_ref[...] * u)

    @pl.when(tb == pl.num_programs(2) - 1)
    def _():
        hr_ref[...] = cr_s[...]
        hi_ref[...] = ci_s[...]


def _ssm_scan(uz, bblk, cre, cim, pw_r, pw_i, d_skip, *, batch, t_len):
    nt = t_len // SCAN_BLOCK
    return pl.pallas_call(
        _ssm_scan_body,
        out_shape=[
            jax.ShapeDtypeStruct((batch * t_len, SSM_WIDTH), F32),
            jax.ShapeDtypeStruct((batch, 1, SSM_GROUPS * SSM_STATE), F32),
            jax.ShapeDtypeStruct((batch, 1, SSM_GROUPS * SSM_STATE), F32),
        ],
        grid=(batch, N_GSTEPS, nt),
        in_specs=[
            pl.BlockSpec((SCAN_BLOCK, 128), lambda b, g, t: (b * nt + t, 2 * g)),
            pl.BlockSpec((SCAN_BLOCK, 128), lambda b, g, t: (b * nt + t, 2 * g + 1)),
            pl.BlockSpec((None, CH_PER_STEP, 2 * ST_PER_STEP), lambda b, g, t: (g, 0, 0)),
            pl.BlockSpec((None, CH_PER_STEP, ST_PER_STEP), lambda b, g, t: (g, 0, 0)),
            pl.BlockSpec((None, CH_PER_STEP, ST_PER_STEP), lambda b, g, t: (g, 0, 0)),
            pl.BlockSpec((POW_ROWS, ST_PER_STEP), lambda b, g, t: (0, g)),
            pl.BlockSpec((POW_ROWS, ST_PER_STEP), lambda b, g, t: (0, g)),
            pl.BlockSpec((1, CH_PER_STEP), lambda b, g, t: (0, g)),
        ],
        out_specs=[
            pl.BlockSpec((SCAN_BLOCK, CH_PER_STEP), lambda b, g, t: (b * nt + t, g)),
            pl.BlockSpec((None, 1, ST_PER_STEP), lambda b, g, t: (b, 0, g)),
            pl.BlockSpec((None, 1, ST_PER_STEP), lambda b, g, t: (b, 0, g)),
        ],
        scratch_shapes=[
            pltpu.VMEM((2 * ST_PER_STEP // 128, SCAN_BLOCK, 128), F32),
            pltpu.VMEM((CH_PER_STEP // 128, SCAN_BLOCK, 128), F32),
            pltpu.VMEM((1, ST_PER_STEP), F32),
            pltpu.VMEM((1, ST_PER_STEP), F32),
        ],
        compiler_params=_params(("parallel", "parallel", "arbitrary")),
        name="ssm_scan",
    )(uz, uz, bblk, cre, cim, pw_r, pw_i, d_skip.reshape(1, SSM_WIDTH))


def _ssm_decode_body(u_ref, b_ref, cr_ref, ci_ref, pr_ref, pi_ref, d_ref, h0r_ref, h0i_ref,
                     y_ref, hr_ref, hi_ref):
    nst = ST_PER_STEP
    u = u_ref[...]
    bu = jnp.dot(u.astype(BF16), b_ref[...], preferred_element_type=F32)
    ar = pr_ref[0:1, :]
    ai = pi_ref[0:1, :]
    h0r = h0r_ref[...]
    h0i = h0i_ref[...]
    hr = bu[:, 0:nst] + ar * h0r - ai * h0i
    hi = bu[:, nst:2 * nst] + ar * h0i + ai * h0r
    hr_ref[...] = hr
    hi_ref[...] = hi
    y = _dot_nt(hr.astype(BF16), cr_ref[...]) - _dot_nt(hi.astype(BF16), ci_ref[...])
    y_ref[...] = _gelu_tanh(y + d_ref[...] * u)


def _ssm_decode(uz, bblk, cre, cim, pw_r, pw_i, d_skip, h0r, h0i):
    nb = uz.shape[0]
    return pl.pallas_call(
        _ssm_decode_body,
        out_shape=[
            jax.ShapeDtypeStruct((nb, SSM_WIDTH), F32),
            jax.ShapeDtypeStruct((nb, SSM_GROUPS * SSM_STATE), F32),
            jax.ShapeDtypeStruct((nb, SSM_GROUPS * SSM_STATE), F32),
        ],
        grid=(N_GSTEPS,),
        in_specs=[
            pl.BlockSpec((nb, CH_PER_STEP), lambda g: (0, g)),
            pl.BlockSpec((None, CH_PER_STEP, 2 * ST_PER_STEP), lambda g: (g, 0, 0)),
            pl.BlockSpec((None, CH_PER_STEP, ST_PER_STEP), lambda g: (g, 0, 0)),
            pl.BlockSpec((None, CH_PER_STEP, ST_PER_STEP), lambda g: (g, 0, 0)),
            pl.BlockSpec((POW_ROWS, ST_PER_STEP), lambda g: (0, g)),
            pl.BlockSpec((POW_ROWS, ST_PER_STEP), lambda g: (0, g)),
            pl.BlockSpec((1, CH_PER_STEP), lambda g: (0, g)),
            pl.BlockSpec((nb, ST_PER_STEP), lambda g: (0, g)),
            pl.BlockSpec((nb, ST_PER_STEP), lambda g: (0, g)),
        ],
        out_specs=[
            pl.BlockSpec((nb, CH_PER_STEP), lambda g: (0, g)),
            pl.BlockSpec((nb, ST_PER_STEP), lambda g: (0, g)),
            pl.BlockSpec((nb, ST_PER_STEP), lambda g: (0, g)),
        ],
        compiler_params=_params(("parallel",)),
        name="ssm_decode",
    )(uz, bblk, cre, cim, pw_r, pw_i, d_skip.reshape(1, SSM_WIDTH), h0r, h0i)


def _glu_body(y_ref, w_ref, b_ref, z_ref, o_ref, yb_scr, *, tn):
    j = pl.program_id(1)

    @pl.when(j == 0)
    def _():
        yb_scr[...] = y_ref[...].astype(BF16)

    t = jnp.dot(yb_scr[...], w_ref[...], preferred_element_type=F32) + b_ref[...]
    y = y_ref[:, pl.ds(pl.multiple_of(j * tn, tn), tn)]
    o_ref[...] = (y * jax.nn.sigmoid(t) * _silu(z_ref[...])).astype(BF16)


def _glu(y, w, b, uz, *, tm, tn):
    m, d = y.shape
    zoff = SSM_WIDTH // tn
    return pl.pallas_call(
        functools.partial(_glu_body, tn=tn),
        out_shape=jax.ShapeDtypeStruct((m, d), BF16),
        grid=(m // tm, d // tn),
        in_specs=[
            pl.BlockSpec((tm, d), lambda i, j: (i, 0)),
            pl.BlockSpec((d, tn), lambda i, j: (0, j)),
            pl.BlockSpec((1, tn), lambda i, j: (0, j)),
            pl.BlockSpec((tm, tn), lambda i, j: (i, zoff + j)),
        ],
        out_specs=pl.BlockSpec((tm, tn), lambda i, j: (i, j)),
        scratch_shapes=[pltpu.VMEM((tm, d), BF16)],
        compiler_params=_params(("parallel", "arbitrary")),
        name="ssm_glu",
    )(y, w, b.reshape(1, d), uz)


def _prep_weights(norm_gdn, w_in_gdn, conv_gdn, a_log_gdn, dt_bias_gdn, onorm_gdn, w_out_gdn,
                  norm_ssm, w_in_ssm, lam_re, lam_im, b_re, b_im, c_re, c_im, d_ssm, log_dt_ssm,
                  w_glu_ssm, b_glu_ssm, w_out_ssm, norm_final):
    w_in = w_in_gdn[0]
    w_ba = jnp.pad(w_in[:, MAIN_DIM:], ((0, 0), (0, 128 - 2 * V_HEADS))).astype(BF16)
    zeros = jnp.zeros((V_HEADS,), F32)
    pad = jnp.zeros((128 - 2 * V_HEADS,), F32)
    gate_params = jnp.stack([jnp.concatenate([zeros, a_log_gdn[0], pad]),
                             jnp.concatenate([zeros, dt_bias_gdn[0], pad])])

    lr, li, ldt = lam_re[0], lam_im[0], log_dt_ssm[0]
    gp = SSM_GROUPS * SSM_STATE
    pw_r, pw_i = _ssm_pow(lr.reshape(1, gp), li.reshape(1, gp),
                          jnp.repeat(ldt, SSM_STATE).reshape(1, gp))
    bblk, cre, cim = _ssm_blocks(lr, li, ldt.reshape(SSM_GROUPS, 1),
                                 jnp.swapaxes(b_re[0], 1, 2), jnp.swapaxes(b_im[0], 1, 2), c_re[0], c_im[0])
    return dict(
        norm_gdn=norm_gdn[0], w_main=w_in, w_ba=w_ba, conv_w=conv_gdn[0], gate_params=gate_params,
        o_gain=onorm_gdn[0], w_out_gdn=w_out_gdn[0].astype(BF16),
        norm_ssm=norm_ssm[0], w_in_ssm=w_in_ssm[0], pw_r=pw_r, pw_i=pw_i, bblk=bblk, cre=cre, cim=cim,
        d_skip=d_ssm[0], w_glu=w_glu_ssm[0].astype(BF16), b_glu=b_glu_ssm[0], w_out_ssm=w_out_ssm[0].astype(BF16),
        norm_final=norm_final)


def _ssm_tail(p, x1, y, uz, *, tm, tm_glu):
    y3 = _glu(y, p["w_glu"], p["b_glu"], uz, tm=tm_glu, tn=512)
    return _outproj(y3, p["w_out_ssm"], x1, p["norm_final"], tm=tm, name="ssm_out")


def _prompt_path(p, x_prompt):
    batch, t_len, d = x_prompt.shape
    x = x_prompt.reshape(batch * t_len, d)
    proj, ba = _norm_matmul(x, p["norm_gdn"], p["w_main"], p["w_ba"], n=MAIN_DIM, tm=1024, tn=512, name="gdn_in")
    og, delta = _gdn_prompt(proj, ba, p["conv_w"], p["gate_params"], p["o_gain"], batch=batch, t_len=t_len)
    conv_state = proj.reshape(batch, t_len, MAIN_DIM)[:, t_len - (CONV_W - 1):, :CONV_DIM]
    x1 = _outproj(og, p["w_out_gdn"], x, tm=512, name="gdn_out")
    uz = _norm_matmul(x1, p["norm_ssm"], p["w_in_ssm"], n=2 * SSM_WIDTH, tm=1024, tn=512, name="ssm_in")
    y, h_re, h_im = _ssm_scan(uz, p["bblk"], p["cre"], p["cim"], p["pw_r"], p["pw_i"], p["d_skip"],
                              batch=batch, t_len=t_len)
    out = _ssm_tail(p, x1, y, uz, tm=512, tm_glu=512)
    return (out.reshape(batch, t_len, d), conv_state[None], delta[None],
            h_re.reshape(1, batch, SSM_GROUPS, SSM_STATE), h_im.reshape(1, batch, SSM_GROUPS, SSM_STATE))


def _sample_path(p, x_sample, state_conv, state_delta, state_re, state_im):
    nb, _, d = x_sample.shape
    x = x_sample.reshape(nb, d)
    proj, ba = _norm_matmul(x, p["norm_gdn"], p["w_main"], p["w_ba"], n=MAIN_DIM, tm=nb, tn=512, name="gdn_in_s")
    beta, eg = _gdn_gates(ba, p["gate_params"])
    xs = jnp.concatenate([state_conv[0], proj[:, None, :CONV_DIM]], axis=1)
    slots = CONV_DIM // DK
    delta, og = _gdn_decode(beta[:, :V_HEADS], eg[:, V_HEADS:2 * V_HEADS],
                            xs.reshape(nb, CONV_W, slots, DK), p["conv_w"].reshape(CONV_W, slots, DK),
                            proj[:, CONV_DIM:].reshape(nb, V_HEADS, DV), p["o_gain"], state_delta[0])
    x1 = _outproj(og.reshape(nb, VAL_DIM).astype(BF16), p["w_out_gdn"], x, tm=nb, name="gdn_out_s")
    uz = _norm_matmul(x1, p["norm_ssm"], p["w_in_ssm"], n=2 * SSM_WIDTH, tm=nb, tn=512, name="ssm_in_s")
    gp = SSM_GROUPS * SSM_STATE
    y, h_re, h_im = _ssm_decode(uz, p["bblk"], p["cre"], p["cim"], p["pw_r"], p["pw_i"], p["d_skip"],
                                state_re[0].reshape(nb, gp), state_im[0].reshape(nb, gp))
    out = _ssm_tail(p, x1, y, uz, tm=nb, tm_glu=nb)
    return (out.reshape(nb, 1, d), xs[:, 1:][None], delta[None],
            h_re.reshape(1, nb, SSM_GROUPS, SSM_STATE), h_im.reshape(1, nb, SSM_GROUPS, SSM_STATE))


def kernel(x_prompt, x_sample, state_gdn_conv, state_gdn_delta, state_ssm_re, state_ssm_im, norm_gdn, w_in_gdn, conv_gdn, a_log_gdn, dt_bias_gdn, onorm_gdn, w_out_gdn, norm_ssm, w_in_ssm, lam_re, lam_im, b_re, b_im, c_re, c_im, d_ssm, log_dt_ssm, w_glu_ssm, b_glu_ssm, w_out_ssm, norm_final):
    p = _prep_weights(norm_gdn, w_in_gdn, conv_gdn, a_log_gdn, dt_bias_gdn, onorm_gdn, w_out_gdn,
                      norm_ssm, w_in_ssm, lam_re, lam_im, b_re, b_im, c_re, c_im, d_ssm, log_dt_ssm,
                      w_glu_ssm, b_glu_ssm, w_out_ssm, norm_final)
    y_p, conv_p, delta_p, re_p, im_p = _prompt_path(p, x_prompt)
    y_s, conv_s, delta_s, re_s, im_s = _sample_path(p, x_sample, state_gdn_conv, state_gdn_delta,
                                                    state_ssm_re, state_ssm_im)
    return (y_p, y_s, conv_p, delta_p, re_p, im_p, conv_s, delta_s, re_s, im_s)
```

```python
import functools
import math

import jax
import jax.numpy as jnp
from jax import lax
from jax.experimental import pallas as pl
from jax.experimental.pallas import tpu as pltpu

F32 = jnp.float32
BF16 = jnp.bfloat16

RMS_EPS = 1e-6
L2_EPS = 1e-6

D_MODEL = 2048
QK_HEADS = 16
V_HEADS = 32
DK = 128
DV = 128
KEY_DIM = QK_HEADS * DK
VAL_DIM = V_HEADS * DV
CONV_DIM = 2 * KEY_DIM + VAL_DIM
CONV_W = 4
MAIN_DIM = CONV_DIM + VAL_DIM

SSM_WIDTH = 4096
SSM_GROUP = 16
SSM_GROUPS = SSM_WIDTH // SSM_GROUP
SSM_STATE = 64
GROUPS_PER_STEP = 16
CH_PER_STEP = GROUPS_PER_STEP * SSM_GROUP
ST_PER_STEP = GROUPS_PER_STEP * SSM_STATE
N_GSTEPS = SSM_GROUPS // GROUPS_PER_STEP

CHUNK = 128
SCAN_BLOCK = 1024
SCAN_SEG = SCAN_BLOCK // 8
POW_ROWS = 8

VMEM_LIMIT = 56 * 1024 * 1024


def _params(sem):
    return pltpu.CompilerParams(dimension_semantics=sem, vmem_limit_bytes=VMEM_LIMIT)


def _mm(a, b):
    return jnp.dot(a.astype(BF16), b.astype(BF16), preferred_element_type=F32)


def _mm_hi(a, b):
    return jnp.dot(a, b, precision=lax.Precision.HIGHEST, preferred_element_type=F32)


def _silu(x):
    return x * jax.nn.sigmoid(x)


def _rms(x, g):
    ms = jnp.mean(x * x, axis=-1, keepdims=True)
    return x * lax.rsqrt(ms + RMS_EPS) * g


def _norm_matmul_body(x_ref, g_ref, w_ref, *rest, has_aux):
    if has_aux:
        waux_ref, o_ref, aux_ref, h_scr = rest
    else:
        o_ref, h_scr = rest

    @pl.when(pl.program_id(1) == 0)
    def _():
        h = _rms(x_ref[...], g_ref[...]).astype(BF16)
        h_scr[...] = h
        if has_aux:
            aux_ref[...] = jnp.dot(h, waux_ref[...], preferred_element_type=F32)

    o_ref[...] = jnp.dot(h_scr[...], w_ref[...].astype(BF16), preferred_element_type=F32)


def _norm_matmul(x, g, w, w_aux=None, *, n, tm, tn, name):
    m, d = x.shape
    has_aux = w_aux is not None
    in_specs = [
        pl.BlockSpec((tm, d), lambda i, j: (i, 0)),
        pl.BlockSpec((1, d), lambda i, j: (0, 0)),
        pl.BlockSpec((d, tn), lambda i, j: (0, j)),
    ]
    out_shape = [jax.ShapeDtypeStruct((m, n), F32)]
    out_specs = [pl.BlockSpec((tm, tn), lambda i, j: (i, j))]
    args = [x, g.reshape(1, d), w]
    if has_aux:
        na = w_aux.shape[1]
        in_specs.append(pl.BlockSpec((d, na), lambda i, j: (0, 0)))
        out_shape.append(jax.ShapeDtypeStruct((m, na), F32))
        out_specs.append(pl.BlockSpec((tm, na), lambda i, j: (i, 0)))
        args.append(w_aux)
    res = pl.pallas_call(
        functools.partial(_norm_matmul_body, has_aux=has_aux),
        out_shape=out_shape,
        grid=(m // tm, n // tn),
        in_specs=in_specs,
        out_specs=out_specs,
        scratch_shapes=[pltpu.VMEM((tm, d), BF16)],
        compiler_params=_params(("parallel", "arbitrary")),
        name=name,
    )(*args)
    return res if has_aux else res[0]


def _outproj_body(a_ref, w_ref, x_ref, *rest, final_norm):
    if final_norm:
        g_ref, o_ref = rest
    else:
        (o_ref,) = rest
    y = x_ref[...] + jnp.dot(a_ref[...], w_ref[...], preferred_element_type=F32)
    o_ref[...] = _rms(y, g_ref[...]) if final_norm else y


def _outproj(a, w, x, g=None, *, tm, name):
    m, kd = a.shape
    n = w.shape[1]
    final_norm = g is not None
    in_specs = [
        pl.BlockSpec((tm, kd), lambda i: (i, 0)),
        pl.BlockSpec((kd, n), lambda i: (0, 0), pipeline_mode=pl.Buffered(1)),
        pl.BlockSpec((tm, n), lambda i: (i, 0)),
    ]
    args = [a, w, x]
    if final_norm:
        in_specs.append(pl.BlockSpec((1, n), lambda i: (0, 0)))
        args.append(g.reshape(1, n))
    return pl.pallas_call(
        functools.partial(_outproj_body, final_norm=final_norm),
        out_shape=jax.ShapeDtypeStruct((m, n), F32),
        grid=(m // tm,),
        in_specs=in_specs,
        out_specs=pl.BlockSpec((tm, n), lambda i: (i, 0)),
        compiler_params=_params(("parallel",)),
        name=name,
    )(*args)


def _conv_silu_chunk(x_ref, cw, n):
    cc = x_ref.shape[1]
    r0 = pl.multiple_of(n * CHUNK, CHUNK)
    cur = x_ref[pl.ds(r0, CHUNK), :]
    p0 = pl.multiple_of(jnp.maximum(r0 - 8, 0), 8)
    prev = jnp.where(n > 0, x_ref[pl.ds(p0, 8), :], 0.0)
    row8 = lax.broadcasted_iota(jnp.int32, (8, cc), 0)
    y = cur * cw[CONV_W - 1:CONV_W, :]
    for s in range(1, CONV_W):
        sh = pltpu.roll(cur, s, 0)
        top = jnp.where(row8 < s, pltpu.roll(prev, s, 0), sh[0:8, :])
        sh = jnp.concatenate([top, sh[8:, :]], axis=0)
        y = y + sh * cw[CONV_W - 1 - s:CONV_W - s, :]
    return _silu(y)


INV_BASE_SHIFT = 3
V_PER_QK = V_HEADS // QK_HEADS
HQ_PER_STEP = 2
DEC_GROUP = 8
PH1_CHUNKS = 2


def _split2(x):
    hi = x.astype(BF16)
    return hi, (x - hi.astype(F32)).astype(BF16)


def _split3(x):
    hi = x.astype(BF16)
    r = x - hi.astype(F32)
    mid = r.astype(BF16)
    return hi, mid, (r - mid.astype(F32)).astype(BF16)


def _dotb(a, b):
    return jnp.dot(a, b, preferred_element_type=F32)


N_LEVELS = CHUNK.bit_length() - 1 - INV_BASE_SHIFT


def _chunk_tables():
    row = lax.broadcasted_iota(jnp.int32, (CHUNK, CHUNK), 0)
    col = lax.broadcasted_iota(jnp.int32, (CHUNK, CHUNK), 1)
    same = [(row >> s) == (col >> s) for s in range(INV_BASE_SHIFT, CHUNK.bit_length())]
    tab32 = jnp.stack([row == col, same[0]]).astype(F32)
    levels = [jnp.logical_and(hi, jnp.logical_not(lo)) for lo, hi in zip(same[:-1], same[1:])]
    tab16 = jnp.stack([row >= col] + levels).astype(BF16)
    return tab32, tab16


def _inv_unit_lower(mats, tab32_ref, tab16_ref):
    n = range(len(mats))
    eye = tab32_ref[0]
    d = [mats[i] * tab32_ref[1] for i in n]
    db = [d[i].astype(BF16) for i in n]
    t = [eye - d[i] for i in n]
    p = [_dotb(db[i], db[i]) for i in n]
    t = [t[i] + _dotb(t[i].astype(BF16), p[i].astype(BF16)) for i in n]
    p = [_dotb(p[i].astype(BF16), p[i].astype(BF16)) for i in n]
    t = [t[i] + _dotb(t[i].astype(BF16), p[i].astype(BF16)) for i in n]
    a_sp = [_split2(mats[i]) for i in n]
    for lvl in range(N_LEVELS):
        tb = [t[i].astype(BF16) for i in n]
        x = [_dotb(tb[i], a_sp[i][0] * tab16_ref[1 + lvl]) for i in n]
        t = [t[i] - _dotb(x[i].astype(BF16), tb[i]) for i in n]
    t_sp = [_split2(t[i]) for i in n]
    at = [_dotb(a_sp[i][0], t_sp[i][0]) for i in n]
    at = [at[i] + _dotb(a_sp[i][0], t_sp[i][1]) for i in n]
    at = [at[i] + _dotb(a_sp[i][1], t_sp[i][0]) for i in n]
    resid = [(eye - t[i]) - at[i] for i in n]
    return [t[i] + _dotb(t_sp[i][0], resid[i].astype(BF16)) for i in n]


def _gdn_prompt_body(q_ref, k_ref, v_ref, z_ref, ba_ref, cwq_ref, cwk_ref, cwv_ref, gp_ref, on_ref,
                     tab32_ref, tab16_ref, og_ref, s_ref, u_s, w_s, qd_s, at_s, kdt_s, gl_s):
    hq = pl.program_id(1)
    t_len = q_ref.shape[0]
    n_chunks = t_len // CHUNK
    row = lax.broadcasted_iota(jnp.int32, (CHUNK, CHUNK), 0)
    col = lax.broadcasted_iota(jnp.int32, (CHUNK, CHUNK), 1)
    causal = row >= col
    strict = row > col
    cwq = cwq_ref[...]
    cwk = cwk_ref[...]
    cwv = cwv_ref[...]
    neg_a = -jnp.exp(gp_ref[0:1, :])
    dt_bias = gp_ref[1:2, :]

    def chunk_prep(n):
        r0 = pl.multiple_of(n * CHUNK, CHUNK)
        rows = pl.ds(r0, CHUNK)
        qc = _conv_silu_chunk(q_ref, cwq, n)
        kc = _conv_silu_chunk(k_ref, cwk, n)
        vc = _conv_silu_chunk(v_ref, cwv, n)
        ba = ba_ref[rows, :]
        beta_all = jax.nn.sigmoid(ba)
        xa = ba + dt_bias
        g_all = neg_a * (jnp.maximum(xa, 0.0) + jnp.log1p(jnp.exp(-jnp.abs(xa))))
        g_hi, g_mid, g_lo = _split3(g_all)
        tril = tab16_ref[0]
        gcum = _dotb(tril, g_hi) + (_dotb(tril, g_mid) + _dotb(tril, g_lo))
        gcum_t = gcum.T
        problems = []
        for q in range(HQ_PER_STEP):
            qh = qc[:, q * DK:(q + 1) * DK]
            kh = kc[:, q * DK:(q + 1) * DK]
            qn = qh * lax.rsqrt(jnp.sum(qh * qh, axis=-1, keepdims=True) + L2_EPS) * (DK ** -0.5)
            kn = kh * lax.rsqrt(jnp.sum(kh * kh, axis=-1, keepdims=True) + L2_EPS)
            kn_t = kn.T
            kn_tb = kn_t.astype(BF16)
            kk = _dotb(kn.astype(BF16), kn_tb)
            qk = _dotb(qn.astype(BF16), kn_tb)
            for j in range(V_PER_QK):
                h = q * V_PER_QK + j
                lane_b = (hq * HQ_PER_STEP + q) * V_PER_QK + j
                lane_a = V_HEADS + lane_b
                gcol = jnp.sum(jnp.where(col == lane_a, gcum, 0.0), axis=1, keepdims=True)
                bcol = jnp.sum(jnp.where(col == lane_b, beta_all, 0.0), axis=1, keepdims=True)
                grow = jnp.sum(jnp.where(row == lane_a, gcum_t, 0.0), axis=0, keepdims=True)
                glast = gcol[CHUNK - 1:CHUNK, :]
                decay = jnp.exp(jnp.where(causal, gcol - grow, -jnp.inf))
                a_mat = jnp.where(strict, kk * decay, 0.0) * bcol
                eg = jnp.exp(gcol)
                rhs = jnp.concatenate([vc[:, h * DV:(h + 1) * DV] * bcol, kn * (bcol * eg)], axis=1).astype(BF16)
                qd_s[h, rows, :] = (qn * eg).astype(BF16)
                at_s[h, rows, :] = (qk * decay).astype(BF16)
                kdt_s[h, rows, :] = (kn_t * jnp.exp(glast - grow)).astype(BF16)
                gl_s[h, pl.ds(n, 1), :] = jnp.broadcast_to(jnp.exp(glast), (1, DV))
                problems.append((h, rows, a_mat, rhs))
        return problems

    def phase1(n2, carry):
        problems = []
        for c in range(PH1_CHUNKS):
            problems += chunk_prep(PH1_CHUNKS * n2 + c)
        t_mats = _inv_unit_lower([pr[2] for pr in problems], tab32_ref, tab16_ref)
        sols = [_dotb(t_mats[i].astype(BF16), problems[i][3]) for i in range(len(problems))]
        for (h, rows, _, _), sol in zip(problems, sols):
            u_s[h, rows, :] = sol[:, :DV]
            w_s[h, rows, :] = sol[:, DV:].astype(BF16)
        return carry

    lax.fori_loop(0, n_chunks // PH1_CHUNKS, phase1, 0)

    gain = on_ref[...]
    heads = range(HQ_PER_STEP * V_PER_QK)

    def phase2(n, states):
        rows = pl.ds(pl.multiple_of(n * CHUNK, CHUNK), CHUNK)
        wq = [jnp.concatenate([w_s[h, rows, :], qd_s[h, rows, :]], axis=0) for h in heads]
        ws_qs = [_dotb(wq[h], states[h].astype(BF16)) for h in heads]
        v_new = [(u_s[h, rows, :] - ws_qs[h][:CHUNK]).astype(BF16) for h in heads]
        o = [ws_qs[h][CHUNK:] + _dotb(at_s[h, rows, :], v_new[h]) for h in heads]
        s_new = [states[h] * gl_s[h, pl.ds(n, 1), :] + _dotb(kdt_s[h, rows, :], v_new[h]) for h in heads]
        for h in heads:
            zg = z_ref[rows, h * DV:(h + 1) * DV]
            og = o[h] * lax.rsqrt(jnp.mean(o[h] * o[h], axis=-1, keepdims=True) + RMS_EPS) * gain * _silu(zg)
            og_ref[rows, h * DV:(h + 1) * DV] = og.astype(BF16)
        return tuple(s_new)

    zero = jnp.zeros((DK, DV), F32)
    final = lax.fori_loop(0, n_chunks, phase2, tuple(zero for _ in heads))
    for h in heads:
        s_ref[0, h] = final[h]


def _gdn_prompt(proj, ba, conv_w, gate_params, o_gain, *, batch, t_len):
    wq = HQ_PER_STEP * DK
    wv = HQ_PER_STEP * V_PER_QK * DV
    nv = HQ_PER_STEP * V_PER_QK
    in_specs = [
        pl.BlockSpec((t_len, wq), lambda b, h: (b, h)),
        pl.BlockSpec((t_len, wq), lambda b, h: (b, KEY_DIM // wq + h)),
        pl.BlockSpec((t_len, wv), lambda b, h: (b, (2 * KEY_DIM) // wv + h)),
        pl.BlockSpec((t_len, wv), lambda b, h: (b, CONV_DIM // wv + h)),
        pl.BlockSpec((t_len, 128), lambda b, h: (b, 0)),
        pl.BlockSpec((CONV_W, wq), lambda b, h: (0, h)),
        pl.BlockSpec((CONV_W, wq), lambda b, h: (0, KEY_DIM // wq + h)),
        pl.BlockSpec((CONV_W, wv), lambda b, h: (0, (2 * KEY_DIM) // wv + h)),
        pl.BlockSpec((2, 128), lambda b, h: (0, 0)),
        pl.BlockSpec((1, DV), lambda b, h: (0, 0)),
        pl.BlockSpec((2, CHUNK, CHUNK), lambda b, h: (0, 0, 0)),
        pl.BlockSpec((1 + N_LEVELS, CHUNK, CHUNK), lambda b, h: (0, 0, 0)),
    ]
    tab32, tab16 = _chunk_tables()
    out_shape = [
        jax.ShapeDtypeStruct((batch * t_len, VAL_DIM), BF16),
        jax.ShapeDtypeStruct((batch, V_HEADS, DK, DV), F32),
    ]
    out_specs = [
        pl.BlockSpec((t_len, wv), lambda b, h: (b, h)),
        pl.BlockSpec((1, nv, DK, DV), lambda b, h: (b, h, 0, 0)),
    ]
    scratch = [
        pltpu.VMEM((nv, t_len, DV), F32),
        pltpu.VMEM((nv, t_len, DK), BF16),
        pltpu.VMEM((nv, t_len, DK), BF16),
        pltpu.VMEM((nv, t_len, CHUNK), BF16),
        pltpu.VMEM((nv, t_len, CHUNK), BF16),
        pltpu.VMEM((nv, t_len // CHUNK, DV), F32),
    ]
    return pl.pallas_call(
        _gdn_prompt_body,
        out_shape=out_shape,
        grid=(batch, QK_HEADS // HQ_PER_STEP),
        in_specs=in_specs,
        out_specs=out_specs,
        scratch_shapes=scratch,
        compiler_params=_params(("parallel", "arbitrary")),
        name="gdn_prompt",
    )(proj, proj, proj, proj, ba, conv_w, conv_w, conv_w, gate_params, o_gain.reshape(1, DV), tab32, tab16)


def _gdn_gates_body(ba_ref, gp_ref, beta_ref, eg_ref):
    ba = ba_ref[...]
    beta_ref[...] = jax.nn.sigmoid(ba)
    xa = ba + gp_ref[1:2, :]
    g = -jnp.exp(gp_ref[0:1, :]) * (jnp.maximum(xa, 0.0) + jnp.log1p(jnp.exp(-jnp.abs(xa))))
    eg_ref[...] = jnp.exp(g)


def _gdn_gates(ba, gate_params):
    m = ba.shape[0]
    return pl.pallas_call(
        _gdn_gates_body,
        out_shape=[jax.ShapeDtypeStruct((m, 128), F32)] * 2,
        name="gdn_gates",
    )(ba, gate_params)


def _gdn_decode_body(beta_ref, eg_ref, xs_ref, cw_ref, z_ref, on_ref, s_ref, so_ref, og_ref):
    b = pl.program_id(0)
    y = xs_ref[0, 0] * cw_ref[0]
    for j in range(1, CONV_W):
        y = y + xs_ref[0, j] * cw_ref[j]
    y = _silu(y)
    q = y[0:QK_HEADS]
    k = y[QK_HEADS:2 * QK_HEADS]
    v = y[2 * QK_HEADS:]
    qn = q * lax.rsqrt(jnp.sum(q * q, axis=-1, keepdims=True) + L2_EPS) * (DK ** -0.5)
    kn = k * lax.rsqrt(jnp.sum(k * k, axis=-1, keepdims=True) + L2_EPS)
    qk_t = jnp.concatenate([qn, kn, jnp.zeros((DK - 2 * QK_HEADS, DK), F32)], axis=0).T
    for g0 in range(0, V_HEADS, DEC_GROUP):
        hs = range(g0, g0 + DEC_GROUP)
        qcol = {h: qk_t[:, h // V_PER_QK:h // V_PER_QK + 1] for h in hs}
        kcol = {h: qk_t[:, QK_HEADS + h // V_PER_QK:QK_HEADS + h // V_PER_QK + 1] for h in hs}
        sd = {h: s_ref[0, h] * eg_ref[b, h] for h in hs}
        ks = {h: jnp.sum(sd[h] * kcol[h], axis=0, keepdims=True) for h in hs}
        v_new = {h: beta_ref[b, h] * (v[h:h + 1, :] - ks[h]) for h in hs}
        sn = {h: sd[h] + kcol[h] * v_new[h] for h in hs}
        for h in hs:
            so_ref[0, h] = sn[h]
        o = {h: jnp.sum(sn[h] * qcol[h], axis=0, keepdims=True) for h in hs}
        for h in hs:
            og_ref[0, h:h + 1, :] = o[h]
    o_all = og_ref[0]
    og_ref[0] = (o_all * lax.rsqrt(jnp.mean(o_all * o_all, axis=-1, keepdims=True) + RMS_EPS)
                 * on_ref[...] * _silu(z_ref[0]))


def _gdn_decode(beta, eg, xs, conv_w3, z3, o_gain, state):
    nb = xs.shape[0]
    slots = CONV_DIM // DK
    smem = pl.BlockSpec(memory_space=pltpu.SMEM)
    return pl.pallas_call(
        _gdn_decode_body,
        out_shape=[
            jax.ShapeDtypeStruct((nb, V_HEADS, DK, DV), F32),
            jax.ShapeDtypeStruct((nb, V_HEADS, DV), F32),
        ],
        grid=(nb,),
        in_specs=[
            smem,
            smem,
            pl.BlockSpec((1, CONV_W, slots, DK), lambda b: (b, 0, 0, 0)),
            pl.BlockSpec((CONV_W, slots, DK), lambda b: (0, 0, 0)),
            pl.BlockSpec((1, V_HEADS, DV), lambda b: (b, 0, 0)),
            pl.BlockSpec((1, DV), lambda b: (0, 0)),
            pl.BlockSpec((1, V_HEADS, DK, DV), lambda b: (b, 0, 0, 0)),
        ],
        out_specs=[
            pl.BlockSpec((1, V_HEADS, DK, DV), lambda b: (b, 0, 0, 0)),
            pl.BlockSpec((1, V_HEADS, DV), lambda b: (b, 0, 0)),
        ],
        compiler_params=_params(("parallel",)),
        name="gdn_decode",
    )(beta, eg, xs, conv_w3, z3, o_gain.reshape(1, DV), state)


def _ssm_pow_body(lr_ref, li_ref, ldt_ref, pr_ref, pi_ref):
    n = jnp.where(lax.broadcasted_iota(jnp.int32, pr_ref.shape, 0) == 0, 1.0, float(SCAN_SEG))
    dt = jnp.exp(ldt_ref[...])
    lr = jnp.minimum(lr_ref[...], -1e-4) * dt
    li = li_ref[...] * dt
    mag = jnp.exp(n * lr)
    pr_ref[...] = mag * jnp.cos(n * li)
    pi_ref[...] = mag * jnp.sin(n * li)


def _ssm_pow(lr, li, ldt):
    gp = lr.shape[1]
    tl = 2048
    spec = pl.BlockSpec((1, tl), lambda i: (0, i))
    ospec = pl.BlockSpec((POW_ROWS, tl), lambda i: (0, i))
    return pl.pallas_call(
        _ssm_pow_body,
        out_shape=[jax.ShapeDtypeStruct((POW_ROWS, gp), F32)] * 2,
        grid=(gp // tl,),
        in_specs=[spec, spec, spec],
        out_specs=[ospec, ospec],
        name="ssm_pow",
    )(lr, li, ldt)


def _ssm_blocks_body(lr_ref, li_ref, ldt_ref, br_ref, bi_ref, cr_ref, ci_ref, bblk_ref, cblk_ref):
    dt = jnp.exp(ldt_ref[...])
    lr = jnp.minimum(lr_ref[...], -1e-4)
    li = li_ref[...]
    mag = jnp.exp(lr * dt)
    xr = mag * jnp.cos(li * dt) - 1.0
    xi = mag * jnp.sin(li * dt)
    den = 1.0 / (lr * lr + li * li)
    fr = (xr * lr + xi * li) * den
    fi = (xi * lr - xr * li) * den
    bblk_ref[...] = jnp.zeros_like(bblk_ref)
    cblk_ref[...] = jnp.zeros_like(cblk_ref)
    per_tile = 128 // SSM_STATE
    for g in range(GROUPS_PER_STEP):
        rows = slice(g * SSM_GROUP, (g + 1) * SSM_GROUP)
        off = (g // per_tile) * 256 + (g % per_tile) * SSM_STATE
        cols = slice(off, off + SSM_STATE)
        cols_im = slice(off + 128, off + 128 + SSM_STATE)
        br = br_ref[g]
        bi = bi_ref[g]
        frg = fr[g:g + 1, :]
        fig = fi[g:g + 1, :]
        bblk_ref[rows, cols] = (frg * br - fig * bi).astype(BF16)
        bblk_ref[rows, cols_im] = (frg * bi + fig * br).astype(BF16)
        cblk_ref[rows, cols] = cr_ref[g].astype(BF16)
        cblk_ref[rows, cols_im] = (-ci_ref[g]).astype(BF16)


def _ssm_blocks(lr, li, ldt, bt_re, bt_im, c_re, c_im):
    g, p = lr.shape
    k = GROUPS_PER_STEP
    s2 = pl.BlockSpec((k, p), lambda i: (i, 0))
    s1 = pl.BlockSpec((k, 1), lambda i: (i, 0))
    s3 = pl.BlockSpec((k, SSM_GROUP, p), lambda i: (i, 0, 0))
    blk = pl.BlockSpec((None, CH_PER_STEP, 2 * ST_PER_STEP), lambda i: (i, 0, 0))
    return pl.pallas_call(
        _ssm_blocks_body,
        out_shape=[jax.ShapeDtypeStruct((g // k, CH_PER_STEP, 2 * ST_PER_STEP), BF16)] * 2,
        grid=(g // k,),
        in_specs=[s2, s2, s1, s3, s3, s3, s3],
        out_specs=[blk, blk],
        name="ssm_blocks",
    )(lr, li, ldt, bt_re, bt_im, c_re, c_im)


def _dot_nt(a, b):
    return lax.dot_general(a, b, (((1,), (1,)), ((), ())), preferred_element_type=F32)


def _gelu_tanh(x):
    return 0.5 * x * (1.0 + jnp.tanh(math.sqrt(2.0 / math.pi) * (x + 0.044715 * (x * x * x))))


def _ssm_scan_body(ua_ref, ub_ref, b_ref, c_ref, pr_ref, pi_ref, d_ref, y_ref, hr_ref, hi_ref,
                   h_s, y_s, cr_s, ci_s):
    tb = pl.program_id(2)
    nt = ST_PER_STEP // 128
    u_perm = jnp.concatenate(
        [jnp.concatenate([ua_ref[pl.ds(i, 8, stride=SCAN_SEG), :], ub_ref[pl.ds(i, 8, stride=SCAN_SEG), :]], axis=1)
         for i in range(SCAN_SEG)], axis=0).astype(BF16)

    @pl.when(tb == 0)
    def _():
        cr_s[...] = jnp.zeros_like(cr_s)
        ci_s[...] = jnp.zeros_like(ci_s)

    def b_proj(c):
        bu = jnp.dot(u_perm, b_ref[:, c * 256:(c + 1) * 256], preferred_element_type=F32)
        h_s[c] = bu[:, :128]
        h_s[nt + c] = bu[:, 128:]

    row8 = lax.broadcasted_iota(jnp.int32, (8, 128), 0)
    b_proj(0)
    y = None
    for c in range(nt):
        if c + 1 < nt:
            b_proj(c + 1)
        lanes = slice(c * 128, (c + 1) * 128)
        ar = jnp.broadcast_to(pr_ref[0:1, lanes], (8, 128))
        ai = jnp.broadcast_to(pi_ref[0:1, lanes], (8, 128))
        hr = jnp.zeros((8, 128), F32)
        hi = jnp.zeros((8, 128), F32)
        for i in range(SCAN_SEG):
            xr = h_s[c, i * 8:(i + 1) * 8, :]
            xi = h_s[nt + c, i * 8:(i + 1) * 8, :]
            hr, hi = ar * hr - ai * hi + xr, ar * hi + ai * hr + xi

        asr = pr_ref[1:2, lanes]
        asi = pi_ref[1:2, lanes]
        c_r = cr_s[:, lanes]
        c_i = ci_s[:, lanes]
        cin_r = jnp.zeros((8, 128), F32)
        cin_i = jnp.zeros((8, 128), F32)
        for s in range(8):
            cin_r = jnp.where(row8 == s, c_r, cin_r)
            cin_i = jnp.where(row8 == s, c_i, cin_i)
            e_r = hr[s:s + 1, :]
            e_i = hi[s:s + 1, :]
            c_r, c_i = e_r + asr * c_r - asi * c_i, e_i + asr * c_i + asi * c_r
        cr_s[:, lanes] = c_r
        ci_s[:, lanes] = c_i

        hr, hi = cin_r, cin_i
        for i in range(SCAN_SEG):
            xr = h_s[c, i * 8:(i + 1) * 8, :]
            xi = h_s[nt + c, i * 8:(i + 1) * 8, :]
            hr, hi = ar * hr - ai * hi + xr, ar * hi + ai * hr + xi
            h_s[c, i * 8:(i + 1) * 8, :] = hr
            h_s[nt + c, i * 8:(i + 1) * 8, :] = hi

        h_cat = jnp.concatenate([h_s[c], h_s[nt + c]], axis=1).astype(BF16)
        y_c = _dot_nt(h_cat, c_ref[:, c * 256:(c + 1) * 256])
        y = y_c if y is None else y + y_c

    y_s[0] = y[:, :128]
    y_s[1] = y[:, 128:]
    per_seg = SCAN_SEG // 8
    y_nat = jnp.concatenate(
        [jnp.concatenate([y_s[h, pl.ds((j % per_seg) * 64 + j // per_seg, 8, stride=8), :] for h in range(2)], axis=1)
         for j in range(SCAN_BLOCK // 8)], axis=0)
    u = jnp.concatenate([ua_ref[...], ub_ref[...]], axis=1)
    y_ref[...] = _gelu_tanh(y_nat + d_ref[...] * u)

    @pl.when(tb == pl.num_programs(2) - 1)
    def _():
        hr_ref[...] = cr_s[...]
        hi_ref[...] = ci_s[...]


def _ssm_scan(uz, bblk, cblk, pw_r, pw_i, d_skip, *, batch, t_len):
    nt = t_len // SCAN_BLOCK
    return pl.pallas_call(
        _ssm_scan_body,
        out_shape=[
            jax.ShapeDtypeStruct((batch * t_len, SSM_WIDTH), F32),
            jax.ShapeDtypeStruct((batch, 1, SSM_GROUPS * SSM_STATE), F32),
            jax.ShapeDtypeStruct((batch, 1, SSM_GROUPS * SSM_STATE), F32),
        ],
        grid=(batch, N_GSTEPS, nt),
        in_specs=[
            pl.BlockSpec((SCAN_BLOCK, 128), lambda b, g, t: (b * nt + t, 2 * g)),
            pl.BlockSpec((SCAN_BLOCK, 128), lambda b, g, t: (b * nt + t, 2 * g + 1)),
            pl.BlockSpec((None, CH_PER_STEP, 2 * ST_PER_STEP), lambda b, g, t: (g, 0, 0)),
            pl.BlockSpec((None, CH_PER_STEP, 2 * ST_PER_STEP), lambda b, g, t: (g, 0, 0)),
            pl.BlockSpec((POW_ROWS, ST_PER_STEP), lambda b, g, t: (0, g)),
            pl.BlockSpec((POW_ROWS, ST_PER_STEP), lambda b, g, t: (0, g)),
            pl.BlockSpec((1, CH_PER_STEP), lambda b, g, t: (0, g)),
        ],
        out_specs=[
            pl.BlockSpec((SCAN_BLOCK, CH_PER_STEP), lambda b, g, t: (b * nt + t, g)),
            pl.BlockSpec((None, 1, ST_PER_STEP), lambda b, g, t: (b, 0, g)),
            pl.BlockSpec((None, 1, ST_PER_STEP), lambda b, g, t: (b, 0, g)),
        ],
        scratch_shapes=[
            pltpu.VMEM((2 * ST_PER_STEP // 128, SCAN_BLOCK, 128), F32),
            pltpu.VMEM((CH_PER_STEP // 128, SCAN_BLOCK, 128), F32),
            pltpu.VMEM((1, ST_PER_STEP), F32),
            pltpu.VMEM((1, ST_PER_STEP), F32),
        ],
        compiler_params=_params(("parallel", "parallel", "arbitrary")),
        name="ssm_scan",
    )(uz, uz, bblk, cblk, pw_r, pw_i, d_skip.reshape(1, SSM_WIDTH))


def _ssm_decode_body(u_ref, b_ref, c_ref, pr_ref, pi_ref, d_ref, h0r_ref, h0i_ref, y_ref, hr_ref, hi_ref):
    nt = ST_PER_STEP // 128
    u = u_ref[...]
    bu = jnp.dot(u.astype(BF16), b_ref[...], preferred_element_type=F32)
    h_cat = []
    for c in range(nt):
        lanes = slice(c * 128, (c + 1) * 128)
        ar = pr_ref[0:1, lanes]
        ai = pi_ref[0:1, lanes]
        h0r = h0r_ref[:, lanes]
        h0i = h0i_ref[:, lanes]
        hr = bu[:, c * 256:c * 256 + 128] + ar * h0r - ai * h0i
        hi = bu[:, c * 256 + 128:(c + 1) * 256] + ar * h0i + ai * h0r
        hr_ref[:, lanes] = hr
        hi_ref[:, lanes] = hi
        h_cat += [hr, hi]
    y = _dot_nt(jnp.concatenate(h_cat, axis=1).astype(BF16), c_ref[...])
    y_ref[...] = _gelu_tanh(y + d_ref[...] * u)


def _ssm_decode(uz, bblk, cblk, pw_r, pw_i, d_skip, h0r, h0i):
    nb = uz.shape[0]
    return pl.pallas_call(
        _ssm_decode_body,
        out_shape=[
            jax.ShapeDtypeStruct((nb, SSM_WIDTH), F32),
            jax.ShapeDtypeStruct((nb, SSM_GROUPS * SSM_STATE), F32),
            jax.ShapeDtypeStruct((nb, SSM_GROUPS * SSM_STATE), F32),
        ],
        grid=(N_GSTEPS,),
        in_specs=[
            pl.BlockSpec((nb, CH_PER_STEP), lambda g: (0, g)),
            pl.BlockSpec((None, CH_PER_STEP, 2 * ST_PER_STEP), lambda g: (g, 0, 0)),
            pl.BlockSpec((None, CH_PER_STEP, 2 * ST_PER_STEP), lambda g: (g, 0, 0)),
            pl.BlockSpec((POW_ROWS, ST_PER_STEP), lambda g: (0, g)),
            pl.BlockSpec((POW_ROWS, ST_PER_STEP), lambda g: (0, g)),
            pl.BlockSpec((1, CH_PER_STEP), lambda g: (0, g)),
            pl.BlockSpec((nb, ST_PER_STEP), lambda g: (0, g)),
            pl.BlockSpec((nb, ST_PER_STEP), lambda g: (0, g)),
        ],
        out_specs=[
            pl.BlockSpec((nb, CH_PER_STEP), lambda g: (0, g)),
            pl.BlockSpec((nb, ST_PER_STEP), lambda g: (0, g)),
            pl.BlockSpec((nb, ST_PER_STEP), lambda g: (0, g)),
        ],
        compiler_params=_params(("parallel",)),
        name="ssm_decode",
    )(uz, bblk, cblk, pw_r, pw_i, d_skip.reshape(1, SSM_WIDTH), h0r, h0i)


def _glu_body(y_ref, w_ref, b_ref, z_ref, o_ref, yb_scr, *, tn):
    j = pl.program_id(1)

    @pl.when(j == 0)
    def _():
        yb_scr[...] = y_ref[...].astype(BF16)

    t = jnp.dot(yb_scr[...], w_ref[...], preferred_element_type=F32) + b_ref[...]
    y = y_ref[:, pl.ds(pl.multiple_of(j * tn, tn), tn)]
    o_ref[...] = (y * jax.nn.sigmoid(t) * _silu(z_ref[...])).astype(BF16)


def _glu(y, w, b, uz, *, tm, tn):
    m, d = y.shape
    zoff = SSM_WIDTH // tn
    return pl.pallas_call(
        functools.partial(_glu_body, tn=tn),
        out_shape=jax.ShapeDtypeStruct((m, d), BF16),
        grid=(m // tm, d // tn),
        in_specs=[
            pl.BlockSpec((tm, d), lambda i, j: (i, 0)),
            pl.BlockSpec((d, tn), lambda i, j: (0, j)),
            pl.BlockSpec((1, tn), lambda i, j: (0, j)),
            pl.BlockSpec((tm, tn), lambda i, j: (i, zoff + j)),
        ],
        out_specs=pl.BlockSpec((tm, tn), lambda i, j: (i, j)),
        scratch_shapes=[pltpu.VMEM((tm, d), BF16)],
        compiler_params=_params(("parallel", "arbitrary")),
        name="ssm_glu",
    )(y, w, b.reshape(1, d), uz)


def _prep_weights(norm_gdn, w_in_gdn, conv_gdn, a_log_gdn, dt_bias_gdn, onorm_gdn, w_out_gdn,
                  norm_ssm, w_in_ssm, lam_re, lam_im, b_re, b_im, c_re, c_im, d_ssm, log_dt_ssm,
                  w_glu_ssm, b_glu_ssm, w_out_ssm, norm_final):
    w_in = w_in_gdn[0]
    w_ba = jnp.pad(w_in[:, MAIN_DIM:], ((0, 0), (0, 128 - 2 * V_HEADS))).astype(BF16)
    zeros = jnp.zeros((V_HEADS,), F32)
    pad = jnp.zeros((128 - 2 * V_HEADS,), F32)
    gate_params = jnp.stack([jnp.concatenate([zeros, a_log_gdn[0], pad]),
                             jnp.concatenate([zeros, dt_bias_gdn[0], pad])])

    lr, li, ldt = lam_re[0], lam_im[0], log_dt_ssm[0]
    gp = SSM_GROUPS * SSM_STATE
    pw_r, pw_i = _ssm_pow(lr.reshape(1, gp), li.reshape(1, gp),
                          jnp.repeat(ldt, SSM_STATE).reshape(1, gp))
    bblk, cblk = _ssm_blocks(lr, li, ldt.reshape(SSM_GROUPS, 1),
                                 jnp.swapaxes(b_re[0], 1, 2), jnp.swapaxes(b_im[0], 1, 2), c_re[0], c_im[0])
    return dict(
        norm_gdn=norm_gdn[0], w_main=w_in, w_ba=w_ba, conv_w=conv_gdn[0], gate_params=gate_params,
        o_gain=onorm_gdn[0], w_out_gdn=w_out_gdn[0].astype(BF16),
        norm_ssm=norm_ssm[0], w_in_ssm=w_in_ssm[0], pw_r=pw_r, pw_i=pw_i, bblk=bblk, cblk=cblk,
        d_skip=d_ssm[0], w_glu=w_glu_ssm[0].astype(BF16), b_glu=b_glu_ssm[0], w_out_ssm=w_out_ssm[0].astype(BF16),
        norm_final=norm_final)


def _ssm_tail(p, x1, y, uz, *, tm, tm_glu):
    y3 = _glu(y, p["w_glu"], p["b_glu"], uz, tm=tm_glu, tn=512)
    return _outproj(y3, p["w_out_ssm"], x1, p["norm_final"], tm=tm, name="ssm_out")


def _prompt_path(p, x_prompt):
    batch, t_len, d = x_prompt.shape
    x = x_prompt.reshape(batch * t_len, d)
    proj, ba = _norm_matmul(x, p["norm_gdn"], p["w_main"], p["w_ba"], n=MAIN_DIM, tm=1024, tn=512, name="gdn_in")
    og, delta = _gdn_prompt(proj, ba, p["conv_w"], p["gate_params"], p["o_gain"], batch=batch, t_len=t_len)
    conv_state = proj.reshape(batch, t_len, MAIN_DIM)[:, t_len - (CONV_W - 1):, :CONV_DIM]
    x1 = _outproj(og, p["w_out_gdn"], x, tm=512, name="gdn_out")
    uz = _norm_matmul(x1, p["norm_ssm"], p["w_in_ssm"], n=2 * SSM_WIDTH, tm=1024, tn=512, name="ssm_in")
    y, h_re, h_im = _ssm_scan(uz, p["bblk"], p["cblk"], p["pw_r"], p["pw_i"], p["d_skip"],
                              batch=batch, t_len=t_len)
    out = _ssm_tail(p, x1, y, uz, tm=512, tm_glu=512)
    return (out.reshape(batch, t_len, d), conv_state[None], delta[None],
            h_re.reshape(1, batch, SSM_GROUPS, SSM_STATE), h_im.reshape(1, batch, SSM_GROUPS, SSM_STATE))


def _sample_path(p, x_sample, state_conv, state_delta, state_re, state_im):
    nb, _, d = x_sample.shape
    x = x_sample.reshape(nb, d)
    proj, ba = _norm_matmul(x, p["norm_gdn"], p["w_main"], p["w_ba"], n=MAIN_DIM, tm=nb, tn=512, name="gdn_in_s")
    beta, eg = _gdn_gates(ba, p["gate_params"])
    xs = jnp.concatenate([state_conv[0], proj[:, None, :CONV_DIM]], axis=1)
    slots = CONV_DIM // DK
    delta, og = _gdn_decode(beta[:, :V_HEADS], eg[:, V_HEADS:2 * V_HEADS],
                            xs.reshape(nb, CONV_W, slots, DK), p["conv_w"].reshape(CONV_W, slots, DK),
                            proj[:, CONV_DIM:].reshape(nb, V_HEADS, DV), p["o_gain"], state_delta[0])
    x1 = _outproj(og.reshape(nb, VAL_DIM).astype(BF16), p["w_out_gdn"], x, tm=nb, name="gdn_out_s")
    uz = _norm_matmul(x1, p["norm_ssm"], p["w_in_ssm"], n=2 * SSM_WIDTH, tm=nb, tn=512, name="ssm_in_s")
    gp = SSM_GROUPS * SSM_STATE
    y, h_re, h_im = _ssm_decode(uz, p["bblk"], p["cblk"], p["pw_r"], p["pw_i"], p["d_skip"],
                                state_re[0].reshape(nb, gp), state_im[0].reshape(nb, gp))
    out = _ssm_tail(p, x1, y, uz, tm=nb, tm_glu=nb)
    return (out.reshape(nb, 1, d), xs[:, 1:][None], delta[None],
            h_re.reshape(1, nb, SSM_GROUPS, SSM_STATE), h_im.reshape(1, nb, SSM_GROUPS, SSM_STATE))


def kernel(x_prompt, x_sample, state_gdn_conv, state_gdn_delta, state_ssm_re, state_ssm_im, norm_gdn, w_in_gdn, conv_gdn, a_log_gdn, dt_bias_gdn, onorm_gdn, w_out_gdn, norm_ssm, w_in_ssm, lam_re, lam_im, b_re, b_im, c_re, c_im, d_ssm, log_dt_ssm, w_glu_ssm, b_glu_ssm, w_out_ssm, norm_final):
    p = _prep_weights(norm_gdn, w_in_gdn, conv_gdn, a_log_gdn, dt_bias_gdn, onorm_gdn, w_out_gdn,
                      norm_ssm, w_in_ssm, lam_re, lam_im, b_re, b_im, c_re, c_im, d_ssm, log_dt_ssm,
                      w_glu_ssm, b_glu_ssm, w_out_ssm, norm_final)
    y_p, conv_p, delta_p, re_p, im_p = _prompt_path(p, x_prompt)
    y_s, conv_s, delta_s, re_s, im_s = _sample_path(p, x_sample, state_gdn_conv, state_gdn_delta,
                                                    state_ssm_re, state_ssm_im)
    return (y_p, y_s, conv_p, delta_p, re_p, im_p, conv_s, delta_s, re_s, im_s)
```

```python
import functools
import math

import jax
import jax.numpy as jnp
from jax import lax
from jax.experimental import pallas as pl
from jax.experimental.pallas import tpu as pltpu

F32 = jnp.float32
BF16 = jnp.bfloat16

RMS_EPS = 1e-6
L2_EPS = 1e-6

D_MODEL = 2048
QK_HEADS = 16
V_HEADS = 32
DK = 128
DV = 128
KEY_DIM = QK_HEADS * DK
VAL_DIM = V_HEADS * DV
CONV_DIM = 2 * KEY_DIM + VAL_DIM
CONV_W = 4
MAIN_DIM = CONV_DIM + VAL_DIM

SSM_WIDTH = 4096
SSM_GROUP = 16
SSM_GROUPS = SSM_WIDTH // SSM_GROUP
SSM_STATE = 64
GROUPS_PER_STEP = 16
CH_PER_STEP = GROUPS_PER_STEP * SSM_GROUP
ST_PER_STEP = GROUPS_PER_STEP * SSM_STATE
N_GSTEPS = SSM_GROUPS // GROUPS_PER_STEP

CHUNK = 128
SCAN_BLOCK = 1024
SCAN_SEG = SCAN_BLOCK // 8
POW_ROWS = 8

VMEM_LIMIT = 56 * 1024 * 1024


def _params(sem):
    return pltpu.CompilerParams(dimension_semantics=sem, vmem_limit_bytes=VMEM_LIMIT)


def _mm(a, b):
    return jnp.dot(a.astype(BF16), b.astype(BF16), preferred_element_type=F32)


def _mm_hi(a, b):
    return jnp.dot(a, b, precision=lax.Precision.HIGHEST, preferred_element_type=F32)


def _silu(x):
    return x * jax.nn.sigmoid(x)


def _rms(x, g):
    ms = jnp.mean(x * x, axis=-1, keepdims=True)
    return x * lax.rsqrt(ms + RMS_EPS) * g


def _norm_matmul_body(x_ref, g_ref, w_ref, *rest, has_aux, w_transposed):
    if has_aux:
        waux_ref, o_ref, aux_ref, h_scr = rest
    else:
        o_ref, h_scr = rest
    mm = _dot_nt if w_transposed else _dotb

    @pl.when(pl.program_id(1) == 0)
    def _():
        h = _rms(x_ref[...], g_ref[...]).astype(BF16)
        h_scr[...] = h
        if has_aux:
            aux_ref[...] = mm(h, waux_ref[...])

    o_ref[...] = mm(h_scr[...], w_ref[...].astype(BF16))


def _norm_matmul(x, g, w, w_aux=None, *, n, tm, tn, name, w_transposed=False):
    m, d = x.shape
    has_aux = w_aux is not None
    in_specs = [
        pl.BlockSpec((tm, d), lambda i, j: (i, 0)),
        pl.BlockSpec((1, d), lambda i, j: (0, 0)),
        pl.BlockSpec((tn, d), lambda i, j: (j, 0)) if w_transposed else pl.BlockSpec((d, tn), lambda i, j: (0, j)),
    ]
    out_shape = [jax.ShapeDtypeStruct((m, n), F32)]
    out_specs = [pl.BlockSpec((tm, tn), lambda i, j: (i, j))]
    args = [x, g.reshape(1, d), w]
    if has_aux:
        na = w_aux.shape[0] if w_transposed else w_aux.shape[1]
        in_specs.append(pl.BlockSpec(w_aux.shape, lambda i, j: (0, 0)))
        out_shape.append(jax.ShapeDtypeStruct((m, na), F32))
        out_specs.append(pl.BlockSpec((tm, na), lambda i, j: (i, 0)))
        args.append(w_aux)
    res = pl.pallas_call(
        functools.partial(_norm_matmul_body, has_aux=has_aux, w_transposed=w_transposed),
        out_shape=out_shape,
        grid=(m // tm, n // tn),
        in_specs=in_specs,
        out_specs=out_specs,
        scratch_shapes=[pltpu.VMEM((tm, d), BF16)],
        compiler_params=_params(("parallel", "arbitrary")),
        name=name,
    )(*args)
    return res if has_aux else res[0]


def _outproj_body(a_ref, w_ref, x_ref, *rest, final_norm):
    if final_norm:
        g_ref, o_ref = rest
    else:
        (o_ref,) = rest
    y = x_ref[...] + jnp.dot(a_ref[...], w_ref[...], preferred_element_type=F32)
    o_ref[...] = _rms(y, g_ref[...]) if final_norm else y


def _outproj(a, w, x, g=None, *, tm, name):
    m, kd = a.shape
    n = w.shape[1]
    final_norm = g is not None
    in_specs = [
        pl.BlockSpec((tm, kd), lambda i: (i, 0)),
        pl.BlockSpec((kd, n), lambda i: (0, 0), pipeline_mode=pl.Buffered(1)),
        pl.BlockSpec((tm, n), lambda i: (i, 0)),
    ]
    args = [a, w, x]
    if final_norm:
        in_specs.append(pl.BlockSpec((1, n), lambda i: (0, 0)))
        args.append(g.reshape(1, n))
    return pl.pallas_call(
        functools.partial(_outproj_body, final_norm=final_norm),
        out_shape=jax.ShapeDtypeStruct((m, n), F32),
        grid=(m // tm,),
        in_specs=in_specs,
        out_specs=pl.BlockSpec((tm, n), lambda i: (i, 0)),
        compiler_params=_params(("parallel",)),
        name=name,
    )(*args)


def _conv_silu_chunk(x_ref, cw, n):
    cc = x_ref.shape[1]
    r0 = pl.multiple_of(n * CHUNK, CHUNK)
    cur = x_ref[pl.ds(r0, CHUNK), :]
    p0 = pl.multiple_of(jnp.maximum(r0 - 8, 0), 8)
    prev = jnp.where(n > 0, x_ref[pl.ds(p0, 8), :], 0.0)
    row8 = lax.broadcasted_iota(jnp.int32, (8, cc), 0)
    y = cur * cw[CONV_W - 1:CONV_W, :]
    for s in range(1, CONV_W):
        sh = pltpu.roll(cur, s, 0)
        top = jnp.where(row8 < s, pltpu.roll(prev, s, 0), sh[0:8, :])
        sh = jnp.concatenate([top, sh[8:, :]], axis=0)
        y = y + sh * cw[CONV_W - 1 - s:CONV_W - s, :]
    return _silu(y)


INV_BASE_SHIFT = 3
V_PER_QK = V_HEADS // QK_HEADS
HQ_PER_STEP = 2
DEC_GROUP = 8
DEC_SAMPLES = 2
PH1_CHUNKS = 2


def _split2(x):
    hi = x.astype(BF16)
    return hi, (x - hi.astype(F32)).astype(BF16)


def _split3(x):
    hi = x.astype(BF16)
    r = x - hi.astype(F32)
    mid = r.astype(BF16)
    return hi, mid, (r - mid.astype(F32)).astype(BF16)


def _dotb(a, b):
    return jnp.dot(a, b, preferred_element_type=F32)


N_LEVELS = CHUNK.bit_length() - 1 - INV_BASE_SHIFT


def _chunk_tables():
    row = lax.broadcasted_iota(jnp.int32, (CHUNK, CHUNK), 0)
    col = lax.broadcasted_iota(jnp.int32, (CHUNK, CHUNK), 1)
    same = [(row >> s) == (col >> s) for s in range(INV_BASE_SHIFT, CHUNK.bit_length())]
    tab32 = jnp.stack([row == col, same[0]]).astype(F32)
    levels = [jnp.logical_and(hi, jnp.logical_not(lo)) for lo, hi in zip(same[:-1], same[1:])]
    tab16 = jnp.stack([row >= col] + levels).astype(BF16)
    return tab32, tab16


def _inv_unit_lower(mats, tab32_ref, tab16_ref, out):
    n = range(len(mats))
    eye = tab32_ref[0]
    d = [mats[i] * tab32_ref[1] for i in n]
    db = [d[i].astype(BF16) for i in n]
    t = [eye - d[i] for i in n]
    a_sp = [_split2(mats[i]) for i in n]
    p = [_dotb(db[i], db[i]) for i in n]
    yield
    t = [t[i] + _dotb(t[i].astype(BF16), p[i].astype(BF16)) for i in n]
    p = [_dotb(p[i].astype(BF16), p[i].astype(BF16)) for i in n]
    yield
    t = [t[i] + _dotb(t[i].astype(BF16), p[i].astype(BF16)) for i in n]
    yield
    for lvl in range(N_LEVELS):
        tb = [t[i].astype(BF16) for i in n]
        x = [_dotb(tb[i], a_sp[i][0] * tab16_ref[1 + lvl]) for i in n]
        yield
        t = [t[i] - _dotb(x[i].astype(BF16), tb[i]) for i in n]
        yield
    t_sp = [_split2(t[i]) for i in n]
    at = [_dotb(a_sp[i][0], t_sp[i][0]) for i in n]
    at = [at[i] + _dotb(a_sp[i][0], t_sp[i][1]) for i in n]
    at = [at[i] + _dotb(a_sp[i][1], t_sp[i][0]) for i in n]
    yield
    resid = [(eye - t[i]) - at[i] for i in n]
    out.extend(t[i] + _dotb(t_sp[i][0], resid[i].astype(BF16)) for i in n)
    yield


def _trace_interleaved(*gens):
    live = list(gens)
    while live:
        for g in list(live):
            try:
                next(g)
            except StopIteration:
                live.remove(g)


def _gdn_prompt_body(q_ref, k_ref, v_ref, z_ref, ba_ref, cwq_ref, cwk_ref, cwv_ref, gp_ref, on_ref,
                     tab32_ref, tab16_ref, og_ref, s_ref, u_s, w_s, qd_s, at_s, kdt_s, gl_s, a_s, rhs_s):
    hq = pl.program_id(1)
    t_len = q_ref.shape[0]
    n_chunks = t_len // CHUNK
    row = lax.broadcasted_iota(jnp.int32, (CHUNK, CHUNK), 0)
    col = lax.broadcasted_iota(jnp.int32, (CHUNK, CHUNK), 1)
    causal = row >= col
    strict = row > col
    cwq = cwq_ref[...]
    cwk = cwk_ref[...]
    cwv = cwv_ref[...]
    neg_a = -jnp.exp(gp_ref[0:1, :])
    dt_bias = gp_ref[1:2, :]

    nv = HQ_PER_STEP * V_PER_QK
    n_iters = n_chunks // PH1_CHUNKS

    def chunk_prep(n, slot, base):
        r0 = pl.multiple_of(n * CHUNK, CHUNK)
        rows = pl.ds(r0, CHUNK)
        qc = _conv_silu_chunk(q_ref, cwq, n)
        yield
        kc = _conv_silu_chunk(k_ref, cwk, n)
        yield
        vc = _conv_silu_chunk(v_ref, cwv, n)
        yield
        ba = ba_ref[rows, :]
        beta_all = jax.nn.sigmoid(ba)
        xa = ba + dt_bias
        g_all = neg_a * (jnp.maximum(xa, 0.0) + jnp.log1p(jnp.exp(-jnp.abs(xa))))
        g_hi, g_mid, g_lo = _split3(g_all)
        tril = tab16_ref[0]
        gcum = _dotb(tril, g_hi) + (_dotb(tril, g_mid) + _dotb(tril, g_lo))
        gcum_t = gcum.T
        yield
        for q in range(HQ_PER_STEP):
            qh = qc[:, q * DK:(q + 1) * DK]
            kh = kc[:, q * DK:(q + 1) * DK]
            qn = qh * lax.rsqrt(jnp.sum(qh * qh, axis=-1, keepdims=True) + L2_EPS) * (DK ** -0.5)
            kn = kh * lax.rsqrt(jnp.sum(kh * kh, axis=-1, keepdims=True) + L2_EPS)
            kn_t = kn.T
            kn_tb = kn_t.astype(BF16)
            kk = _dotb(kn.astype(BF16), kn_tb)
            qk = _dotb(qn.astype(BF16), kn_tb)
            yield
            for j in range(V_PER_QK):
                h = q * V_PER_QK + j
                lane_b = (hq * HQ_PER_STEP + q) * V_PER_QK + j
                lane_a = V_HEADS + lane_b
                gcol = jnp.sum(jnp.where(col == lane_a, gcum, 0.0), axis=1, keepdims=True)
                bcol = jnp.sum(jnp.where(col == lane_b, beta_all, 0.0), axis=1, keepdims=True)
                grow = jnp.sum(jnp.where(row == lane_a, gcum_t, 0.0), axis=0, keepdims=True)
                glast = gcol[CHUNK - 1:CHUNK, :]
                decay = jnp.exp(jnp.where(causal, gcol - grow, -jnp.inf))
                a_mat = jnp.where(strict, kk * decay, 0.0) * bcol
                eg = jnp.exp(gcol)
                rhs = jnp.concatenate([vc[:, h * DV:(h + 1) * DV] * bcol, kn * (bcol * eg)], axis=1).astype(BF16)
                qd_s[h, rows, :] = (qn * eg).astype(BF16)
                at_s[h, rows, :] = (qk * decay).astype(BF16)
                kdt_s[h, rows, :] = (kn_t * jnp.exp(glast - grow)).astype(BF16)
                gl_s[h, pl.ds(n, 1), :] = jnp.broadcast_to(jnp.exp(glast), (1, DV))
                a_s[slot, base + h] = a_mat
                rhs_s[slot, base + h] = rhs
                yield

    def prep(it):
        it = jnp.asarray(it, jnp.int32)
        slot = it % 2
        for c in range(PH1_CHUNKS):
            yield from chunk_prep(PH1_CHUNKS * it + c, slot, c * nv)

    def solve(it):
        it = jnp.asarray(it, jnp.int32)
        slot = it % 2
        idx = range(PH1_CHUNKS * nv)
        t_mats = []
        yield from _inv_unit_lower([a_s[slot, i] for i in idx], tab32_ref, tab16_ref, t_mats)
        sols = [_dotb(t_mats[i].astype(BF16), rhs_s[slot, i]) for i in idx]
        yield
        for i in idx:
            rows = pl.ds(pl.multiple_of((PH1_CHUNKS * it + i // nv) * CHUNK, CHUNK), CHUNK)
            u_s[i % nv, rows, :] = sols[i][:, :DV]
            w_s[i % nv, rows, :] = sols[i][:, DV:].astype(BF16)
        yield

    _trace_interleaved(prep(0))

    def phase1(it, carry):
        _trace_interleaved(solve(it - 1), prep(it))
        return carry

    lax.fori_loop(1, n_iters, phase1, 0)
    _trace_interleaved(solve(n_iters - 1))

    gain = on_ref[...]
    heads = range(HQ_PER_STEP * V_PER_QK)

    def phase2(n, states):
        rows = pl.ds(pl.multiple_of(n * CHUNK, CHUNK), CHUNK)
        wq = [jnp.concatenate([w_s[h, rows, :], qd_s[h, rows, :]], axis=0) for h in heads]
        ws_qs = [_dotb(wq[h], states[h].astype(BF16)) for h in heads]
        v_new = [(u_s[h, rows, :] - ws_qs[h][:CHUNK]).astype(BF16) for h in heads]
        o = [ws_qs[h][CHUNK:] + _dotb(at_s[h, rows, :], v_new[h]) for h in heads]
        s_new = [states[h] * gl_s[h, pl.ds(n, 1), :] + _dotb(kdt_s[h, rows, :], v_new[h]) for h in heads]
        for h in heads:
            zg = z_ref[rows, h * DV:(h + 1) * DV]
            og = o[h] * lax.rsqrt(jnp.mean(o[h] * o[h], axis=-1, keepdims=True) + RMS_EPS) * gain * _silu(zg)
            og_ref[rows, h * DV:(h + 1) * DV] = og.astype(BF16)
        return tuple(s_new)

    zero = jnp.zeros((DK, DV), F32)
    final = lax.fori_loop(0, n_chunks, phase2, tuple(zero for _ in heads))
    for h in heads:
        s_ref[0, h] = final[h]


def _gdn_prompt(proj, ba, conv_w, gate_params, o_gain, *, batch, t_len):
    wq = HQ_PER_STEP * DK
    wv = HQ_PER_STEP * V_PER_QK * DV
    nv = HQ_PER_STEP * V_PER_QK
    in_specs = [
        pl.BlockSpec((t_len, wq), lambda b, h: (b, h)),
        pl.BlockSpec((t_len, wq), lambda b, h: (b, KEY_DIM // wq + h)),
        pl.BlockSpec((t_len, wv), lambda b, h: (b, (2 * KEY_DIM) // wv + h)),
        pl.BlockSpec((t_len, wv), lambda b, h: (b, CONV_DIM // wv + h)),
        pl.BlockSpec((t_len, 128), lambda b, h: (b, 0)),
        pl.BlockSpec((CONV_W, wq), lambda b, h: (0, h)),
        pl.BlockSpec((CONV_W, wq), lambda b, h: (0, KEY_DIM // wq + h)),
        pl.BlockSpec((CONV_W, wv), lambda b, h: (0, (2 * KEY_DIM) // wv + h)),
        pl.BlockSpec((2, 128), lambda b, h: (0, 0)),
        pl.BlockSpec((1, DV), lambda b, h: (0, 0)),
        pl.BlockSpec((2, CHUNK, CHUNK), lambda b, h: (0, 0, 0)),
        pl.BlockSpec((1 + N_LEVELS, CHUNK, CHUNK), lambda b, h: (0, 0, 0)),
    ]
    tab32, tab16 = _chunk_tables()
    out_shape = [
        jax.ShapeDtypeStruct((batch * t_len, VAL_DIM), BF16),
        jax.ShapeDtypeStruct((batch, V_HEADS, DK, DV), F32),
    ]
    out_specs = [
        pl.BlockSpec((t_len, wv), lambda b, h: (b, h)),
        pl.BlockSpec((1, nv, DK, DV), lambda b, h: (b, h, 0, 0)),
    ]
    scratch = [
        pltpu.VMEM((nv, t_len, DV), F32),
        pltpu.VMEM((nv, t_len, DK), BF16),
        pltpu.VMEM((nv, t_len, DK), BF16),
        pltpu.VMEM((nv, t_len, CHUNK), BF16),
        pltpu.VMEM((nv, t_len, CHUNK), BF16),
        pltpu.VMEM((nv, t_len // CHUNK, DV), F32),
        pltpu.VMEM((2, PH1_CHUNKS * nv, CHUNK, CHUNK), F32),
        pltpu.VMEM((2, PH1_CHUNKS * nv, CHUNK, DV + DK), BF16),
    ]
    return pl.pallas_call(
        _gdn_prompt_body,
        out_shape=out_shape,
        grid=(batch, QK_HEADS // HQ_PER_STEP),
        in_specs=in_specs,
        out_specs=out_specs,
        scratch_shapes=scratch,
        compiler_params=_params(("parallel", "arbitrary")),
        name="gdn_prompt",
    )(proj, proj, proj, proj, ba, conv_w, conv_w, conv_w, gate_params, o_gain.reshape(1, DV), tab32, tab16)


def _gdn_gates_body(ba_ref, gp_ref, beta_ref, eg_ref):
    ba = ba_ref[...]
    beta_ref[...] = jax.nn.sigmoid(ba)
    xa = ba + gp_ref[1:2, :]
    g = -jnp.exp(gp_ref[0:1, :]) * (jnp.maximum(xa, 0.0) + jnp.log1p(jnp.exp(-jnp.abs(xa))))
    eg_ref[...] = jnp.exp(g)


def _gdn_gates(ba, gate_params):
    m = ba.shape[0]
    return pl.pallas_call(
        _gdn_gates_body,
        out_shape=[jax.ShapeDtypeStruct((m, 128), F32)] * 2,
        name="gdn_gates",
    )(ba, gate_params)


def _gdn_decode_body(beta_ref, eg_ref, xs_ref, cw_ref, z_ref, on_ref, s_ref, so_ref, og_ref):
    for si in range(DEC_SAMPLES):
        b = pl.program_id(0) * DEC_SAMPLES + si
        y = xs_ref[si, 0] * cw_ref[0]
        for j in range(1, CONV_W):
            y = y + xs_ref[si, j] * cw_ref[j]
        y = _silu(y)
        q = y[0:QK_HEADS]
        k = y[QK_HEADS:2 * QK_HEADS]
        v = y[2 * QK_HEADS:]
        qn = q * lax.rsqrt(jnp.sum(q * q, axis=-1, keepdims=True) + L2_EPS) * (DK ** -0.5)
        kn = k * lax.rsqrt(jnp.sum(k * k, axis=-1, keepdims=True) + L2_EPS)
        qk_t = jnp.concatenate([qn, kn, jnp.zeros((DK - 2 * QK_HEADS, DK), F32)], axis=0).T
        for g0 in range(0, V_HEADS, DEC_GROUP):
            hs = range(g0, g0 + DEC_GROUP)
            qcol = {h: qk_t[:, h // V_PER_QK:h // V_PER_QK + 1] for h in hs}
            kcol = {h: qk_t[:, QK_HEADS + h // V_PER_QK:QK_HEADS + h // V_PER_QK + 1] for h in hs}
            sd = {h: s_ref[si, h] * eg_ref[b, h] for h in hs}
            ks = {h: jnp.sum(sd[h] * kcol[h], axis=0, keepdims=True) for h in hs}
            v_new = {h: beta_ref[b, h] * (v[h:h + 1, :] - ks[h]) for h in hs}
            sn = {h: sd[h] + kcol[h] * v_new[h] for h in hs}
            for h in hs:
                so_ref[si, h] = sn[h]
            o = {h: jnp.sum(sn[h] * qcol[h], axis=0, keepdims=True) for h in hs}
            for h in hs:
                og_ref[si, h:h + 1, :] = o[h]
        o_all = og_ref[si]
        og_ref[si] = (o_all * lax.rsqrt(jnp.mean(o_all * o_all, axis=-1, keepdims=True) + RMS_EPS)
                      * on_ref[...] * _silu(z_ref[si]))


def _gdn_decode(beta, eg, xs, conv_w3, z3, o_gain, state):
    nb = xs.shape[0]
    slots = CONV_DIM // DK
    smem = pl.BlockSpec(memory_space=pltpu.SMEM)
    return pl.pallas_call(
        _gdn_decode_body,
        out_shape=[
            jax.ShapeDtypeStruct((nb, V_HEADS, DK, DV), F32),
            jax.ShapeDtypeStruct((nb, V_HEADS, DV), F32),
        ],
        grid=(nb // DEC_SAMPLES,),
        in_specs=[
            smem,
            smem,
            pl.BlockSpec((DEC_SAMPLES, CONV_W, slots, DK), lambda b: (b, 0, 0, 0)),
            pl.BlockSpec((CONV_W, slots, DK), lambda b: (0, 0, 0)),
            pl.BlockSpec((DEC_SAMPLES, V_HEADS, DV), lambda b: (b, 0, 0)),
            pl.BlockSpec((1, DV), lambda b: (0, 0)),
            pl.BlockSpec((DEC_SAMPLES, V_HEADS, DK, DV), lambda b: (b, 0, 0, 0)),
        ],
        out_specs=[
            pl.BlockSpec((DEC_SAMPLES, V_HEADS, DK, DV), lambda b: (b, 0, 0, 0)),
            pl.BlockSpec((DEC_SAMPLES, V_HEADS, DV), lambda b: (b, 0, 0)),
        ],
        compiler_params=_params(("parallel",)),
        name="gdn_decode",
    )(beta, eg, xs, conv_w3, z3, o_gain.reshape(1, DV), state)


def _ssm_pow_body(lr_ref, li_ref, ldt_ref, pr_ref, pi_ref):
    n = jnp.where(lax.broadcasted_iota(jnp.int32, pr_ref.shape, 0) == 0, 1.0, float(SCAN_SEG))
    dt = jnp.exp(ldt_ref[...])
    lr = jnp.minimum(lr_ref[...], -1e-4) * dt
    li = li_ref[...] * dt
    mag = jnp.exp(n * lr)
    pr_ref[...] = mag * jnp.cos(n * li)
    pi_ref[...] = mag * jnp.sin(n * li)


def _ssm_pow(lr, li, ldt):
    gp = lr.shape[1]
    tl = 2048
    spec = pl.BlockSpec((1, tl), lambda i: (0, i))
    ospec = pl.BlockSpec((POW_ROWS, tl), lambda i: (0, i))
    return pl.pallas_call(
        _ssm_pow_body,
        out_shape=[jax.ShapeDtypeStruct((POW_ROWS, gp), F32)] * 2,
        grid=(gp // tl,),
        in_specs=[spec, spec, spec],
        out_specs=[ospec, ospec],
        name="ssm_pow",
    )(lr, li, ldt)


def _ssm_blocks_body(lr_ref, li_ref, ldt_ref, br_ref, bi_ref, cr_ref, ci_ref, bblk_ref, cblk_ref):
    dt = jnp.exp(ldt_ref[...])
    lr = jnp.minimum(lr_ref[...], -1e-4)
    li = li_ref[...]
    mag = jnp.exp(lr * dt)
    xr = mag * jnp.cos(li * dt) - 1.0
    xi = mag * jnp.sin(li * dt)
    den = 1.0 / (lr * lr + li * li)
    fr = (xr * lr + xi * li) * den
    fi = (xi * lr - xr * li) * den
    bblk_ref[...] = jnp.zeros_like(bblk_ref)
    cblk_ref[...] = jnp.zeros_like(cblk_ref)
    per_tile = 128 // SSM_STATE
    for g in range(GROUPS_PER_STEP):
        rows = slice(g * SSM_GROUP, (g + 1) * SSM_GROUP)
        off = (g // per_tile) * 256 + (g % per_tile) * SSM_STATE
        cols = slice(off, off + SSM_STATE)
        cols_im = slice(off + 128, off + 128 + SSM_STATE)
        br = br_ref[g]
        bi = bi_ref[g]
        frg = fr[g:g + 1, :]
        fig = fi[g:g + 1, :]
        bblk_ref[rows, cols] = (frg * br - fig * bi).astype(BF16)
        bblk_ref[rows, cols_im] = (frg * bi + fig * br).astype(BF16)
        cblk_ref[rows, cols] = cr_ref[g].astype(BF16)
        cblk_ref[rows, cols_im] = (-ci_ref[g]).astype(BF16)


def _ssm_blocks(lr, li, ldt, bt_re, bt_im, c_re, c_im):
    g, p = lr.shape
    k = GROUPS_PER_STEP
    s2 = pl.BlockSpec((k, p), lambda i: (i, 0))
    s1 = pl.BlockSpec((k, 1), lambda i: (i, 0))
    s3 = pl.BlockSpec((k, SSM_GROUP, p), lambda i: (i, 0, 0))
    blk = pl.BlockSpec((None, CH_PER_STEP, 2 * ST_PER_STEP), lambda i: (i, 0, 0))
    return pl.pallas_call(
        _ssm_blocks_body,
        out_shape=[jax.ShapeDtypeStruct((g // k, CH_PER_STEP, 2 * ST_PER_STEP), BF16)] * 2,
        grid=(g // k,),
        in_specs=[s2, s2, s1, s3, s3, s3, s3],
        out_specs=[blk, blk],
        name="ssm_blocks",
    )(lr, li, ldt, bt_re, bt_im, c_re, c_im)


def _dot_nt(a, b):
    return lax.dot_general(a, b, (((1,), (1,)), ((), ())), preferred_element_type=F32)


def _gelu_tanh(x):
    return 0.5 * x * (1.0 + jnp.tanh(math.sqrt(2.0 / math.pi) * (x + 0.044715 * (x * x * x))))


def _ssm_scan_body(ua_ref, ub_ref, b_ref, c_ref, pr_ref, pi_ref, d_ref, y_ref, hr_ref, hi_ref,
                   h_s, y_s, cr_s, ci_s):
    tb = pl.program_id(2)
    nt = ST_PER_STEP // 128
    u_perm = jnp.concatenate(
        [jnp.concatenate([ua_ref[pl.ds(i, 8, stride=SCAN_SEG), :], ub_ref[pl.ds(i, 8, stride=SCAN_SEG), :]], axis=1)
         for i in range(SCAN_SEG)], axis=0).astype(BF16)

    @pl.when(tb == 0)
    def _():
        cr_s[...] = jnp.zeros_like(cr_s)
        ci_s[...] = jnp.zeros_like(ci_s)

    def b_proj(c):
        bu = jnp.dot(u_perm, b_ref[:, c * 256:(c + 1) * 256], preferred_element_type=F32)
        h_s[c] = bu[:, :128]
        h_s[nt + c] = bu[:, 128:]

    row8 = lax.broadcasted_iota(jnp.int32, (8, 128), 0)
    b_proj(0)
    y = None
    for c in range(nt):
        if c + 1 < nt:
            b_proj(c + 1)
        lanes = slice(c * 128, (c + 1) * 128)
        ar = jnp.broadcast_to(pr_ref[0:1, lanes], (8, 128))
        ai = jnp.broadcast_to(pi_ref[0:1, lanes], (8, 128))
        hr = jnp.zeros((8, 128), F32)
        hi = jnp.zeros((8, 128), F32)
        for i in range(SCAN_SEG):
            xr = h_s[c, i * 8:(i + 1) * 8, :]
            xi = h_s[nt + c, i * 8:(i + 1) * 8, :]
            hr, hi = ar * hr - ai * hi + xr, ar * hi + ai * hr + xi

        asr = pr_ref[1:2, lanes]
        asi = pi_ref[1:2, lanes]
        c_r = cr_s[:, lanes]
        c_i = ci_s[:, lanes]
        cin_r = jnp.zeros((8, 128), F32)
        cin_i = jnp.zeros((8, 128), F32)
        for s in range(8):
            cin_r = jnp.where(row8 == s, c_r, cin_r)
            cin_i = jnp.where(row8 == s, c_i, cin_i)
            e_r = hr[s:s + 1, :]
            e_i = hi[s:s + 1, :]
            c_r, c_i = e_r + asr * c_r - asi * c_i, e_i + asr * c_i + asi * c_r
        cr_s[:, lanes] = c_r
        ci_s[:, lanes] = c_i

        hr, hi = cin_r, cin_i
        for i in range(SCAN_SEG):
            xr = h_s[c, i * 8:(i + 1) * 8, :]
            xi = h_s[nt + c, i * 8:(i + 1) * 8, :]
            hr, hi = ar * hr - ai * hi + xr, ar * hi + ai * hr + xi
            h_s[c, i * 8:(i + 1) * 8, :] = hr
            h_s[nt + c, i * 8:(i + 1) * 8, :] = hi

        h_cat = jnp.concatenate([h_s[c], h_s[nt + c]], axis=1).astype(BF16)
        y_c = _dot_nt(h_cat, c_ref[:, c * 256:(c + 1) * 256])
        y = y_c if y is None else y + y_c

    y_s[0] = y[:, :128]
    y_s[1] = y[:, 128:]
    per_seg = SCAN_SEG // 8
    y_nat = jnp.concatenate(
        [jnp.concatenate([y_s[h, pl.ds((j % per_seg) * 64 + j // per_seg, 8, stride=8), :] for h in range(2)], axis=1)
         for j in range(SCAN_BLOCK // 8)], axis=0)
    u = jnp.concatenate([ua_ref[...], ub_ref[...]], axis=1)
    y_ref[...] = _gelu_tanh(y_nat + d_ref[...] * u)

    @pl.when(tb == pl.num_programs(2) - 1)
    def _():
        hr_ref[...] = cr_s[...]
        hi_ref[...] = ci_s[...]


def _ssm_scan(uz, bblk, cblk, pw_r, pw_i, d_skip, *, batch, t_len):
    nt = t_len // SCAN_BLOCK
    return pl.pallas_call(
        _ssm_scan_body,
        out_shape=[
            jax.ShapeDtypeStruct((batch * t_len, SSM_WIDTH), F32),
            jax.ShapeDtypeStruct((batch, 1, SSM_GROUPS * SSM_STATE), F32),
            jax.ShapeDtypeStruct((batch, 1, SSM_GROUPS * SSM_STATE), F32),
        ],
        grid=(batch, N_GSTEPS, nt),
        in_specs=[
            pl.BlockSpec((SCAN_BLOCK, 128), lambda b, g, t: (b * nt + t, 2 * g)),
            pl.BlockSpec((SCAN_BLOCK, 128), lambda b, g, t: (b * nt + t, 2 * g + 1)),
            pl.BlockSpec((None, CH_PER_STEP, 2 * ST_PER_STEP), lambda b, g, t: (g, 0, 0)),
            pl.BlockSpec((None, CH_PER_STEP, 2 * ST_PER_STEP), lambda b, g, t: (g, 0, 0)),
            pl.BlockSpec((POW_ROWS, ST_PER_STEP), lambda b, g, t: (0, g)),
            pl.BlockSpec((POW_ROWS, ST_PER_STEP), lambda b, g, t: (0, g)),
            pl.BlockSpec((1, CH_PER_STEP), lambda b, g, t: (0, g)),
        ],
        out_specs=[
            pl.BlockSpec((SCAN_BLOCK, CH_PER_STEP), lambda b, g, t: (b * nt + t, g)),
            pl.BlockSpec((None, 1, ST_PER_STEP), lambda b, g, t: (b, 0, g)),
            pl.BlockSpec((None, 1, ST_PER_STEP), lambda b, g, t: (b, 0, g)),
        ],
        scratch_shapes=[
            pltpu.VMEM((2 * ST_PER_STEP // 128, SCAN_BLOCK, 128), F32),
            pltpu.VMEM((CH_PER_STEP // 128, SCAN_BLOCK, 128), F32),
            pltpu.VMEM((1, ST_PER_STEP), F32),
            pltpu.VMEM((1, ST_PER_STEP), F32),
        ],
        compiler_params=_params(("parallel", "parallel", "arbitrary")),
        name="ssm_scan",
    )(uz, uz, bblk, cblk, pw_r, pw_i, d_skip.reshape(1, SSM_WIDTH))


def _ssm_decode_body(u_ref, b_ref, c_ref, pr_ref, pi_ref, d_ref, h0r_ref, h0i_ref, y_ref, hr_ref, hi_ref):
    nt = ST_PER_STEP // 128
    u = u_ref[...]
    bu = jnp.dot(u.astype(BF16), b_ref[...], preferred_element_type=F32)
    h_cat = []
    for c in range(nt):
        lanes = slice(c * 128, (c + 1) * 128)
        ar = pr_ref[0:1, lanes]
        ai = pi_ref[0:1, lanes]
        h0r = h0r_ref[:, lanes]
        h0i = h0i_ref[:, lanes]
        hr = bu[:, c * 256:c * 256 + 128] + ar * h0r - ai * h0i
        hi = bu[:, c * 256 + 128:(c + 1) * 256] + ar * h0i + ai * h0r
        hr_ref[:, lanes] = hr
        hi_ref[:, lanes] = hi
        h_cat += [hr, hi]
    y = _dot_nt(jnp.concatenate(h_cat, axis=1).astype(BF16), c_ref[...])
    y_ref[...] = _gelu_tanh(y + d_ref[...] * u)


def _ssm_decode(uz, bblk, cblk, pw_r, pw_i, d_skip, h0r, h0i):
    nb = uz.shape[0]
    return pl.pallas_call(
        _ssm_decode_body,
        out_shape=[
            jax.ShapeDtypeStruct((nb, SSM_WIDTH), F32),
            jax.ShapeDtypeStruct((nb, SSM_GROUPS * SSM_STATE), F32),
            jax.ShapeDtypeStruct((nb, SSM_GROUPS * SSM_STATE), F32),
        ],
        grid=(N_GSTEPS,),
        in_specs=[
            pl.BlockSpec((nb, CH_PER_STEP), lambda g: (0, g)),
            pl.BlockSpec((None, CH_PER_STEP, 2 * ST_PER_STEP), lambda g: (g, 0, 0)),
            pl.BlockSpec((None, CH_PER_STEP, 2 * ST_PER_STEP), lambda g: (g, 0, 0)),
            pl.BlockSpec((POW_ROWS, ST_PER_STEP), lambda g: (0, g)),
            pl.BlockSpec((POW_ROWS, ST_PER_STEP), lambda g: (0, g)),
            pl.BlockSpec((1, CH_PER_STEP), lambda g: (0, g)),
            pl.BlockSpec((nb, ST_PER_STEP), lambda g: (0, g)),
            pl.BlockSpec((nb, ST_PER_STEP), lambda g: (0, g)),
        ],
        out_specs=[
            pl.BlockSpec((nb, CH_PER_STEP), lambda g: (0, g)),
            pl.BlockSpec((nb, ST_PER_STEP), lambda g: (0, g)),
            pl.BlockSpec((nb, ST_PER_STEP), lambda g: (0, g)),
        ],
        compiler_params=_params(("parallel",)),
        name="ssm_decode",
    )(uz, bblk, cblk, pw_r, pw_i, d_skip.reshape(1, SSM_WIDTH), h0r, h0i)


def _glu_body(y_ref, w_ref, b_ref, z_ref, o_ref, yb_scr, *, tn):
    j = pl.program_id(1)

    @pl.when(j == 0)
    def _():
        yb_scr[...] = y_ref[...].astype(BF16)

    t = jnp.dot(yb_scr[...], w_ref[...], preferred_element_type=F32) + b_ref[...]
    y = y_ref[:, pl.ds(pl.multiple_of(j * tn, tn), tn)]
    o_ref[...] = (y * jax.nn.sigmoid(t) * _silu(z_ref[...])).astype(BF16)


def _glu(y, w, b, uz, *, tm, tn):
    m, d = y.shape
    zoff = SSM_WIDTH // tn
    return pl.pallas_call(
        functools.partial(_glu_body, tn=tn),
        out_shape=jax.ShapeDtypeStruct((m, d), BF16),
        grid=(m // tm, d // tn),
        in_specs=[
            pl.BlockSpec((tm, d), lambda i, j: (i, 0)),
            pl.BlockSpec((d, tn), lambda i, j: (0, j)),
            pl.BlockSpec((1, tn), lambda i, j: (0, j)),
            pl.BlockSpec((tm, tn), lambda i, j: (i, zoff + j)),
        ],
        out_specs=pl.BlockSpec((tm, tn), lambda i, j: (i, j)),
        scratch_shapes=[pltpu.VMEM((tm, d), BF16)],
        compiler_params=_params(("parallel", "arbitrary")),
        name="ssm_glu",
    )(y, w, b.reshape(1, d), uz)


def _prep_weights(norm_gdn, w_in_gdn, conv_gdn, a_log_gdn, dt_bias_gdn, onorm_gdn, w_out_gdn,
                  norm_ssm, w_in_ssm, lam_re, lam_im, b_re, b_im, c_re, c_im, d_ssm, log_dt_ssm,
                  w_glu_ssm, b_glu_ssm, w_out_ssm, norm_final):
    w_in_t = jnp.swapaxes(w_in_gdn[0], 0, 1)
    w_ba = jnp.pad(w_in_t[MAIN_DIM:], ((0, 128 - 2 * V_HEADS), (0, 0))).astype(BF16)
    zeros = jnp.zeros((V_HEADS,), F32)
    pad = jnp.zeros((128 - 2 * V_HEADS,), F32)
    gate_params = jnp.stack([jnp.concatenate([zeros, a_log_gdn[0], pad]),
                             jnp.concatenate([zeros, dt_bias_gdn[0], pad])])

    lr, li, ldt = lam_re[0], lam_im[0], log_dt_ssm[0]
    gp = SSM_GROUPS * SSM_STATE
    pw_r, pw_i = _ssm_pow(lr.reshape(1, gp), li.reshape(1, gp),
                          jnp.repeat(ldt, SSM_STATE).reshape(1, gp))
    bblk, cblk = _ssm_blocks(lr, li, ldt.reshape(SSM_GROUPS, 1),
                                 jnp.swapaxes(b_re[0], 1, 2), jnp.swapaxes(b_im[0], 1, 2), c_re[0], c_im[0])
    return dict(
        norm_gdn=norm_gdn[0], w_main=w_in_t, w_ba=w_ba, conv_w=conv_gdn[0], gate_params=gate_params,
        o_gain=onorm_gdn[0], w_out_gdn=w_out_gdn[0].astype(BF16),
        norm_ssm=norm_ssm[0], w_in_ssm=w_in_ssm[0], pw_r=pw_r, pw_i=pw_i, bblk=bblk, cblk=cblk,
        d_skip=d_ssm[0], w_glu=w_glu_ssm[0].astype(BF16), b_glu=b_glu_ssm[0], w_out_ssm=w_out_ssm[0].astype(BF16),
        norm_final=norm_final)


def _ssm_tail(p, x1, y, uz, *, tm, tm_glu):
    y3 = _glu(y, p["w_glu"], p["b_glu"], uz, tm=tm_glu, tn=512)
    return _outproj(y3, p["w_out_ssm"], x1, p["norm_final"], tm=tm, name="ssm_out")


def _prompt_path(p, x_prompt):
    batch, t_len, d = x_prompt.shape
    x = x_prompt.reshape(batch * t_len, d)
    proj, ba = _norm_matmul(x, p["norm_gdn"], p["w_main"], p["w_ba"], n=MAIN_DIM, tm=1024, tn=512, name="gdn_in",
                            w_transposed=True)
    og, delta = _gdn_prompt(proj, ba, p["conv_w"], p["gate_params"], p["o_gain"], batch=batch, t_len=t_len)
    conv_state = proj.reshape(batch, t_len, MAIN_DIM)[:, t_len - (CONV_W - 1):, :CONV_DIM]
    x1 = _outproj(og, p["w_out_gdn"], x, tm=512, name="gdn_out")
    uz = _norm_matmul(x1, p["norm_ssm"], p["w_in_ssm"], n=2 * SSM_WIDTH, tm=1024, tn=512, name="ssm_in")
    y, h_re, h_im = _ssm_scan(uz, p["bblk"], p["cblk"], p["pw_r"], p["pw_i"], p["d_skip"],
                              batch=batch, t_len=t_len)
    out = _ssm_tail(p, x1, y, uz, tm=512, tm_glu=512)
    return (out.reshape(batch, t_len, d), conv_state[None], delta[None],
            h_re.reshape(1, batch, SSM_GROUPS, SSM_STATE), h_im.reshape(1, batch, SSM_GROUPS, SSM_STATE))


def _sample_path(p, x_sample, state_conv, state_delta, state_re, state_im):
    nb, _, d = x_sample.shape
    x = x_sample.reshape(nb, d)
    proj, ba = _norm_matmul(x, p["norm_gdn"], p["w_main"], p["w_ba"], n=MAIN_DIM, tm=nb, tn=512, name="gdn_in_s",
                            w_transposed=True)
    beta, eg = _gdn_gates(ba, p["gate_params"])
    xs = jnp.concatenate([state_conv[0], proj[:, None, :CONV_DIM]], axis=1)
    slots = CONV_DIM // DK
    delta, og = _gdn_decode(beta[:, :V_HEADS], eg[:, V_HEADS:2 * V_HEADS],
                            xs.reshape(nb, CONV_W, slots, DK), p["conv_w"].reshape(CONV_W, slots, DK),
                            proj[:, CONV_DIM:].reshape(nb, V_HEADS, DV), p["o_gain"], state_delta[0])
    x1 = _outproj(og.reshape(nb, VAL_DIM).astype(BF16), p["w_out_gdn"], x, tm=nb, name="gdn_out_s")
    uz = _norm_matmul(x1, p["norm_ssm"], p["w_in_ssm"], n=2 * SSM_WIDTH, tm=nb, tn=512, name="ssm_in_s")
    gp = SSM_GROUPS * SSM_STATE
    y, h_re, h_im = _ssm_decode(uz, p["bblk"], p["cblk"], p["pw_r"], p["pw_i"], p["d_skip"],
                                state_re[0].reshape(nb, gp), state_im[0].reshape(nb, gp))
    out = _ssm_tail(p, x1, y, uz, tm=nb, tm_glu=nb)
    return (out.reshape(nb, 1, d), xs[:, 1:][None], delta[None],
            h_re.reshape(1, nb, SSM_GROUPS, SSM_STATE), h_im.reshape(1, nb, SSM_GROUPS, SSM_STATE))


def kernel(x_prompt, x_sample, state_gdn_conv, state_gdn_delta, state_ssm_re, state_ssm_im, norm_gdn, w_in_gdn, conv_gdn, a_log_gdn, dt_bias_gdn, onorm_gdn, w_out_gdn, norm_ssm, w_in_ssm, lam_re, lam_im, b_re, b_im, c_re, c_im, d_ssm, log_dt_ssm, w_glu_ssm, b_glu_ssm, w_out_ssm, norm_final):
    p = _prep_weights(norm_gdn, w_in_gdn, conv_gdn, a_log_gdn, dt_bias_gdn, onorm_gdn, w_out_gdn,
                      norm_ssm, w_in_ssm, lam_re, lam_im, b_re, b_im, c_re, c_im, d_ssm, log_dt_ssm,
                      w_glu_ssm, b_glu_ssm, w_out_ssm, norm_final)
    y_p, conv_p, delta_p, re_p, im_p = _prompt_path(p, x_prompt)
    y_s, conv_s, delta_s, re_s, im_s = _sample_path(p, x_sample, state_gdn_conv, state_gdn_delta,
                                                    state_ssm_re, state_ssm_im)
    return (y_p, y_s, conv_p, delta_p, re_p, im_p, conv_s, delta_s, re_s, im_s)
```

```python
import functools
import math

import jax
import jax.numpy as jnp
from jax import lax
from jax.experimental import pallas as pl
from jax.experimental.pallas import tpu as pltpu

F32 = jnp.float32
BF16 = jnp.bfloat16

RMS_EPS = 1e-6
L2_EPS = 1e-6

D_MODEL = 2048
QK_HEADS = 16
V_HEADS = 32
DK = 128
DV = 128
KEY_DIM = QK_HEADS * DK
VAL_DIM = V_HEADS * DV
CONV_DIM = 2 * KEY_DIM + VAL_DIM
CONV_W = 4
MAIN_DIM = CONV_DIM + VAL_DIM

SSM_WIDTH = 4096
SSM_GROUP = 16
SSM_GROUPS = SSM_WIDTH // SSM_GROUP
SSM_STATE = 64
GROUPS_PER_STEP = 16
CH_PER_STEP = GROUPS_PER_STEP * SSM_GROUP
ST_PER_STEP = GROUPS_PER_STEP * SSM_STATE
N_GSTEPS = SSM_GROUPS // GROUPS_PER_STEP

CHUNK = 128
SCAN_BLOCK = 1024
SCAN_SEG = SCAN_BLOCK // 8
POW_ROWS = 8

VMEM_LIMIT = 56 * 1024 * 1024


def _params(sem):
    return pltpu.CompilerParams(dimension_semantics=sem, vmem_limit_bytes=VMEM_LIMIT)


def _mm(a, b):
    return jnp.dot(a.astype(BF16), b.astype(BF16), preferred_element_type=F32)


def _mm_hi(a, b):
    return jnp.dot(a, b, precision=lax.Precision.HIGHEST, preferred_element_type=F32)


def _silu(x):
    return x * jax.nn.sigmoid(x)


def _rms(x, g):
    ms = jnp.mean(x * x, axis=-1, keepdims=True)
    return x * lax.rsqrt(ms + RMS_EPS) * g


def _norm_matmul_body(x_ref, g_ref, w_ref, *rest, has_aux, w_transposed):
    if has_aux:
        waux_ref, o_ref, aux_ref, h_scr = rest
    else:
        o_ref, h_scr = rest
    mm = _dot_nt if w_transposed else _dotb

    @pl.when(pl.program_id(1) == 0)
    def _():
        h = _rms(x_ref[...], g_ref[...]).astype(BF16)
        h_scr[...] = h
        if has_aux:
            aux_ref[...] = mm(h, waux_ref[...])

    o_ref[...] = mm(h_scr[...], w_ref[...].astype(BF16))


def _norm_matmul(x, g, w, w_aux=None, *, n, tm, tn, name, w_transposed=False):
    m, d = x.shape
    has_aux = w_aux is not None
    in_specs = [
        pl.BlockSpec((tm, d), lambda i, j: (i, 0)),
        pl.BlockSpec((1, d), lambda i, j: (0, 0)),
        pl.BlockSpec((tn, d), lambda i, j: (j, 0)) if w_transposed else pl.BlockSpec((d, tn), lambda i, j: (0, j)),
    ]
    out_shape = [jax.ShapeDtypeStruct((m, n), F32)]
    out_specs = [pl.BlockSpec((tm, tn), lambda i, j: (i, j))]
    args = [x, g.reshape(1, d), w]
    if has_aux:
        na = w_aux.shape[0] if w_transposed else w_aux.shape[1]
        in_specs.append(pl.BlockSpec(w_aux.shape, lambda i, j: (0, 0)))
        out_shape.append(jax.ShapeDtypeStruct((m, na), F32))
        out_specs.append(pl.BlockSpec((tm, na), lambda i, j: (i, 0)))
        args.append(w_aux)
    res = pl.pallas_call(
        functools.partial(_norm_matmul_body, has_aux=has_aux, w_transposed=w_transposed),
        out_shape=out_shape,
        grid=(m // tm, n // tn),
        in_specs=in_specs,
        out_specs=out_specs,
        scratch_shapes=[pltpu.VMEM((tm, d), BF16)],
        compiler_params=_params(("parallel", "arbitrary")),
        name=name,
    )(*args)
    return res if has_aux else res[0]


def _outproj_body(a_ref, w_ref, x_ref, *rest, final_norm):
    if final_norm:
        g_ref, o_ref = rest
    else:
        (o_ref,) = rest
    y = x_ref[...] + jnp.dot(a_ref[...], w_ref[...], preferred_element_type=F32)
    o_ref[...] = _rms(y, g_ref[...]) if final_norm else y


def _outproj(a, w, x, g=None, *, tm, name):
    m, kd = a.shape
    n = w.shape[1]
    final_norm = g is not None
    in_specs = [
        pl.BlockSpec((tm, kd), lambda i: (i, 0)),
        pl.BlockSpec((kd, n), lambda i: (0, 0), pipeline_mode=pl.Buffered(1)),
        pl.BlockSpec((tm, n), lambda i: (i, 0)),
    ]
    args = [a, w, x]
    if final_norm:
        in_specs.append(pl.BlockSpec((1, n), lambda i: (0, 0)))
        args.append(g.reshape(1, n))
    return pl.pallas_call(
        functools.partial(_outproj_body, final_norm=final_norm),
        out_shape=jax.ShapeDtypeStruct((m, n), F32),
        grid=(m // tm,),
        in_specs=in_specs,
        out_specs=pl.BlockSpec((tm, n), lambda i: (i, 0)),
        compiler_params=_params(("parallel",)),
        name=name,
    )(*args)


def _conv_silu_chunk(x_ref, cw, n):
    cc = x_ref.shape[1]
    r0 = pl.multiple_of(n * CHUNK, CHUNK)
    cur = x_ref[pl.ds(r0, CHUNK), :]
    p0 = pl.multiple_of(jnp.maximum(r0 - 8, 0), 8)
    prev = jnp.where(n > 0, x_ref[pl.ds(p0, 8), :], 0.0)
    row8 = lax.broadcasted_iota(jnp.int32, (8, cc), 0)
    y = cur * cw[CONV_W - 1:CONV_W, :]
    for s in range(1, CONV_W):
        sh = pltpu.roll(cur, s, 0)
        top = jnp.where(row8 < s, pltpu.roll(prev, s, 0), sh[0:8, :])
        sh = jnp.concatenate([top, sh[8:, :]], axis=0)
        y = y + sh * cw[CONV_W - 1 - s:CONV_W - s, :]
    return _silu(y)


INV_BASE_SHIFT = 3
V_PER_QK = V_HEADS // QK_HEADS
HQ_PER_STEP = 2
DEC_GROUP = 8
DEC_SAMPLES = 2
PH1_CHUNKS = 2


def _split2(x):
    hi = x.astype(BF16)
    return hi, (x - hi.astype(F32)).astype(BF16)


def _split3(x):
    hi = x.astype(BF16)
    r = x - hi.astype(F32)
    mid = r.astype(BF16)
    return hi, mid, (r - mid.astype(F32)).astype(BF16)


def _dotb(a, b):
    return jnp.dot(a, b, preferred_element_type=F32)


N_LEVELS = CHUNK.bit_length() - 1 - INV_BASE_SHIFT


def _chunk_tables():
    row = lax.broadcasted_iota(jnp.int32, (CHUNK, CHUNK), 0)
    col = lax.broadcasted_iota(jnp.int32, (CHUNK, CHUNK), 1)
    same = [(row >> s) == (col >> s) for s in range(INV_BASE_SHIFT, CHUNK.bit_length())]
    tab32 = jnp.stack([row == col, same[0]]).astype(F32)
    levels = [jnp.logical_and(hi, jnp.logical_not(lo)) for lo, hi in zip(same[:-1], same[1:])]
    tab16 = jnp.stack([row >= col] + levels).astype(BF16)
    return tab32, tab16


def _inv_unit_lower(mats, tab32_ref, tab16_ref, out):
    n = range(len(mats))
    eye = tab32_ref[0]
    d = [mats[i] * tab32_ref[1] for i in n]
    db = [d[i].astype(BF16) for i in n]
    t = [eye - d[i] for i in n]
    a_sp = [_split2(mats[i]) for i in n]
    p = [_dotb(db[i], db[i]) for i in n]
    yield
    t = [t[i] + _dotb(t[i].astype(BF16), p[i].astype(BF16)) for i in n]
    p = [_dotb(p[i].astype(BF16), p[i].astype(BF16)) for i in n]
    yield
    t = [t[i] + _dotb(t[i].astype(BF16), p[i].astype(BF16)) for i in n]
    yield
    for lvl in range(N_LEVELS):
        tb = [t[i].astype(BF16) for i in n]
        x = [_dotb(tb[i], a_sp[i][0] * tab16_ref[1 + lvl]) for i in n]
        yield
        t = [t[i] - _dotb(x[i].astype(BF16), tb[i]) for i in n]
        yield
    t_sp = [_split2(t[i]) for i in n]
    at = [_dotb(a_sp[i][0], t_sp[i][0]) for i in n]
    at = [at[i] + _dotb(a_sp[i][0], t_sp[i][1]) for i in n]
    at = [at[i] + _dotb(a_sp[i][1], t_sp[i][0]) for i in n]
    yield
    resid = [(eye - t[i]) - at[i] for i in n]
    out.extend(t[i] + _dotb(t_sp[i][0], resid[i].astype(BF16)) for i in n)
    yield


def _trace_interleaved(*gens):
    live = list(gens)
    while live:
        for g in list(live):
            try:
                next(g)
            except StopIteration:
                live.remove(g)


def _gdn_prompt_body(q_ref, k_ref, v_ref, z_ref, ba_ref, cwq_ref, cwk_ref, cwv_ref, gp_ref, on_ref,
                     tab32_ref, tab16_ref, og_ref, s_ref, u_s, w_s, qd_s, at_s, kdt_s, gl_s, a_s, rhs_s):
    hq = pl.program_id(1)
    t_len = q_ref.shape[0]
    n_chunks = t_len // CHUNK
    row = lax.broadcasted_iota(jnp.int32, (CHUNK, CHUNK), 0)
    col = lax.broadcasted_iota(jnp.int32, (CHUNK, CHUNK), 1)
    causal = row >= col
    strict = row > col
    cwq = cwq_ref[...]
    cwk = cwk_ref[...]
    cwv = cwv_ref[...]
    neg_a = -jnp.exp(gp_ref[0:1, :])
    dt_bias = gp_ref[1:2, :]

    nv = HQ_PER_STEP * V_PER_QK
    n_iters = n_chunks // PH1_CHUNKS

    def chunk_prep(n, slot, base):
        r0 = pl.multiple_of(n * CHUNK, CHUNK)
        rows = pl.ds(r0, CHUNK)
        qc = _conv_silu_chunk(q_ref, cwq, n)
        yield
        kc = _conv_silu_chunk(k_ref, cwk, n)
        yield
        vc = _conv_silu_chunk(v_ref, cwv, n)
        yield
        ba = ba_ref[rows, :]
        beta_all = jax.nn.sigmoid(ba)
        xa = ba + dt_bias
        g_all = neg_a * (jnp.maximum(xa, 0.0) + jnp.log1p(jnp.exp(-jnp.abs(xa))))
        g_hi, g_mid, g_lo = _split3(g_all)
        tril = tab16_ref[0]
        gcum = _dotb(tril, g_hi) + (_dotb(tril, g_mid) + _dotb(tril, g_lo))
        gcum_t = gcum.T
        yield
        for q in range(HQ_PER_STEP):
            qh = qc[:, q * DK:(q + 1) * DK]
            kh = kc[:, q * DK:(q + 1) * DK]
            qn = qh * lax.rsqrt(jnp.sum(qh * qh, axis=-1, keepdims=True) + L2_EPS) * (DK ** -0.5)
            kn = kh * lax.rsqrt(jnp.sum(kh * kh, axis=-1, keepdims=True) + L2_EPS)
            kn_t = kn.T
            kn_tb = kn_t.astype(BF16)
            kk = _dotb(kn.astype(BF16), kn_tb)
            qk = _dotb(qn.astype(BF16), kn_tb)
            yield
            for j in range(V_PER_QK):
                h = q * V_PER_QK + j
                lane_b = (hq * HQ_PER_STEP + q) * V_PER_QK + j
                lane_a = V_HEADS + lane_b
                gcol = jnp.sum(jnp.where(col == lane_a, gcum, 0.0), axis=1, keepdims=True)
                bcol = jnp.sum(jnp.where(col == lane_b, beta_all, 0.0), axis=1, keepdims=True)
                grow = jnp.sum(jnp.where(row == lane_a, gcum_t, 0.0), axis=0, keepdims=True)
                glast = gcol[CHUNK - 1:CHUNK, :]
                decay = jnp.exp(jnp.where(causal, gcol - grow, -jnp.inf))
                a_mat = jnp.where(strict, kk * decay, 0.0) * bcol
                eg = jnp.exp(gcol)
                rhs = jnp.concatenate([vc[:, h * DV:(h + 1) * DV] * bcol, kn * (bcol * eg)], axis=1).astype(BF16)
                qd_s[h, rows, :] = (qn * eg).astype(BF16)
                at_s[h, rows, :] = (qk * decay).astype(BF16)
                kdt_s[h, rows, :] = (kn_t * jnp.exp(glast - grow)).astype(BF16)
                gl_s[h, pl.ds(n, 1), :] = jnp.broadcast_to(jnp.exp(glast), (1, DV))
                a_s[slot, base + h] = a_mat
                rhs_s[slot, base + h] = rhs
                yield

    def prep(it):
        it = jnp.asarray(it, jnp.int32)
        slot = it % 2
        for c in range(PH1_CHUNKS):
            yield from chunk_prep(PH1_CHUNKS * it + c, slot, c * nv)

    def solve(it):
        it = jnp.asarray(it, jnp.int32)
        slot = it % 2
        idx = range(PH1_CHUNKS * nv)
        t_mats = []
        yield from _inv_unit_lower([a_s[slot, i] for i in idx], tab32_ref, tab16_ref, t_mats)
        sols = [_dotb(t_mats[i].astype(BF16), rhs_s[slot, i]) for i in idx]
        yield
        for i in idx:
            rows = pl.ds(pl.multiple_of((PH1_CHUNKS * it + i // nv) * CHUNK, CHUNK), CHUNK)
            u_s[i % nv, rows, :] = sols[i][:, :DV]
            w_s[i % nv, rows, :] = sols[i][:, DV:].astype(BF16)
        yield

    gain = on_ref[...]
    heads = range(nv)

    def recur(it):
        it = jnp.asarray(it, jnp.int32)
        for c in range(PH1_CHUNKS):
            n = PH1_CHUNKS * it + c
            rows = pl.ds(pl.multiple_of(n * CHUNK, CHUNK), CHUNK)
            states = [s_ref[0, h] for h in heads]
            wq = [jnp.concatenate([w_s[h, rows, :], qd_s[h, rows, :]], axis=0) for h in heads]
            ws_qs = [_dotb(wq[h], states[h].astype(BF16)) for h in heads]
            yield
            v_new = [(u_s[h, rows, :] - ws_qs[h][:CHUNK]).astype(BF16) for h in heads]
            o = [ws_qs[h][CHUNK:] + _dotb(at_s[h, rows, :], v_new[h]) for h in heads]
            s_new = [states[h] * gl_s[h, pl.ds(n, 1), :] + _dotb(kdt_s[h, rows, :], v_new[h]) for h in heads]
            yield
            for h in heads:
                s_ref[0, h] = s_new[h]
                zg = z_ref[rows, h * DV:(h + 1) * DV]
                og = o[h] * lax.rsqrt(jnp.mean(o[h] * o[h], axis=-1, keepdims=True) + RMS_EPS) * gain * _silu(zg)
                og_ref[rows, h * DV:(h + 1) * DV] = og.astype(BF16)
            yield

    for h in heads:
        s_ref[0, h] = jnp.zeros((DK, DV), F32)
    _trace_interleaved(prep(0))
    _trace_interleaved(solve(0), prep(1))

    def steady(it, carry):
        _trace_interleaved(solve(it - 1), prep(it), recur(it - 2))
        return carry

    lax.fori_loop(2, n_iters, steady, 0)
    _trace_interleaved(solve(n_iters - 1), recur(n_iters - 2))
    _trace_interleaved(recur(n_iters - 1))


def _gdn_prompt(proj, ba, conv_w, gate_params, o_gain, *, batch, t_len):
    wq = HQ_PER_STEP * DK
    wv = HQ_PER_STEP * V_PER_QK * DV
    nv = HQ_PER_STEP * V_PER_QK
    in_specs = [
        pl.BlockSpec((t_len, wq), lambda b, h: (b, h)),
        pl.BlockSpec((t_len, wq), lambda b, h: (b, KEY_DIM // wq + h)),
        pl.BlockSpec((t_len, wv), lambda b, h: (b, (2 * KEY_DIM) // wv + h)),
        pl.BlockSpec((t_len, wv), lambda b, h: (b, CONV_DIM // wv + h)),
        pl.BlockSpec((t_len, 128), lambda b, h: (b, 0)),
        pl.BlockSpec((CONV_W, wq), lambda b, h: (0, h)),
        pl.BlockSpec((CONV_W, wq), lambda b, h: (0, KEY_DIM // wq + h)),
        pl.BlockSpec((CONV_W, wv), lambda b, h: (0, (2 * KEY_DIM) // wv + h)),
        pl.BlockSpec((2, 128), lambda b, h: (0, 0)),
        pl.BlockSpec((1, DV), lambda b, h: (0, 0)),
        pl.BlockSpec((2, CHUNK, CHUNK), lambda b, h: (0, 0, 0)),
        pl.BlockSpec((1 + N_LEVELS, CHUNK, CHUNK), lambda b, h: (0, 0, 0)),
    ]
    tab32, tab16 = _chunk_tables()
    out_shape = [
        jax.ShapeDtypeStruct((batch * t_len, VAL_DIM), BF16),
        jax.ShapeDtypeStruct((batch, V_HEADS, DK, DV), F32),
    ]
    out_specs = [
        pl.BlockSpec((t_len, wv), lambda b, h: (b, h)),
        pl.BlockSpec((1, nv, DK, DV), lambda b, h: (b, h, 0, 0)),
    ]
    scratch = [
        pltpu.VMEM((nv, t_len, DV), F32),
        pltpu.VMEM((nv, t_len, DK), BF16),
        pltpu.VMEM((nv, t_len, DK), BF16),
        pltpu.VMEM((nv, t_len, CHUNK), BF16),
        pltpu.VMEM((nv, t_len, CHUNK), BF16),
        pltpu.VMEM((nv, t_len // CHUNK, DV), F32),
        pltpu.VMEM((2, PH1_CHUNKS * nv, CHUNK, CHUNK), F32),
        pltpu.VMEM((2, PH1_CHUNKS * nv, CHUNK, DV + DK), BF16),
    ]
    return pl.pallas_call(
        _gdn_prompt_body,
        out_shape=out_shape,
        grid=(batch, QK_HEADS // HQ_PER_STEP),
        in_specs=in_specs,
        out_specs=out_specs,
        scratch_shapes=scratch,
        compiler_params=_params(("parallel", "arbitrary")),
        name="gdn_prompt",
    )(proj, proj, proj, proj, ba, conv_w, conv_w, conv_w, gate_params, o_gain.reshape(1, DV), tab32, tab16)


def _gdn_gates_body(ba_ref, gp_ref, beta_ref, eg_ref):
    ba = ba_ref[...]
    beta_ref[...] = jax.nn.sigmoid(ba)
    xa = ba + gp_ref[1:2, :]
    g = -jnp.exp(gp_ref[0:1, :]) * (jnp.maximum(xa, 0.0) + jnp.log1p(jnp.exp(-jnp.abs(xa))))
    eg_ref[...] = jnp.exp(g)


def _gdn_gates(ba, gate_params):
    m = ba.shape[0]
    return pl.pallas_call(
        _gdn_gates_body,
        out_shape=[jax.ShapeDtypeStruct((m, 128), F32)] * 2,
        name="gdn_gates",
    )(ba, gate_params)


def _gdn_decode_body(beta_ref, eg_ref, xs_ref, cw_ref, z_ref, on_ref, s_ref, so_ref, og_ref):
    for si in range(DEC_SAMPLES):
        b = pl.program_id(0) * DEC_SAMPLES + si
        y = xs_ref[si, 0] * cw_ref[0]
        for j in range(1, CONV_W):
            y = y + xs_ref[si, j] * cw_ref[j]
        y = _silu(y)
        q = y[0:QK_HEADS]
        k = y[QK_HEADS:2 * QK_HEADS]
        v = y[2 * QK_HEADS:]
        qn = q * lax.rsqrt(jnp.sum(q * q, axis=-1, keepdims=True) + L2_EPS) * (DK ** -0.5)
        kn = k * lax.rsqrt(jnp.sum(k * k, axis=-1, keepdims=True) + L2_EPS)
        qk_t = jnp.concatenate([qn, kn, jnp.zeros((DK - 2 * QK_HEADS, DK), F32)], axis=0).T
        for g0 in range(0, V_HEADS, DEC_GROUP):
            hs = range(g0, g0 + DEC_GROUP)
            qcol = {h: qk_t[:, h // V_PER_QK:h // V_PER_QK + 1] for h in hs}
            kcol = {h: qk_t[:, QK_HEADS + h // V_PER_QK:QK_HEADS + h // V_PER_QK + 1] for h in hs}
            sd = {h: s_ref[si, h] * eg_ref[b, h] for h in hs}
            ks = {h: jnp.sum(sd[h] * kcol[h], axis=0, keepdims=True) for h in hs}
            v_new = {h: beta_ref[b, h] * (v[h:h + 1, :] - ks[h]) for h in hs}
            sn = {h: sd[h] + kcol[h] * v_new[h] for h in hs}
            for h in hs:
                so_ref[si, h] = sn[h]
            o = {h: jnp.sum(sn[h] * qcol[h], axis=0, keepdims=True) for h in hs}
            for h in hs:
                og_ref[si, h:h + 1, :] = o[h]
        o_all = og_ref[si]
        og_ref[si] = (o_all * lax.rsqrt(jnp.mean(o_all * o_all, axis=-1, keepdims=True) + RMS_EPS)
                      * on_ref[...] * _silu(z_ref[si]))


def _gdn_decode(beta, eg, xs, conv_w3, z3, o_gain, state):
    nb = xs.shape[0]
    slots = CONV_DIM // DK
    smem = pl.BlockSpec(memory_space=pltpu.SMEM)
    return pl.pallas_call(
        _gdn_decode_body,
        out_shape=[
            jax.ShapeDtypeStruct((nb, V_HEADS, DK, DV), F32),
            jax.ShapeDtypeStruct((nb, V_HEADS, DV), F32),
        ],
        grid=(nb // DEC_SAMPLES,),
        in_specs=[
            smem,
            smem,
            pl.BlockSpec((DEC_SAMPLES, CONV_W, slots, DK), lambda b: (b, 0, 0, 0)),
            pl.BlockSpec((CONV_W, slots, DK), lambda b: (0, 0, 0)),
            pl.BlockSpec((DEC_SAMPLES, V_HEADS, DV), lambda b: (b, 0, 0)),
            pl.BlockSpec((1, DV), lambda b: (0, 0)),
            pl.BlockSpec((DEC_SAMPLES, V_HEADS, DK, DV), lambda b: (b, 0, 0, 0)),
        ],
        out_specs=[
            pl.BlockSpec((DEC_SAMPLES, V_HEADS, DK, DV), lambda b: (b, 0, 0, 0)),
            pl.BlockSpec((DEC_SAMPLES, V_HEADS, DV), lambda b: (b, 0, 0)),
        ],
        compiler_params=_params(("parallel",)),
        name="gdn_decode",
    )(beta, eg, xs, conv_w3, z3, o_gain.reshape(1, DV), state)


def _ssm_pow_body(lr_ref, li_ref, ldt_ref, pr_ref, pi_ref):
    n = jnp.where(lax.broadcasted_iota(jnp.int32, pr_ref.shape, 0) == 0, 1.0, float(SCAN_SEG))
    dt = jnp.exp(ldt_ref[...])
    lr = jnp.minimum(lr_ref[...], -1e-4) * dt
    li = li_ref[...] * dt
    mag = jnp.exp(n * lr)
    pr_ref[...] = mag * jnp.cos(n * li)
    pi_ref[...] = mag * jnp.sin(n * li)


def _ssm_pow(lr, li, ldt):
    gp = lr.shape[1]
    tl = 2048
    spec = pl.BlockSpec((1, tl), lambda i: (0, i))
    ospec = pl.BlockSpec((POW_ROWS, tl), lambda i: (0, i))
    return pl.pallas_call(
        _ssm_pow_body,
        out_shape=[jax.ShapeDtypeStruct((POW_ROWS, gp), F32)] * 2,
        grid=(gp // tl,),
        in_specs=[spec, spec, spec],
        out_specs=[ospec, ospec],
        name="ssm_pow",
    )(lr, li, ldt)


def _ssm_blocks_body(lr_ref, li_ref, ldt_ref, br_ref, bi_ref, cr_ref, ci_ref, bblk_ref, cblk_ref):
    dt = jnp.exp(ldt_ref[...])
    lr = jnp.minimum(lr_ref[...], -1e-4)
    li = li_ref[...]
    mag = jnp.exp(lr * dt)
    xr = mag * jnp.cos(li * dt) - 1.0
    xi = mag * jnp.sin(li * dt)
    den = 1.0 / (lr * lr + li * li)
    fr = (xr * lr + xi * li) * den
    fi = (xi * lr - xr * li) * den
    bblk_ref[...] = jnp.zeros_like(bblk_ref)
    cblk_ref[...] = jnp.zeros_like(cblk_ref)
    per_tile = 128 // SSM_STATE
    for g in range(GROUPS_PER_STEP):
        rows = slice(g * SSM_GROUP, (g + 1) * SSM_GROUP)
        off = (g // per_tile) * 256 + (g % per_tile) * SSM_STATE
        cols = slice(off, off + SSM_STATE)
        cols_im = slice(off + 128, off + 128 + SSM_STATE)
        br = br_ref[g]
        bi = bi_ref[g]
        frg = fr[g:g + 1, :]
        fig = fi[g:g + 1, :]
        bblk_ref[rows, cols] = (frg * br - fig * bi).astype(BF16)
        bblk_ref[rows, cols_im] = (frg * bi + fig * br).astype(BF16)
        cblk_ref[rows, cols] = cr_ref[g].astype(BF16)
        cblk_ref[rows, cols_im] = (-ci_ref[g]).astype(BF16)


def _ssm_blocks(lr, li, ldt, bt_re, bt_im, c_re, c_im):
    g, p = lr.shape
    k = GROUPS_PER_STEP
    s2 = pl.BlockSpec((k, p), lambda i: (i, 0))
    s1 = pl.BlockSpec((k, 1), lambda i: (i, 0))
    s3 = pl.BlockSpec((k, SSM_GROUP, p), lambda i: (i, 0, 0))
    blk = pl.BlockSpec((None, CH_PER_STEP, 2 * ST_PER_STEP), lambda i: (i, 0, 0))
    return pl.pallas_call(
        _ssm_blocks_body,
        out_shape=[jax.ShapeDtypeStruct((g // k, CH_PER_STEP, 2 * ST_PER_STEP), BF16)] * 2,
        grid=(g // k,),
        in_specs=[s2, s2, s1, s3, s3, s3, s3],
        out_specs=[blk, blk],
        name="ssm_blocks",
    )(lr, li, ldt, bt_re, bt_im, c_re, c_im)


def _dot_nt(a, b):
    return lax.dot_general(a, b, (((1,), (1,)), ((), ())), preferred_element_type=F32)


def _gelu_tanh(x):
    return 0.5 * x * (1.0 + jnp.tanh(math.sqrt(2.0 / math.pi) * (x + 0.044715 * (x * x * x))))


def _ssm_scan_body(ua_ref, ub_ref, b_ref, c_ref, pr_ref, pi_ref, d_ref, y_ref, hr_ref, hi_ref,
                   h_s, y_s, cr_s, ci_s):
    tb = pl.program_id(2)
    nt = ST_PER_STEP // 128
    u_perm = jnp.concatenate(
        [jnp.concatenate([ua_ref[pl.ds(i, 8, stride=SCAN_SEG), :], ub_ref[pl.ds(i, 8, stride=SCAN_SEG), :]], axis=1)
         for i in range(SCAN_SEG)], axis=0).astype(BF16)

    @pl.when(tb == 0)
    def _():
        cr_s[...] = jnp.zeros_like(cr_s)
        ci_s[...] = jnp.zeros_like(ci_s)

    def b_proj(c):
        bu = jnp.dot(u_perm, b_ref[:, c * 256:(c + 1) * 256], preferred_element_type=F32)
        h_s[c] = bu[:, :128]
        h_s[nt + c] = bu[:, 128:]

    row8 = lax.broadcasted_iota(jnp.int32, (8, 128), 0)
    b_proj(0)
    y = None
    for c in range(nt):
        if c + 1 < nt:
            b_proj(c + 1)
        lanes = slice(c * 128, (c + 1) * 128)
        ar = jnp.broadcast_to(pr_ref[0:1, lanes], (8, 128))
        ai = jnp.broadcast_to(pi_ref[0:1, lanes], (8, 128))
        hr = jnp.zeros((8, 128), F32)
        hi = jnp.zeros((8, 128), F32)
        for i in range(SCAN_SEG):
            xr = h_s[c, i * 8:(i + 1) * 8, :]
            xi = h_s[nt + c, i * 8:(i + 1) * 8, :]
            hr, hi = ar * hr - ai * hi + xr, ar * hi + ai * hr + xi

        asr = pr_ref[1:2, lanes]
        asi = pi_ref[1:2, lanes]
        c_r = cr_s[:, lanes]
        c_i = ci_s[:, lanes]
        cin_r = jnp.zeros((8, 128), F32)
        cin_i = jnp.zeros((8, 128), F32)
        for s in range(8):
            cin_r = jnp.where(row8 == s, c_r, cin_r)
            cin_i = jnp.where(row8 == s, c_i, cin_i)
            e_r = hr[s:s + 1, :]
            e_i = hi[s:s + 1, :]
            c_r, c_i = e_r + asr * c_r - asi * c_i, e_i + asr * c_i + asi * c_r
        cr_s[:, lanes] = c_r
        ci_s[:, lanes] = c_i

        hr, hi = cin_r, cin_i
        for i in range(SCAN_SEG):
            xr = h_s[c, i * 8:(i + 1) * 8, :]
            xi = h_s[nt + c, i * 8:(i + 1) * 8, :]
            hr, hi = ar * hr - ai * hi + xr, ar * hi + ai * hr + xi
            h_s[c, i * 8:(i + 1) * 8, :] = hr
            h_s[nt + c, i * 8:(i + 1) * 8, :] = hi

        h_cat = jnp.concatenate([h_s[c], h_s[nt + c]], axis=1).astype(BF16)
        y_c = _dot_nt(h_cat, c_ref[:, c * 256:(c + 1) * 256])
        y = y_c if y is None else y + y_c

    y_s[0] = y[:, :128]
    y_s[1] = y[:, 128:]
    per_seg = SCAN_SEG // 8
    y_nat = jnp.concatenate(
        [jnp.concatenate([y_s[h, pl.ds((j % per_seg) * 64 + j // per_seg, 8, stride=8), :] for h in range(2)], axis=1)
         for j in range(SCAN_BLOCK // 8)], axis=0)
    u = jnp.concatenate([ua_ref[...], ub_ref[...]], axis=1)
    y_ref[...] = _gelu_tanh(y_nat + d_ref[...] * u)

    @pl.when(tb == pl.num_programs(2) - 1)
    def _():
        hr_ref[...] = cr_s[...]
        hi_ref[...] = ci_s[...]


def _ssm_scan(uz, bblk, cblk, pw_r, pw_i, d_skip, *, batch, t_len):
    nt = t_len // SCAN_BLOCK
    return pl.pallas_call(
        _ssm_scan_body,
        out_shape=[
            jax.ShapeDtypeStruct((batch * t_len, SSM_WIDTH), F32),
            jax.ShapeDtypeStruct((batch, 1, SSM_GROUPS * SSM_STATE), F32),
            jax.ShapeDtypeStruct((batch, 1, SSM_GROUPS * SSM_STATE), F32),
        ],
        grid=(batch, N_GSTEPS, nt),
        in_specs=[
            pl.BlockSpec((SCAN_BLOCK, 128), lambda b, g, t: (b * nt + t, 2 * g)),
            pl.BlockSpec((SCAN_BLOCK, 128), lambda b, g, t: (b * nt + t, 2 * g + 1)),
            pl.BlockSpec((None, CH_PER_STEP, 2 * ST_PER_STEP), lambda b, g, t: (g, 0, 0)),
            pl.BlockSpec((None, CH_PER_STEP, 2 * ST_PER_STEP), lambda b, g, t: (g, 0, 0)),
            pl.BlockSpec((POW_ROWS, ST_PER_STEP), lambda b, g, t: (0, g)),
            pl.BlockSpec((POW_ROWS, ST_PER_STEP), lambda b, g, t: (0, g)),
            pl.BlockSpec((1, CH_PER_STEP), lambda b, g, t: (0, g)),
        ],
        out_specs=[
            pl.BlockSpec((SCAN_BLOCK, CH_PER_STEP), lambda b, g, t: (b * nt + t, g)),
            pl.BlockSpec((None, 1, ST_PER_STEP), lambda b, g, t: (b, 0, g)),
            pl.BlockSpec((None, 1, ST_PER_STEP), lambda b, g, t: (b, 0, g)),
        ],
        scratch_shapes=[
            pltpu.VMEM((2 * ST_PER_STEP // 128, SCAN_BLOCK, 128), F32),
            pltpu.VMEM((CH_PER_STEP // 128, SCAN_BLOCK, 128), F32),
            pltpu.VMEM((1, ST_PER_STEP), F32),
            pltpu.VMEM((1, ST_PER_STEP), F32),
        ],
        compiler_params=_params(("parallel", "parallel", "arbitrary")),
        name="ssm_scan",
    )(uz, uz, bblk, cblk, pw_r, pw_i, d_skip.reshape(1, SSM_WIDTH))


def _ssm_decode_body(u_ref, b_ref, c_ref, pr_ref, pi_ref, d_ref, h0r_ref, h0i_ref, y_ref, hr_ref, hi_ref):
    nt = ST_PER_STEP // 128
    u = u_ref[...]
    bu = jnp.dot(u.astype(BF16), b_ref[...], preferred_element_type=F32)
    h_cat = []
    for c in range(nt):
        lanes = slice(c * 128, (c + 1) * 128)
        ar = pr_ref[0:1, lanes]
        ai = pi_ref[0:1, lanes]
        h0r = h0r_ref[:, lanes]
        h0i = h0i_ref[:, lanes]
        hr = bu[:, c * 256:c * 256 + 128] + ar * h0r - ai * h0i
        hi = bu[:, c * 256 + 128:(c + 1) * 256] + ar * h0i + ai * h0r
        hr_ref[:, lanes] = hr
        hi_ref[:, lanes] = hi
        h_cat += [hr, hi]
    y = _dot_nt(jnp.concatenate(h_cat, axis=1).astype(BF16), c_ref[...])
    y_ref[...] = _gelu_tanh(y + d_ref[...] * u)


def _ssm_decode(uz, bblk, cblk, pw_r, pw_i, d_skip, h0r, h0i):
    nb = uz.shape[0]
    return pl.pallas_call(
        _ssm_decode_body,
        out_shape=[
            jax.ShapeDtypeStruct((nb, SSM_WIDTH), F32),
            jax.ShapeDtypeStruct((nb, SSM_GROUPS * SSM_STATE), F32),
            jax.ShapeDtypeStruct((nb, SSM_GROUPS * SSM_STATE), F32),
        ],
        grid=(N_GSTEPS,),
        in_specs=[
            pl.BlockSpec((nb, CH_PER_STEP), lambda g: (0, g)),
            pl.BlockSpec((None, CH_PER_STEP, 2 * ST_PER_STEP), lambda g: (g, 0, 0)),
            pl.BlockSpec((None, CH_PER_STEP, 2 * ST_PER_STEP), lambda g: (g, 0, 0)),
            pl.BlockSpec((POW_ROWS, ST_PER_STEP), lambda g: (0, g)),
            pl.BlockSpec((POW_ROWS, ST_PER_STEP), lambda g: (0, g)),
            pl.BlockSpec((1, CH_PER_STEP), lambda g: (0, g)),
            pl.BlockSpec((nb, ST_PER_STEP), lambda g: (0, g)),
            pl.BlockSpec((nb, ST_PER_STEP), lambda g: (0, g)),
        ],
        out_specs=[
            pl.BlockSpec((nb, CH_PER_STEP), lambda g: (0, g)),
            pl.BlockSpec((nb, ST_PER_STEP), lambda g: (0, g)),
            pl.BlockSpec((nb, ST_PER_STEP), lambda g: (0, g)),
        ],
        compiler_params=_params(("parallel",)),
        name="ssm_decode",
    )(uz, bblk, cblk, pw_r, pw_i, d_skip.reshape(1, SSM_WIDTH), h0r, h0i)


GLU_SLAB = 256


def _glu_body(y_ref, w_ref, b_ref, z_ref, o_ref, yb_scr, *, tn):
    j = pl.program_id(1)

    @pl.when(j == 0)
    def _():
        yb_scr[...] = y_ref[...].astype(BF16)

    for c in range(tn // GLU_SLAB):
        cols = slice(c * GLU_SLAB, (c + 1) * GLU_SLAB)
        t = jnp.dot(yb_scr[...], w_ref[:, cols], preferred_element_type=F32) + b_ref[:, cols]
        y = y_ref[:, pl.ds(pl.multiple_of(j * tn + c * GLU_SLAB, GLU_SLAB), GLU_SLAB)]
        o_ref[:, cols] = (y * jax.nn.sigmoid(t) * _silu(z_ref[:, cols])).astype(BF16)


def _glu(y, w, b, uz, *, tm, tn):
    m, d = y.shape
    zoff = SSM_WIDTH // tn
    return pl.pallas_call(
        functools.partial(_glu_body, tn=tn),
        out_shape=jax.ShapeDtypeStruct((m, d), BF16),
        grid=(m // tm, d // tn),
        in_specs=[
            pl.BlockSpec((tm, d), lambda i, j: (i, 0)),
            pl.BlockSpec((d, tn), lambda i, j: (0, j)),
            pl.BlockSpec((1, tn), lambda i, j: (0, j)),
            pl.BlockSpec((tm, tn), lambda i, j: (i, zoff + j)),
        ],
        out_specs=pl.BlockSpec((tm, tn), lambda i, j: (i, j)),
        scratch_shapes=[pltpu.VMEM((tm, d), BF16)],
        compiler_params=_params(("parallel", "arbitrary")),
        name="ssm_glu",
    )(y, w, b.reshape(1, d), uz)


def _prep_weights(norm_gdn, w_in_gdn, conv_gdn, a_log_gdn, dt_bias_gdn, onorm_gdn, w_out_gdn,
                  norm_ssm, w_in_ssm, lam_re, lam_im, b_re, b_im, c_re, c_im, d_ssm, log_dt_ssm,
                  w_glu_ssm, b_glu_ssm, w_out_ssm, norm_final):
    w_in_t = jnp.swapaxes(w_in_gdn[0], 0, 1)
    w_ba = jnp.pad(w_in_t[MAIN_DIM:], ((0, 128 - 2 * V_HEADS), (0, 0))).astype(BF16)
    zeros = jnp.zeros((V_HEADS,), F32)
    pad = jnp.zeros((128 - 2 * V_HEADS,), F32)
    gate_params = jnp.stack([jnp.concatenate([zeros, a_log_gdn[0], pad]),
                             jnp.concatenate([zeros, dt_bias_gdn[0], pad])])

    lr, li, ldt = lam_re[0], lam_im[0], log_dt_ssm[0]
    gp = SSM_GROUPS * SSM_STATE
    pw_r, pw_i = _ssm_pow(lr.reshape(1, gp), li.reshape(1, gp),
                          jnp.repeat(ldt, SSM_STATE).reshape(1, gp))
    bblk, cblk = _ssm_blocks(lr, li, ldt.reshape(SSM_GROUPS, 1),
                                 jnp.swapaxes(b_re[0], 1, 2), jnp.swapaxes(b_im[0], 1, 2), c_re[0], c_im[0])
    return dict(
        norm_gdn=norm_gdn[0], w_main=w_in_t, w_ba=w_ba, conv_w=conv_gdn[0], gate_params=gate_params,
        o_gain=onorm_gdn[0], w_out_gdn=w_out_gdn[0].astype(BF16),
        norm_ssm=norm_ssm[0], w_in_ssm=w_in_ssm[0], pw_r=pw_r, pw_i=pw_i, bblk=bblk, cblk=cblk,
        d_skip=d_ssm[0], w_glu=w_glu_ssm[0].astype(BF16), b_glu=b_glu_ssm[0], w_out_ssm=w_out_ssm[0].astype(BF16),
        norm_final=norm_final)


def _ssm_tail(p, x1, y, uz, *, tm, tm_glu):
    y3 = _glu(y, p["w_glu"], p["b_glu"], uz, tm=tm_glu, tn=1024)
    return _outproj(y3, p["w_out_ssm"], x1, p["norm_final"], tm=tm, name="ssm_out")


def _prompt_path(p, x_prompt):
    batch, t_len, d = x_prompt.shape
    x = x_prompt.reshape(batch * t_len, d)
    proj, ba = _norm_matmul(x, p["norm_gdn"], p["w_main"], p["w_ba"], n=MAIN_DIM, tm=1024, tn=512, name="gdn_in",
                            w_transposed=True)
    og, delta = _gdn_prompt(proj, ba, p["conv_w"], p["gate_params"], p["o_gain"], batch=batch, t_len=t_len)
    conv_state = proj.reshape(batch, t_len, MAIN_DIM)[:, t_len - (CONV_W - 1):, :CONV_DIM]
    x1 = _outproj(og, p["w_out_gdn"], x, tm=512, name="gdn_out")
    uz = _norm_matmul(x1, p["norm_ssm"], p["w_in_ssm"], n=2 * SSM_WIDTH, tm=1024, tn=512, name="ssm_in")
    y, h_re, h_im = _ssm_scan(uz, p["bblk"], p["cblk"], p["pw_r"], p["pw_i"], p["d_skip"],
                              batch=batch, t_len=t_len)
    out = _ssm_tail(p, x1, y, uz, tm=512, tm_glu=512)
    return (out.reshape(batch, t_len, d), conv_state[None], delta[None],
            h_re.reshape(1, batch, SSM_GROUPS, SSM_STATE), h_im.reshape(1, batch, SSM_GROUPS, SSM_STATE))


def _sample_path(p, x_sample, state_conv, state_delta, state_re, state_im):
    nb, _, d = x_sample.shape
    x = x_sample.reshape(nb, d)
    proj, ba = _norm_matmul(x, p["norm_gdn"], p["w_main"], p["w_ba"], n=MAIN_DIM, tm=nb, tn=512, name="gdn_in_s",
                            w_transposed=True)
    beta, eg = _gdn_gates(ba, p["gate_params"])
    xs = jnp.concatenate([state_conv[0], proj[:, None, :CONV_DIM]], axis=1)
    slots = CONV_DIM // DK
    delta, og = _gdn_decode(beta[:, :V_HEADS], eg[:, V_HEADS:2 * V_HEADS],
                            xs.reshape(nb, CONV_W, slots, DK), p["conv_w"].reshape(CONV_W, slots, DK),
                            proj[:, CONV_DIM:].reshape(nb, V_HEADS, DV), p["o_gain"], state_delta[0])
    x1 = _outproj(og.reshape(nb, VAL_DIM).astype(BF16), p["w_out_gdn"], x, tm=nb, name="gdn_out_s")
    uz = _norm_matmul(x1, p["norm_ssm"], p["w_in_ssm"], n=2 * SSM_WIDTH, tm=nb, tn=512, name="ssm_in_s")
    gp = SSM_GROUPS * SSM_STATE
    y, h_re, h_im = _ssm_decode(uz, p["bblk"], p["cblk"], p["pw_r"], p["pw_i"], p["d_skip"],
                                state_re[0].reshape(nb, gp), state_im[0].reshape(nb, gp))
    out = _ssm_tail(p, x1, y, uz, tm=nb, tm_glu=nb)
    return (out.reshape(nb, 1, d), xs[:, 1:][None], delta[None],
            h_re.reshape(1, nb, SSM_GROUPS, SSM_STATE), h_im.reshape(1, nb, SSM_GROUPS, SSM_STATE))


def kernel(x_prompt, x_sample, state_gdn_conv, state_gdn_delta, state_ssm_re, state_ssm_im, norm_gdn, w_in_gdn, conv_gdn, a_log_gdn, dt_bias_gdn, onorm_gdn, w_out_gdn, norm_ssm, w_in_ssm, lam_re, lam_im, b_re, b_im, c_re, c_im, d_ssm, log_dt_ssm, w_glu_ssm, b_glu_ssm, w_out_ssm, norm_final):
    p = _prep_weights(norm_gdn, w_in_gdn, conv_gdn, a_log_gdn, dt_bias_gdn, onorm_gdn, w_out_gdn,
                      norm_ssm, w_in_ssm, lam_re, lam_im, b_re, b_im, c_re, c_im, d_ssm, log_dt_ssm,
                      w_glu_ssm, b_glu_ssm, w_out_ssm, norm_final)
    y_p, conv_p, delta_p, re_p, im_p = _prompt_path(p, x_prompt)
    y_s, conv_s, delta_s, re_s, im_s = _sample_path(p, x_sample, state_gdn_conv, state_gdn_delta,
                                                    state_ssm_re, state_ssm_im)
    return (y_p, y_s, conv_p, delta_p, re_p, im_p, conv_s, delta_s, re_s, im_s)
```

```python
import functools
import math

import jax
import jax.numpy as jnp
from jax import lax
from jax.experimental import pallas as pl
from jax.experimental.pallas import tpu as pltpu

F32 = jnp.float32
BF16 = jnp.bfloat16

RMS_EPS = 1e-6
L2_EPS = 1e-6

D_MODEL = 2048
QK_HEADS = 16
V_HEADS = 32
DK = 128
DV = 128
KEY_DIM = QK_HEADS * DK
VAL_DIM = V_HEADS * DV
CONV_DIM = 2 * KEY_DIM + VAL_DIM
CONV_W = 4
MAIN_DIM = CONV_DIM + VAL_DIM

SSM_WIDTH = 4096
SSM_GROUP = 16
SSM_GROUPS = SSM_WIDTH // SSM_GROUP
SSM_STATE = 64
GROUPS_PER_STEP = 16
CH_PER_STEP = GROUPS_PER_STEP * SSM_GROUP
ST_PER_STEP = GROUPS_PER_STEP * SSM_STATE
N_GSTEPS = SSM_GROUPS // GROUPS_PER_STEP

CHUNK = 128
SCAN_BLOCK = 1024
SCAN_SEG = SCAN_BLOCK // 8
POW_ROWS = 8
SCAN_TILES = 1
SCAN_YIELD = 4

VMEM_LIMIT = 56 * 1024 * 1024


def _params(sem):
    return pltpu.CompilerParams(dimension_semantics=sem, vmem_limit_bytes=VMEM_LIMIT)


def _mm(a, b):
    return jnp.dot(a.astype(BF16), b.astype(BF16), preferred_element_type=F32)


def _mm_hi(a, b):
    return jnp.dot(a, b, precision=lax.Precision.HIGHEST, preferred_element_type=F32)


def _silu(x):
    return x * jax.nn.sigmoid(x)


def _rms(x, g):
    ms = jnp.mean(x * x, axis=-1, keepdims=True)
    return x * lax.rsqrt(ms + RMS_EPS) * g


def _norm_matmul_body(x_ref, g_ref, w_ref, *rest, has_aux, w_transposed):
    if has_aux:
        waux_ref, o_ref, aux_ref, h_scr = rest
    else:
        o_ref, h_scr = rest
    mm = _dot_nt if w_transposed else _dotb

    @pl.when(pl.program_id(1) == 0)
    def _():
        h = _rms(x_ref[...], g_ref[...]).astype(BF16)
        h_scr[...] = h
        if has_aux:
            aux_ref[...] = mm(h, waux_ref[...])

    o_ref[...] = mm(h_scr[...], w_ref[...].astype(BF16))


def _norm_matmul(x, g, w, w_aux=None, *, n, tm, tn, name, w_transposed=False):
    m, d = x.shape
    has_aux = w_aux is not None
    in_specs = [
        pl.BlockSpec((tm, d), lambda i, j: (i, 0), pipeline_mode=pl.Buffered(1)),
        pl.BlockSpec((1, d), lambda i, j: (0, 0)),
        pl.BlockSpec((tn, d), lambda i, j: (j, 0)) if w_transposed else pl.BlockSpec((d, tn), lambda i, j: (0, j)),
    ]
    out_shape = [jax.ShapeDtypeStruct((m, n), F32)]
    out_specs = [pl.BlockSpec((tm, tn), lambda i, j: (i, j))]
    args = [x, g.reshape(1, d), w]
    if has_aux:
        na = w_aux.shape[0] if w_transposed else w_aux.shape[1]
        in_specs.append(pl.BlockSpec(w_aux.shape, lambda i, j: (0, 0)))
        out_shape.append(jax.ShapeDtypeStruct((m, na), F32))
        out_specs.append(pl.BlockSpec((tm, na), lambda i, j: (i, 0)))
        args.append(w_aux)
    res = pl.pallas_call(
        functools.partial(_norm_matmul_body, has_aux=has_aux, w_transposed=w_transposed),
        out_shape=out_shape,
        grid=(m // tm, n // tn),
        in_specs=in_specs,
        out_specs=out_specs,
        scratch_shapes=[pltpu.VMEM((tm, d), BF16)],
        compiler_params=_params(("parallel", "arbitrary")),
        name=name,
    )(*args)
    return res if has_aux else res[0]


def _outproj_body(a_ref, w_ref, x_ref, *rest, final_norm):
    if final_norm:
        g_ref, o_ref = rest
    else:
        (o_ref,) = rest
    y = x_ref[...] + jnp.dot(a_ref[...], w_ref[...], preferred_element_type=F32)
    o_ref[...] = _rms(y, g_ref[...]) if final_norm else y


def _outproj(a, w, x, g=None, *, tm, name):
    m, kd = a.shape
    n = w.shape[1]
    final_norm = g is not None
    in_specs = [
        pl.BlockSpec((tm, kd), lambda i: (i, 0)),
        pl.BlockSpec((kd, n), lambda i: (0, 0), pipeline_mode=pl.Buffered(1)),
        pl.BlockSpec((tm, n), lambda i: (i, 0)),
    ]
    args = [a, w, x]
    if final_norm:
        in_specs.append(pl.BlockSpec((1, n), lambda i: (0, 0)))
        args.append(g.reshape(1, n))
    return pl.pallas_call(
        functools.partial(_outproj_body, final_norm=final_norm),
        out_shape=jax.ShapeDtypeStruct((m, n), F32),
        grid=(m // tm,),
        in_specs=in_specs,
        out_specs=pl.BlockSpec((tm, n), lambda i: (i, 0)),
        compiler_params=_params(("parallel",)),
        name=name,
    )(*args)


def _conv_silu_chunk(x_ref, cw, n):
    cc = x_ref.shape[1]
    r0 = pl.multiple_of(n * CHUNK, CHUNK)
    cur = x_ref[pl.ds(r0, CHUNK), :]
    p0 = pl.multiple_of(jnp.maximum(r0 - 8, 0), 8)
    prev = jnp.where(n > 0, x_ref[pl.ds(p0, 8), :], 0.0)
    row8 = lax.broadcasted_iota(jnp.int32, (8, cc), 0)
    y = cur * cw[CONV_W - 1:CONV_W, :]
    for s in range(1, CONV_W):
        sh = pltpu.roll(cur, s, 0)
        top = jnp.where(row8 < s, pltpu.roll(prev, s, 0), sh[0:8, :])
        sh = jnp.concatenate([top, sh[8:, :]], axis=0)
        y = y + sh * cw[CONV_W - 1 - s:CONV_W - s, :]
    return _silu(y)


INV_BASE_SHIFT = 3
V_PER_QK = V_HEADS // QK_HEADS
HQ_PER_STEP = 2
DEC_GROUP = 8
DEC_SAMPLES = 2
PH1_CHUNKS = 2


def _split2(x):
    hi = x.astype(BF16)
    return hi, (x - hi.astype(F32)).astype(BF16)


def _split3(x):
    hi = x.astype(BF16)
    r = x - hi.astype(F32)
    mid = r.astype(BF16)
    return hi, mid, (r - mid.astype(F32)).astype(BF16)


def _dotb(a, b):
    return jnp.dot(a, b, preferred_element_type=F32)


N_LEVELS = CHUNK.bit_length() - 1 - INV_BASE_SHIFT


def _chunk_tables():
    row = lax.broadcasted_iota(jnp.int32, (CHUNK, CHUNK), 0)
    col = lax.broadcasted_iota(jnp.int32, (CHUNK, CHUNK), 1)
    same = [(row >> s) == (col >> s) for s in range(INV_BASE_SHIFT, CHUNK.bit_length())]
    tab32 = jnp.stack([row == col, same[0]]).astype(F32)
    levels = [jnp.logical_and(hi, jnp.logical_not(lo)) for lo, hi in zip(same[:-1], same[1:])]
    tab16 = jnp.stack([row >= col] + levels).astype(BF16)
    return tab32, tab16


def _inv_unit_lower(mats, tab32_ref, tab16_ref, out):
    n = range(len(mats))
    eye = tab32_ref[0]
    d = [mats[i] * tab32_ref[1] for i in n]
    db = [d[i].astype(BF16) for i in n]
    t = [eye - d[i] for i in n]
    a_sp = [_split2(mats[i]) for i in n]
    p = [_dotb(db[i], db[i]) for i in n]
    yield
    t = [t[i] + _dotb(t[i].astype(BF16), p[i].astype(BF16)) for i in n]
    p = [_dotb(p[i].astype(BF16), p[i].astype(BF16)) for i in n]
    yield
    t = [t[i] + _dotb(t[i].astype(BF16), p[i].astype(BF16)) for i in n]
    yield
    for lvl in range(N_LEVELS):
        tb = [t[i].astype(BF16) for i in n]
        x = [_dotb(tb[i], a_sp[i][0] * tab16_ref[1 + lvl]) for i in n]
        yield
        t = [t[i] - _dotb(x[i].astype(BF16), tb[i]) for i in n]
        yield
    t_sp = [_split2(t[i]) for i in n]
    at = [_dotb(a_sp[i][0], t_sp[i][0]) for i in n]
    at = [at[i] + _dotb(a_sp[i][0], t_sp[i][1]) for i in n]
    at = [at[i] + _dotb(a_sp[i][1], t_sp[i][0]) for i in n]
    yield
    resid = [(eye - t[i]) - at[i] for i in n]
    out.extend(t[i] + _dotb(t_sp[i][0], resid[i].astype(BF16)) for i in n)
    yield


def _trace_interleaved(*gens):
    live = list(gens)
    while live:
        for g in list(live):
            try:
                next(g)
            except StopIteration:
                live.remove(g)


def _gdn_prompt_body(q_ref, k_ref, v_ref, z_ref, ba_ref, cwq_ref, cwk_ref, cwv_ref, gp_ref, on_ref,
                     tab32_ref, tab16_ref, og_ref, s_ref, u_s, w_s, qd_s, at_s, kdt_s, gl_s, a_s, rhs_s):
    hq = pl.program_id(1)
    t_len = q_ref.shape[0]
    n_chunks = t_len // CHUNK
    row = lax.broadcasted_iota(jnp.int32, (CHUNK, CHUNK), 0)
    col = lax.broadcasted_iota(jnp.int32, (CHUNK, CHUNK), 1)
    causal = row >= col
    strict = row > col
    cwq = cwq_ref[...]
    cwk = cwk_ref[...]
    cwv = cwv_ref[...]
    neg_a = -jnp.exp(gp_ref[0:1, :])
    dt_bias = gp_ref[1:2, :]

    nv = HQ_PER_STEP * V_PER_QK
    n_iters = n_chunks // PH1_CHUNKS

    def chunk_prep(n, slot, base):
        r0 = pl.multiple_of(n * CHUNK, CHUNK)
        rows = pl.ds(r0, CHUNK)
        qc = _conv_silu_chunk(q_ref, cwq, n)
        yield
        kc = _conv_silu_chunk(k_ref, cwk, n)
        yield
        vc = _conv_silu_chunk(v_ref, cwv, n)
        yield
        ba = ba_ref[rows, :]
        beta_all = jax.nn.sigmoid(ba)
        xa = ba + dt_bias
        g_all = neg_a * (jnp.maximum(xa, 0.0) + jnp.log1p(jnp.exp(-jnp.abs(xa))))
        g_hi, g_mid, g_lo = _split3(g_all)
        tril = tab16_ref[0]
        gcum = _dotb(tril, g_hi) + (_dotb(tril, g_mid) + _dotb(tril, g_lo))
        gcum_t = gcum.T
        yield
        for q in range(HQ_PER_STEP):
            qh = qc[:, q * DK:(q + 1) * DK]
            kh = kc[:, q * DK:(q + 1) * DK]
            qn = qh * lax.rsqrt(jnp.sum(qh * qh, axis=-1, keepdims=True) + L2_EPS) * (DK ** -0.5)
            kn = kh * lax.rsqrt(jnp.sum(kh * kh, axis=-1, keepdims=True) + L2_EPS)
            kn_t = kn.T
            kn_tb = kn_t.astype(BF16)
            kk = _dotb(kn.astype(BF16), kn_tb)
            qk = _dotb(qn.astype(BF16), kn_tb)
            yield
            for j in range(V_PER_QK):
                h = q * V_PER_QK + j
                lane_b = (hq * HQ_PER_STEP + q) * V_PER_QK + j
                lane_a = V_HEADS + lane_b
                gcol = jnp.sum(jnp.where(col == lane_a, gcum, 0.0), axis=1, keepdims=True)
                bcol = jnp.sum(jnp.where(col == lane_b, beta_all, 0.0), axis=1, keepdims=True)
                grow = jnp.sum(jnp.where(row == lane_a, gcum_t, 0.0), axis=0, keepdims=True)
                glast = gcol[CHUNK - 1:CHUNK, :]
                decay = jnp.exp(jnp.where(causal, gcol - grow, -jnp.inf))
                a_mat = jnp.where(strict, kk * decay, 0.0) * bcol
                eg = jnp.exp(gcol)
                rhs = jnp.concatenate([vc[:, h * DV:(h + 1) * DV] * bcol, kn * (bcol * eg)], axis=1).astype(BF16)
                qd_s[h, rows, :] = (qn * eg).astype(BF16)
                at_s[h, rows, :] = (qk * decay).astype(BF16)
                kdt_s[h, rows, :] = (kn_t * jnp.exp(glast - grow)).astype(BF16)
                gl_s[h, pl.ds(n, 1), :] = jnp.broadcast_to(jnp.exp(glast), (1, DV))
                a_s[slot, base + h] = a_mat
                rhs_s[slot, base + h] = rhs
                yield

    def prep(it):
        it = jnp.asarray(it, jnp.int32)
        slot = it % 2
        for c in range(PH1_CHUNKS):
            yield from chunk_prep(PH1_CHUNKS * it + c, slot, c * nv)

    def solve(it):
        it = jnp.asarray(it, jnp.int32)
        slot = it % 2
        idx = range(PH1_CHUNKS * nv)
        t_mats = []
        yield from _inv_unit_lower([a_s[slot, i] for i in idx], tab32_ref, tab16_ref, t_mats)
        sols = [_dotb(t_mats[i].astype(BF16), rhs_s[slot, i]) for i in idx]
        yield
        for i in idx:
            rows = pl.ds(pl.multiple_of((PH1_CHUNKS * it + i // nv) * CHUNK, CHUNK), CHUNK)
            u_s[i % nv, rows, :] = sols[i][:, :DV]
            w_s[i % nv, rows, :] = sols[i][:, DV:].astype(BF16)
        yield

    gain = on_ref[...]
    heads = range(nv)

    def recur(it):
        it = jnp.asarray(it, jnp.int32)
        for c in range(PH1_CHUNKS):
            n = PH1_CHUNKS * it + c
            rows = pl.ds(pl.multiple_of(n * CHUNK, CHUNK), CHUNK)
            states = [s_ref[0, h] for h in heads]
            wq = [jnp.concatenate([w_s[h, rows, :], qd_s[h, rows, :]], axis=0) for h in heads]
            ws_qs = [_dotb(wq[h], states[h].astype(BF16)) for h in heads]
            yield
            v_new = [(u_s[h, rows, :] - ws_qs[h][:CHUNK]).astype(BF16) for h in heads]
            o = [ws_qs[h][CHUNK:] + _dotb(at_s[h, rows, :], v_new[h]) for h in heads]
            s_new = [states[h] * gl_s[h, pl.ds(n, 1), :] + _dotb(kdt_s[h, rows, :], v_new[h]) for h in heads]
            yield
            for h in heads:
                s_ref[0, h] = s_new[h]
                zg = z_ref[rows, h * DV:(h + 1) * DV]
                og = o[h] * lax.rsqrt(jnp.mean(o[h] * o[h], axis=-1, keepdims=True) + RMS_EPS) * gain * _silu(zg)
                og_ref[rows, h * DV:(h + 1) * DV] = og.astype(BF16)
            yield

    for h in heads:
        s_ref[0, h] = jnp.zeros((DK, DV), F32)
    _trace_interleaved(prep(0))
    _trace_interleaved(solve(0), prep(1))

    def steady(it, carry):
        _trace_interleaved(solve(it - 1), prep(it), recur(it - 2))
        return carry

    lax.fori_loop(2, n_iters, steady, 0)
    _trace_interleaved(solve(n_iters - 1), recur(n_iters - 2))
    _trace_interleaved(recur(n_iters - 1))


def _gdn_prompt(proj, ba, conv_w, gate_params, o_gain, *, batch, t_len):
    wq = HQ_PER_STEP * DK
    wv = HQ_PER_STEP * V_PER_QK * DV
    nv = HQ_PER_STEP * V_PER_QK
    in_specs = [
        pl.BlockSpec((t_len, wq), lambda b, h: (b, h)),
        pl.BlockSpec((t_len, wq), lambda b, h: (b, KEY_DIM // wq + h)),
        pl.BlockSpec((t_len, wv), lambda b, h: (b, (2 * KEY_DIM) // wv + h)),
        pl.BlockSpec((t_len, wv), lambda b, h: (b, CONV_DIM // wv + h)),
        pl.BlockSpec((t_len, 128), lambda b, h: (b, 0)),
        pl.BlockSpec((CONV_W, wq), lambda b, h: (0, h)),
        pl.BlockSpec((CONV_W, wq), lambda b, h: (0, KEY_DIM // wq + h)),
        pl.BlockSpec((CONV_W, wv), lambda b, h: (0, (2 * KEY_DIM) // wv + h)),
        pl.BlockSpec((2, 128), lambda b, h: (0, 0)),
        pl.BlockSpec((1, DV), lambda b, h: (0, 0)),
        pl.BlockSpec((2, CHUNK, CHUNK), lambda b, h: (0, 0, 0)),
        pl.BlockSpec((1 + N_LEVELS, CHUNK, CHUNK), lambda b, h: (0, 0, 0)),
    ]
    tab32, tab16 = _chunk_tables()
    out_shape = [
        jax.ShapeDtypeStruct((batch * t_len, VAL_DIM), BF16),
        jax.ShapeDtypeStruct((batch, V_HEADS, DK, DV), F32),
    ]
    out_specs = [
        pl.BlockSpec((t_len, wv), lambda b, h: (b, h)),
        pl.BlockSpec((1, nv, DK, DV), lambda b, h: (b, h, 0, 0)),
    ]
    scratch = [
        pltpu.VMEM((nv, t_len, DV), F32),
        pltpu.VMEM((nv, t_len, DK), BF16),
        pltpu.VMEM((nv, t_len, DK), BF16),
        pltpu.VMEM((nv, t_len, CHUNK), BF16),
        pltpu.VMEM((nv, t_len, CHUNK), BF16),
        pltpu.VMEM((nv, t_len // CHUNK, DV), F32),
        pltpu.VMEM((2, PH1_CHUNKS * nv, CHUNK, CHUNK), F32),
        pltpu.VMEM((2, PH1_CHUNKS * nv, CHUNK, DV + DK), BF16),
    ]
    return pl.pallas_call(
        _gdn_prompt_body,
        out_shape=out_shape,
        grid=(batch, QK_HEADS // HQ_PER_STEP),
        in_specs=in_specs,
        out_specs=out_specs,
        scratch_shapes=scratch,
        compiler_params=_params(("parallel", "arbitrary")),
        name="gdn_prompt",
    )(proj, proj, proj, proj, ba, conv_w, conv_w, conv_w, gate_params, o_gain.reshape(1, DV), tab32, tab16)


def _gdn_gates_body(ba_ref, gp_ref, beta_ref, eg_ref):
    ba = ba_ref[...]
    beta_ref[...] = jax.nn.sigmoid(ba)
    xa = ba + gp_ref[1:2, :]
    g = -jnp.exp(gp_ref[0:1, :]) * (jnp.maximum(xa, 0.0) + jnp.log1p(jnp.exp(-jnp.abs(xa))))
    eg_ref[...] = jnp.exp(g)


def _gdn_gates(ba, gate_params):
    m = ba.shape[0]
    return pl.pallas_call(
        _gdn_gates_body,
        out_shape=[jax.ShapeDtypeStruct((m, 128), F32)] * 2,
        name="gdn_gates",
    )(ba, gate_params)


def _gdn_decode_body(beta_ref, eg_ref, xs_ref, cw_ref, z_ref, on_ref, s_ref, so_ref, og_ref):
    for si in range(DEC_SAMPLES):
        b = pl.program_id(0) * DEC_SAMPLES + si
        y = xs_ref[si, 0] * cw_ref[0]
        for j in range(1, CONV_W):
            y = y + xs_ref[si, j] * cw_ref[j]
        y = _silu(y)
        q = y[0:QK_HEADS]
        k = y[QK_HEADS:2 * QK_HEADS]
        v = y[2 * QK_HEADS:]
        qn = q * lax.rsqrt(jnp.sum(q * q, axis=-1, keepdims=True) + L2_EPS) * (DK ** -0.5)
        kn = k * lax.rsqrt(jnp.sum(k * k, axis=-1, keepdims=True) + L2_EPS)
        qk_t = jnp.concatenate([qn, kn, jnp.zeros((DK - 2 * QK_HEADS, DK), F32)], axis=0).T
        for g0 in range(0, V_HEADS, DEC_GROUP):
            hs = range(g0, g0 + DEC_GROUP)
            qcol = {h: qk_t[:, h // V_PER_QK:h // V_PER_QK + 1] for h in hs}
            kcol = {h: qk_t[:, QK_HEADS + h // V_PER_QK:QK_HEADS + h // V_PER_QK + 1] for h in hs}
            sd = {h: s_ref[si, h] * eg_ref[b, h] for h in hs}
            ks = {h: jnp.sum(sd[h] * kcol[h], axis=0, keepdims=True) for h in hs}
            v_new = {h: beta_ref[b, h] * (v[h:h + 1, :] - ks[h]) for h in hs}
            sn = {h: sd[h] + kcol[h] * v_new[h] for h in hs}
            for h in hs:
                so_ref[si, h] = sn[h]
            o = {h: jnp.sum(sn[h] * qcol[h], axis=0, keepdims=True) for h in hs}
            for h in hs:
                og_ref[si, h:h + 1, :] = o[h]
        o_all = og_ref[si]
        og_ref[si] = (o_all * lax.rsqrt(jnp.mean(o_all * o_all, axis=-1, keepdims=True) + RMS_EPS)
                      * on_ref[...] * _silu(z_ref[si]))


def _gdn_decode(beta, eg, xs, conv_w3, z3, o_gain, state):
    nb = xs.shape[0]
    slots = CONV_DIM // DK
    smem = pl.BlockSpec(memory_space=pltpu.SMEM)
    return pl.pallas_call(
        _gdn_decode_body,
        out_shape=[
            jax.ShapeDtypeStruct((nb, V_HEADS, DK, DV), F32),
            jax.ShapeDtypeStruct((nb, V_HEADS, DV), F32),
        ],
        grid=(nb // DEC_SAMPLES,),
        in_specs=[
            smem,
            smem,
            pl.BlockSpec((DEC_SAMPLES, CONV_W, slots, DK), lambda b: (b, 0, 0, 0)),
            pl.BlockSpec((CONV_W, slots, DK), lambda b: (0, 0, 0)),
            pl.BlockSpec((DEC_SAMPLES, V_HEADS, DV), lambda b: (b, 0, 0)),
            pl.BlockSpec((1, DV), lambda b: (0, 0)),
            pl.BlockSpec((DEC_SAMPLES, V_HEADS, DK, DV), lambda b: (b, 0, 0, 0)),
        ],
        out_specs=[
            pl.BlockSpec((DEC_SAMPLES, V_HEADS, DK, DV), lambda b: (b, 0, 0, 0)),
            pl.BlockSpec((DEC_SAMPLES, V_HEADS, DV), lambda b: (b, 0, 0)),
        ],
        compiler_params=_params(("parallel",)),
        name="gdn_decode",
    )(beta, eg, xs, conv_w3, z3, o_gain.reshape(1, DV), state)


def _ssm_pow_body(lr_ref, li_ref, ldt_ref, pr_ref, pi_ref):
    n = jnp.where(lax.broadcasted_iota(jnp.int32, pr_ref.shape, 0) == 0, 1.0, float(SCAN_SEG))
    dt = jnp.exp(ldt_ref[...])
    lr = jnp.minimum(lr_ref[...], -1e-4) * dt
    li = li_ref[...] * dt
    mag = jnp.exp(n * lr)
    pr_ref[...] = mag * jnp.cos(n * li)
    pi_ref[...] = mag * jnp.sin(n * li)


def _ssm_pow(lr, li, ldt):
    gp = lr.shape[1]
    tl = 2048
    spec = pl.BlockSpec((1, tl), lambda i: (0, i))
    ospec = pl.BlockSpec((POW_ROWS, tl), lambda i: (0, i))
    return pl.pallas_call(
        _ssm_pow_body,
        out_shape=[jax.ShapeDtypeStruct((POW_ROWS, gp), F32)] * 2,
        grid=(gp // tl,),
        in_specs=[spec, spec, spec],
        out_specs=[ospec, ospec],
        name="ssm_pow",
    )(lr, li, ldt)


def _ssm_blocks_body(lr_ref, li_ref, ldt_ref, br_ref, bi_ref, cr_ref, ci_ref, bblk_ref, cblk_ref):
    dt = jnp.exp(ldt_ref[...])
    lr = jnp.minimum(lr_ref[...], -1e-4)
    li = li_ref[...]
    mag = jnp.exp(lr * dt)
    xr = mag * jnp.cos(li * dt) - 1.0
    xi = mag * jnp.sin(li * dt)
    den = 1.0 / (lr * lr + li * li)
    fr = (xr * lr + xi * li) * den
    fi = (xi * lr - xr * li) * den
    bblk_ref[...] = jnp.zeros_like(bblk_ref)
    cblk_ref[...] = jnp.zeros_like(cblk_ref)
    per_tile = 128 // SSM_STATE
    for g in range(GROUPS_PER_STEP):
        rows = slice(g * SSM_GROUP, (g + 1) * SSM_GROUP)
        off = (g // per_tile) * 256 + (g % per_tile) * SSM_STATE
        cols = slice(off, off + SSM_STATE)
        cols_im = slice(off + 128, off + 128 + SSM_STATE)
        br = br_ref[g]
        bi = bi_ref[g]
        frg = fr[g:g + 1, :]
        fig = fi[g:g + 1, :]
        bblk_ref[rows, cols] = (frg * br - fig * bi).astype(BF16)
        bblk_ref[rows, cols_im] = (frg * bi + fig * br).astype(BF16)
        cblk_ref[rows, cols] = cr_ref[g].astype(BF16)
        cblk_ref[rows, cols_im] = (-ci_ref[g]).astype(BF16)


def _ssm_blocks(lr, li, ldt, bt_re, bt_im, c_re, c_im):
    g, p = lr.shape
    k = GROUPS_PER_STEP
    s2 = pl.BlockSpec((k, p), lambda i: (i, 0))
    s1 = pl.BlockSpec((k, 1), lambda i: (i, 0))
    s3 = pl.BlockSpec((k, SSM_GROUP, p), lambda i: (i, 0, 0))
    blk = pl.BlockSpec((None, CH_PER_STEP, 2 * ST_PER_STEP), lambda i: (i, 0, 0))
    return pl.pallas_call(
        _ssm_blocks_body,
        out_shape=[jax.ShapeDtypeStruct((g // k, CH_PER_STEP, 2 * ST_PER_STEP), BF16)] * 2,
        grid=(g // k,),
        in_specs=[s2, s2, s1, s3, s3, s3, s3],
        out_specs=[blk, blk],
        name="ssm_blocks",
    )(lr, li, ldt, bt_re, bt_im, c_re, c_im)


def _dot_nt(a, b):
    return lax.dot_general(a, b, (((1,), (1,)), ((), ())), preferred_element_type=F32)


def _gelu_tanh(x):
    return 0.5 * x * (1.0 + jnp.tanh(math.sqrt(2.0 / math.pi) * (x + 0.044715 * (x * x * x))))


def _ssm_scan_body(ua_ref, ub_ref, b_ref, c_ref, pr_ref, pi_ref, d_ref, y_ref, hr_ref, hi_ref,
                   h_s, y_s, cr_s, ci_s):
    tb = pl.program_id(2)
    nt = ST_PER_STEP // 128
    u_perm = jnp.concatenate(
        [jnp.concatenate([ua_ref[pl.ds(i, 8, stride=SCAN_SEG), :], ub_ref[pl.ds(i, 8, stride=SCAN_SEG), :]], axis=1)
         for i in range(SCAN_SEG)], axis=0).astype(BF16)

    @pl.when(tb == 0)
    def _():
        cr_s[...] = jnp.zeros_like(cr_s)
        ci_s[...] = jnp.zeros_like(ci_s)

    def b_proj(g):
        for c in range(g * SCAN_TILES, (g + 1) * SCAN_TILES):
            bu = jnp.dot(u_perm, b_ref[:, c * 256:(c + 1) * 256], preferred_element_type=F32)
            h_s[c] = bu[:, :128]
            h_s[nt + c] = bu[:, 128:]

    row8 = lax.broadcasted_iota(jnp.int32, (8, 128), 0)

    def tile_scan(c):
        lanes = slice(c * 128, (c + 1) * 128)
        ar = jnp.broadcast_to(pr_ref[0:1, lanes], (8, 128))
        ai = jnp.broadcast_to(pi_ref[0:1, lanes], (8, 128))
        hr = jnp.zeros((8, 128), F32)
        hi = jnp.zeros((8, 128), F32)
        for i in range(SCAN_SEG):
            xr = h_s[c, i * 8:(i + 1) * 8, :]
            xi = h_s[nt + c, i * 8:(i + 1) * 8, :]
            hr, hi = ar * hr - ai * hi + xr, ar * hi + ai * hr + xi
            if i % SCAN_YIELD == SCAN_YIELD - 1:
                yield

        asr = pr_ref[1:2, lanes]
        asi = pi_ref[1:2, lanes]
        c_r = cr_s[:, lanes]
        c_i = ci_s[:, lanes]
        cin_r = jnp.zeros((8, 128), F32)
        cin_i = jnp.zeros((8, 128), F32)
        for s in range(8):
            cin_r = jnp.where(row8 == s, c_r, cin_r)
            cin_i = jnp.where(row8 == s, c_i, cin_i)
            e_r = hr[s:s + 1, :]
            e_i = hi[s:s + 1, :]
            c_r, c_i = e_r + asr * c_r - asi * c_i, e_i + asr * c_i + asi * c_r
        cr_s[:, lanes] = c_r
        ci_s[:, lanes] = c_i
        yield

        hr, hi = cin_r, cin_i
        for i in range(SCAN_SEG):
            xr = h_s[c, i * 8:(i + 1) * 8, :]
            xi = h_s[nt + c, i * 8:(i + 1) * 8, :]
            hr, hi = ar * hr - ai * hi + xr, ar * hi + ai * hr + xi
            h_s[c, i * 8:(i + 1) * 8, :] = hr
            h_s[nt + c, i * 8:(i + 1) * 8, :] = hi
            if i % SCAN_YIELD == SCAN_YIELD - 1:
                yield

    n_groups = nt // SCAN_TILES
    b_proj(0)
    y = None
    for g in range(n_groups):
        if g + 1 < n_groups:
            b_proj(g + 1)
        tiles = range(g * SCAN_TILES, (g + 1) * SCAN_TILES)
        _trace_interleaved(*[tile_scan(c) for c in tiles])
        for c in tiles:
            h_cat = jnp.concatenate([h_s[c], h_s[nt + c]], axis=1).astype(BF16)
            y_c = _dot_nt(h_cat, c_ref[:, c * 256:(c + 1) * 256])
            y = y_c if y is None else y + y_c

    y_s[0] = y[:, :128]
    y_s[1] = y[:, 128:]
    per_seg = SCAN_SEG // 8
    y_nat = jnp.concatenate(
        [jnp.concatenate([y_s[h, pl.ds((j % per_seg) * 64 + j // per_seg, 8, stride=8), :] for h in range(2)], axis=1)
         for j in range(SCAN_BLOCK // 8)], axis=0)
    u = jnp.concatenate([ua_ref[...], ub_ref[...]], axis=1)
    y_ref[...] = _gelu_tanh(y_nat + d_ref[...] * u)

    @pl.when(tb == pl.num_programs(2) - 1)
    def _():
        hr_ref[...] = cr_s[...]
        hi_ref[...] = ci_s[...]


def _ssm_scan(uz, bblk, cblk, pw_r, pw_i, d_skip, *, batch, t_len):
    nt = t_len // SCAN_BLOCK
    return pl.pallas_call(
        _ssm_scan_body,
        out_shape=[
            jax.ShapeDtypeStruct((batch * t_len, SSM_WIDTH), F32),
            jax.ShapeDtypeStruct((batch, 1, SSM_GROUPS * SSM_STATE), F32),
            jax.ShapeDtypeStruct((batch, 1, SSM_GROUPS * SSM_STATE), F32),
        ],
        grid=(batch, N_GSTEPS, nt),
        in_specs=[
            pl.BlockSpec((SCAN_BLOCK, 128), lambda b, g, t: (b * nt + t, 2 * g)),
            pl.BlockSpec((SCAN_BLOCK, 128), lambda b, g, t: (b * nt + t, 2 * g + 1)),
            pl.BlockSpec((None, CH_PER_STEP, 2 * ST_PER_STEP), lambda b, g, t: (g, 0, 0)),
            pl.BlockSpec((None, CH_PER_STEP, 2 * ST_PER_STEP), lambda b, g, t: (g, 0, 0)),
            pl.BlockSpec((POW_ROWS, ST_PER_STEP), lambda b, g, t: (0, g)),
            pl.BlockSpec((POW_ROWS, ST_PER_STEP), lambda b, g, t: (0, g)),
            pl.BlockSpec((1, CH_PER_STEP), lambda b, g, t: (0, g)),
        ],
        out_specs=[
            pl.BlockSpec((SCAN_BLOCK, CH_PER_STEP), lambda b, g, t: (b * nt + t, g)),
            pl.BlockSpec((None, 1, ST_PER_STEP), lambda b, g, t: (b, 0, g)),
            pl.BlockSpec((None, 1, ST_PER_STEP), lambda b, g, t: (b, 0, g)),
        ],
        scratch_shapes=[
            pltpu.VMEM((2 * ST_PER_STEP // 128, SCAN_BLOCK, 128), F32),
            pltpu.VMEM((CH_PER_STEP // 128, SCAN_BLOCK, 128), F32),
            pltpu.VMEM((1, ST_PER_STEP), F32),
            pltpu.VMEM((1, ST_PER_STEP), F32),
        ],
        compiler_params=_params(("parallel", "parallel", "arbitrary")),
        name="ssm_scan",
    )(uz, uz, bblk, cblk, pw_r, pw_i, d_skip.reshape(1, SSM_WIDTH))


def _ssm_decode_body(u_ref, b_ref, c_ref, pr_ref, pi_ref, d_ref, h0r_ref, h0i_ref, y_ref, hr_ref, hi_ref):
    nt = ST_PER_STEP // 128
    u = u_ref[...]
    bu = jnp.dot(u.astype(BF16), b_ref[...], preferred_element_type=F32)
    h_cat = []
    for c in range(nt):
        lanes = slice(c * 128, (c + 1) * 128)
        ar = pr_ref[0:1, lanes]
        ai = pi_ref[0:1, lanes]
        h0r = h0r_ref[:, lanes]
        h0i = h0i_ref[:, lanes]
        hr = bu[:, c * 256:c * 256 + 128] + ar * h0r - ai * h0i
        hi = bu[:, c * 256 + 128:(c + 1) * 256] + ar * h0i + ai * h0r
        hr_ref[:, lanes] = hr
        hi_ref[:, lanes] = hi
        h_cat += [hr, hi]
    y = _dot_nt(jnp.concatenate(h_cat, axis=1).astype(BF16), c_ref[...])
    y_ref[...] = _gelu_tanh(y + d_ref[...] * u)


def _ssm_decode(uz, bblk, cblk, pw_r, pw_i, d_skip, h0r, h0i):
    nb = uz.shape[0]
    return pl.pallas_call(
        _ssm_decode_body,
        out_shape=[
            jax.ShapeDtypeStruct((nb, SSM_WIDTH), F32),
            jax.ShapeDtypeStruct((nb, SSM_GROUPS * SSM_STATE), F32),
            jax.ShapeDtypeStruct((nb, SSM_GROUPS * SSM_STATE), F32),
        ],
        grid=(N_GSTEPS,),
        in_specs=[
            pl.BlockSpec((nb, CH_PER_STEP), lambda g: (0, g)),
            pl.BlockSpec((None, CH_PER_STEP, 2 * ST_PER_STEP), lambda g: (g, 0, 0)),
            pl.BlockSpec((None, CH_PER_STEP, 2 * ST_PER_STEP), lambda g: (g, 0, 0)),
            pl.BlockSpec((POW_ROWS, ST_PER_STEP), lambda g: (0, g)),
            pl.BlockSpec((POW_ROWS, ST_PER_STEP), lambda g: (0, g)),
            pl.BlockSpec((1, CH_PER_STEP), lambda g: (0, g)),
            pl.BlockSpec((nb, ST_PER_STEP), lambda g: (0, g)),
            pl.BlockSpec((nb, ST_PER_STEP), lambda g: (0, g)),
        ],
        out_specs=[
            pl.BlockSpec((nb, CH_PER_STEP), lambda g: (0, g)),
            pl.BlockSpec((nb, ST_PER_STEP), lambda g: (0, g)),
            pl.BlockSpec((nb, ST_PER_STEP), lambda g: (0, g)),
        ],
        compiler_params=_params(("parallel",)),
        name="ssm_decode",
    )(uz, bblk, cblk, pw_r, pw_i, d_skip.reshape(1, SSM_WIDTH), h0r, h0i)


GLU_SLAB = 256


def _glu_body(y_ref, w_ref, b_ref, z_ref, o_ref, yb_scr, *, tn):
    j = pl.program_id(1)

    @pl.when(j == 0)
    def _():
        yb_scr[...] = y_ref[...].astype(BF16)

    for c in range(tn // GLU_SLAB):
        cols = slice(c * GLU_SLAB, (c + 1) * GLU_SLAB)
        t = jnp.dot(yb_scr[...], w_ref[:, cols], preferred_element_type=F32) + b_ref[:, cols]
        y = y_ref[:, pl.ds(pl.multiple_of(j * tn + c * GLU_SLAB, GLU_SLAB), GLU_SLAB)]
        o_ref[:, cols] = (y * jax.nn.sigmoid(t) * _silu(z_ref[:, cols])).astype(BF16)


def _glu(y, w, b, uz, *, tm, tn):
    m, d = y.shape
    zoff = SSM_WIDTH // tn
    return pl.pallas_call(
        functools.partial(_glu_body, tn=tn),
        out_shape=jax.ShapeDtypeStruct((m, d), BF16),
        grid=(m // tm, d // tn),
        in_specs=[
            pl.BlockSpec((tm, d), lambda i, j: (i, 0)),
            pl.BlockSpec((d, tn), lambda i, j: (0, j)),
            pl.BlockSpec((1, tn), lambda i, j: (0, j)),
            pl.BlockSpec((tm, tn), lambda i, j: (i, zoff + j)),
        ],
        out_specs=pl.BlockSpec((tm, tn), lambda i, j: (i, j)),
        scratch_shapes=[pltpu.VMEM((tm, d), BF16)],
        compiler_params=_params(("parallel", "arbitrary")),
        name="ssm_glu",
    )(y, w, b.reshape(1, d), uz)


def _prep_weights(norm_gdn, w_in_gdn, conv_gdn, a_log_gdn, dt_bias_gdn, onorm_gdn, w_out_gdn,
                  norm_ssm, w_in_ssm, lam_re, lam_im, b_re, b_im, c_re, c_im, d_ssm, log_dt_ssm,
                  w_glu_ssm, b_glu_ssm, w_out_ssm, norm_final):
    w_in_t = jnp.swapaxes(w_in_gdn[0], 0, 1)
    w_ba = jnp.pad(w_in_t[MAIN_DIM:], ((0, 128 - 2 * V_HEADS), (0, 0))).astype(BF16)
    zeros = jnp.zeros((V_HEADS,), F32)
    pad = jnp.zeros((128 - 2 * V_HEADS,), F32)
    gate_params = jnp.stack([jnp.concatenate([zeros, a_log_gdn[0], pad]),
                             jnp.concatenate([zeros, dt_bias_gdn[0], pad])])

    lr, li, ldt = lam_re[0], lam_im[0], log_dt_ssm[0]
    gp = SSM_GROUPS * SSM_STATE
    pw_r, pw_i = _ssm_pow(lr.reshape(1, gp), li.reshape(1, gp),
                          jnp.repeat(ldt, SSM_STATE).reshape(1, gp))
    bblk, cblk = _ssm_blocks(lr, li, ldt.reshape(SSM_GROUPS, 1),
                                 jnp.swapaxes(b_re[0], 1, 2), jnp.swapaxes(b_im[0], 1, 2), c_re[0], c_im[0])
    return dict(
        norm_gdn=norm_gdn[0], w_main=w_in_t, w_ba=w_ba, conv_w=conv_gdn[0], gate_params=gate_params,
        o_gain=onorm_gdn[0], w_out_gdn=w_out_gdn[0].astype(BF16),
        norm_ssm=norm_ssm[0], w_in_ssm=w_in_ssm[0], pw_r=pw_r, pw_i=pw_i, bblk=bblk, cblk=cblk,
        d_skip=d_ssm[0], w_glu=w_glu_ssm[0].astype(BF16), b_glu=b_glu_ssm[0], w_out_ssm=w_out_ssm[0].astype(BF16),
        norm_final=norm_final)


def _ssm_tail(p, x1, y, uz, *, tm, tm_glu):
    y3 = _glu(y, p["w_glu"], p["b_glu"], uz, tm=tm_glu, tn=1024)
    return _outproj(y3, p["w_out_ssm"], x1, p["norm_final"], tm=tm, name="ssm_out")


def _prompt_path(p, x_prompt):
    batch, t_len, d = x_prompt.shape
    x = x_prompt.reshape(batch * t_len, d)
    proj, ba = _norm_matmul(x, p["norm_gdn"], p["w_main"], p["w_ba"], n=MAIN_DIM, tm=2048, tn=512, name="gdn_in",
                            w_transposed=True)
    og, delta = _gdn_prompt(proj, ba, p["conv_w"], p["gate_params"], p["o_gain"], batch=batch, t_len=t_len)
    conv_state = proj.reshape(batch, t_len, MAIN_DIM)[:, t_len - (CONV_W - 1):, :CONV_DIM]
    x1 = _outproj(og, p["w_out_gdn"], x, tm=512, name="gdn_out")
    uz = _norm_matmul(x1, p["norm_ssm"], p["w_in_ssm"], n=2 * SSM_WIDTH, tm=2048, tn=512, name="ssm_in")
    y, h_re, h_im = _ssm_scan(uz, p["bblk"], p["cblk"], p["pw_r"], p["pw_i"], p["d_skip"],
                              batch=batch, t_len=t_len)
    out = _ssm_tail(p, x1, y, uz, tm=512, tm_glu=512)
    return (out.reshape(batch, t_len, d), conv_state[None], delta[None],
            h_re.reshape(1, batch, SSM_GROUPS, SSM_STATE), h_im.reshape(1, batch, SSM_GROUPS, SSM_STATE))


def _sample_path(p, x_sample, state_conv, state_delta, state_re, state_im):
    nb, _, d = x_sample.shape
    x = x_sample.reshape(nb, d)
    proj, ba = _norm_matmul(x, p["norm_gdn"], p["w_main"], p["w_ba"], n=MAIN_DIM, tm=nb, tn=512, name="gdn_in_s",
                            w_transposed=True)
    beta, eg = _gdn_gates(ba, p["gate_params"])
    xs = jnp.concatenate([state_conv[0], proj[:, None, :CONV_DIM]], axis=1)
    slots = CONV_DIM // DK
    delta, og = _gdn_decode(beta[:, :V_HEADS], eg[:, V_HEADS:2 * V_HEADS],
                            xs.reshape(nb, CONV_W, slots, DK), p["conv_w"].reshape(CONV_W, slots, DK),
                            proj[:, CONV_DIM:].reshape(nb, V_HEADS, DV), p["o_gain"], state_delta[0])
    x1 = _outproj(og.reshape(nb, VAL_DIM).astype(BF16), p["w_out_gdn"], x, tm=nb, name="gdn_out_s")
    uz = _norm_matmul(x1, p["norm_ssm"], p["w_in_ssm"], n=2 * SSM_WIDTH, tm=nb, tn=512, name="ssm_in_s")
    gp = SSM_GROUPS * SSM_STATE
    y, h_re, h_im = _ssm_decode(uz, p["bblk"], p["cblk"], p["pw_r"], p["pw_i"], p["d_skip"],
                                state_re[0].reshape(nb, gp), state_im[0].reshape(nb, gp))
    out = _ssm_tail(p, x1, y, uz, tm=nb, tm_glu=nb)
    return (out.reshape(nb, 1, d), xs[:, 1:][None], delta[None],
            h_re.reshape(1, nb, SSM_GROUPS, SSM_STATE), h_im.reshape(1, nb, SSM_GROUPS, SSM_STATE))


def kernel(x_prompt, x_sample, state_gdn_conv, state_gdn_delta, state_ssm_re, state_ssm_im, norm_gdn, w_in_gdn, conv_gdn, a_log_gdn, dt_bias_gdn, onorm_gdn, w_out_gdn, norm_ssm, w_in_ssm, lam_re, lam_im, b_re, b_im, c_re, c_im, d_ssm, log_dt_ssm, w_glu_ssm, b_glu_ssm, w_out_ssm, norm_final):
    p = _prep_weights(norm_gdn, w_in_gdn, conv_gdn, a_log_gdn, dt_bias_gdn, onorm_gdn, w_out_gdn,
                      norm_ssm, w_in_ssm, lam_re, lam_im, b_re, b_im, c_re, c_im, d_ssm, log_dt_ssm,
                      w_glu_ssm, b_glu_ssm, w_out_ssm, norm_final)
    y_p, conv_p, delta_p, re_p, im_p = _prompt_path(p, x_prompt)
    y_s, conv_s, delta_s, re_s, im_s = _sample_path(p, x_sample, state_gdn_conv, state_gdn_delta,
                                                    state_ssm_re, state_ssm_im)
    return (y_p, y_s, conv_p, delta_p, re_p, im_p, conv_s, delta_s, re_s, im_s)
```

```python
import functools
import math

import jax
import jax.numpy as jnp
from jax import lax
from jax.experimental import pallas as pl
from jax.experimental.pallas import tpu as pltpu

F32 = jnp.float32
BF16 = jnp.bfloat16

RMS_EPS = 1e-6
L2_EPS = 1e-6

D_MODEL = 2048
QK_HEADS = 16
V_HEADS = 32
DK = 128
DV = 128
KEY_DIM = QK_HEADS * DK
VAL_DIM = V_HEADS * DV
CONV_DIM = 2 * KEY_DIM + VAL_DIM
CONV_W = 4
MAIN_DIM = CONV_DIM + VAL_DIM

SSM_WIDTH = 4096
SSM_GROUP = 16
SSM_GROUPS = SSM_WIDTH // SSM_GROUP
SSM_STATE = 64
GROUPS_PER_STEP = 16
CH_PER_STEP = GROUPS_PER_STEP * SSM_GROUP
ST_PER_STEP = GROUPS_PER_STEP * SSM_STATE
N_GSTEPS = SSM_GROUPS // GROUPS_PER_STEP

CHUNK = 128
SCAN_BLOCK = 1024
SCAN_SEG = SCAN_BLOCK // 8
POW_ROWS = 8
SCAN_TILES = 1
SCAN_YIELD = 4

VMEM_LIMIT = 56 * 1024 * 1024


def _params(sem):
    return pltpu.CompilerParams(dimension_semantics=sem, vmem_limit_bytes=VMEM_LIMIT)


def _mm(a, b):
    return jnp.dot(a.astype(BF16), b.astype(BF16), preferred_element_type=F32)


def _mm_hi(a, b):
    return jnp.dot(a, b, precision=lax.Precision.HIGHEST, preferred_element_type=F32)


def _silu(x):
    return x * jax.nn.sigmoid(x)


def _rms(x, g):
    ms = jnp.mean(x * x, axis=-1, keepdims=True)
    return x * lax.rsqrt(ms + RMS_EPS) * g


def _norm_matmul_body(x_ref, g_ref, w_ref, *rest, has_aux, w_transposed):
    if has_aux:
        waux_ref, o_ref, aux_ref, h_scr = rest
    else:
        o_ref, h_scr = rest
    mm = _dot_nt if w_transposed else _dotb

    @pl.when(pl.program_id(1) == 0)
    def _():
        h = _rms(x_ref[...], g_ref[...]).astype(BF16)
        h_scr[...] = h
        if has_aux:
            aux_ref[...] = mm(h, waux_ref[...])

    o_ref[...] = mm(h_scr[...], w_ref[...].astype(BF16))


def _norm_matmul(x, g, w, w_aux=None, *, n, tm, tn, name, w_transposed=False):
    m, d = x.shape
    has_aux = w_aux is not None
    in_specs = [
        pl.BlockSpec((tm, d), lambda i, j: (i, 0), pipeline_mode=pl.Buffered(1)),
        pl.BlockSpec((1, d), lambda i, j: (0, 0)),
        pl.BlockSpec((tn, d), lambda i, j: (j, 0)) if w_transposed else pl.BlockSpec((d, tn), lambda i, j: (0, j)),
    ]
    out_shape = [jax.ShapeDtypeStruct((m, n), F32)]
    out_specs = [pl.BlockSpec((tm, tn), lambda i, j: (i, j))]
    args = [x, g.reshape(1, d), w]
    if has_aux:
        na = w_aux.shape[0] if w_transposed else w_aux.shape[1]
        in_specs.append(pl.BlockSpec(w_aux.shape, lambda i, j: (0, 0)))
        out_shape.append(jax.ShapeDtypeStruct((m, na), F32))
        out_specs.append(pl.BlockSpec((tm, na), lambda i, j: (i, 0)))
        args.append(w_aux)
    res = pl.pallas_call(
        functools.partial(_norm_matmul_body, has_aux=has_aux, w_transposed=w_transposed),
        out_shape=out_shape,
        grid=(m // tm, n // tn),
        in_specs=in_specs,
        out_specs=out_specs,
        scratch_shapes=[pltpu.VMEM((tm, d), BF16)],
        compiler_params=_params(("parallel", "arbitrary")),
        name=name,
    )(*args)
    return res if has_aux else res[0]


def _outproj_body(a_ref, w_ref, x_ref, *rest, final_norm):
    if final_norm:
        g_ref, o_ref = rest
    else:
        (o_ref,) = rest
    y = x_ref[...] + jnp.dot(a_ref[...], w_ref[...], preferred_element_type=F32)
    o_ref[...] = _rms(y, g_ref[...]) if final_norm else y


def _outproj(a, w, x, g=None, *, tm, name):
    m, kd = a.shape
    n = w.shape[1]
    final_norm = g is not None
    in_specs = [
        pl.BlockSpec((tm, kd), lambda i: (i, 0)),
        pl.BlockSpec((kd, n), lambda i: (0, 0), pipeline_mode=pl.Buffered(1)),
        pl.BlockSpec((tm, n), lambda i: (i, 0)),
    ]
    args = [a, w, x]
    if final_norm:
        in_specs.append(pl.BlockSpec((1, n), lambda i: (0, 0)))
        args.append(g.reshape(1, n))
    return pl.pallas_call(
        functools.partial(_outproj_body, final_norm=final_norm),
        out_shape=jax.ShapeDtypeStruct((m, n), F32),
        grid=(m // tm,),
        in_specs=in_specs,
        out_specs=pl.BlockSpec((tm, n), lambda i: (i, 0)),
        compiler_params=_params(("parallel",)),
        name=name,
    )(*args)


def _conv_silu_chunk(x_ref, cw, n):
    cc = x_ref.shape[1]
    r0 = pl.multiple_of(n * CHUNK, CHUNK)
    cur = x_ref[pl.ds(r0, CHUNK), :]
    p0 = pl.multiple_of(jnp.maximum(r0 - 8, 0), 8)
    prev = jnp.where(n > 0, x_ref[pl.ds(p0, 8), :], 0.0)
    row8 = lax.broadcasted_iota(jnp.int32, (8, cc), 0)
    y = cur * cw[CONV_W - 1:CONV_W, :]
    for s in range(1, CONV_W):
        sh = pltpu.roll(cur, s, 0)
        top = jnp.where(row8 < s, pltpu.roll(prev, s, 0), sh[0:8, :])
        sh = jnp.concatenate([top, sh[8:, :]], axis=0)
        y = y + sh * cw[CONV_W - 1 - s:CONV_W - s, :]
    return _silu(y)


INV_BASE_SHIFT = 3
V_PER_QK = V_HEADS // QK_HEADS
HQ_PER_STEP = 2
DEC_GROUP = 8
DEC_SAMPLES = 2
PH1_CHUNKS = 2


def _split2(x):
    hi = x.astype(BF16)
    return hi, (x - hi.astype(F32)).astype(BF16)


def _split3(x):
    hi = x.astype(BF16)
    r = x - hi.astype(F32)
    mid = r.astype(BF16)
    return hi, mid, (r - mid.astype(F32)).astype(BF16)


def _dotb(a, b):
    return jnp.dot(a, b, preferred_element_type=F32)


N_LEVELS = CHUNK.bit_length() - 1 - INV_BASE_SHIFT


def _chunk_tables():
    row = lax.broadcasted_iota(jnp.int32, (CHUNK, CHUNK), 0)
    col = lax.broadcasted_iota(jnp.int32, (CHUNK, CHUNK), 1)
    same = [(row >> s) == (col >> s) for s in range(INV_BASE_SHIFT, CHUNK.bit_length())]
    tab32 = jnp.stack([row == col, same[0]]).astype(F32)
    levels = [jnp.logical_and(hi, jnp.logical_not(lo)) for lo, hi in zip(same[:-1], same[1:])]
    tab16 = jnp.stack([row >= col] + levels).astype(BF16)
    return tab32, tab16


def _inv_unit_lower(mats, tab32_ref, tab16_ref, out):
    n = range(len(mats))
    eye = tab32_ref[0]
    d = [mats[i] * tab32_ref[1] for i in n]
    db = [d[i].astype(BF16) for i in n]
    t = [eye - d[i] for i in n]
    ab = [mats[i].astype(BF16) for i in n]
    p = [_dotb(db[i], db[i]) for i in n]
    yield
    t = [t[i] + _dotb(t[i].astype(BF16), p[i].astype(BF16)) for i in n]
    p = [_dotb(p[i].astype(BF16), p[i].astype(BF16)) for i in n]
    yield
    t = [t[i] + _dotb(t[i].astype(BF16), p[i].astype(BF16)) for i in n]
    yield
    for lvl in range(N_LEVELS):
        tb = [t[i].astype(BF16) for i in n]
        x = [_dotb(tb[i], ab[i] * tab16_ref[1 + lvl]) for i in n]
        yield
        t = [t[i] - _dotb(x[i].astype(BF16), tb[i]) for i in n]
        yield
    out.extend(t)


def _trace_interleaved(*gens):
    live = list(gens)
    while live:
        for g in list(live):
            try:
                next(g)
            except StopIteration:
                live.remove(g)


def _chunk_gates_body(ba_ref, gp_ref, tril_ref, beta_ref, gcum_ref, gcumt_ref):
    ba = ba_ref[...]
    beta_ref[...] = jax.nn.sigmoid(ba)
    xa = ba + gp_ref[1:2, :]
    g_all = -jnp.exp(gp_ref[0:1, :]) * (jnp.maximum(xa, 0.0) + jnp.log1p(jnp.exp(-jnp.abs(xa))))
    g_hi, g_mid, g_lo = _split3(g_all)
    tril = tril_ref[...]
    gcum = _dotb(tril, g_hi) + (_dotb(tril, g_mid) + _dotb(tril, g_lo))
    gcum_ref[...] = gcum
    gcumt_ref[...] = gcum.T


def _chunk_gates(ba, gate_params, tril):
    m = ba.shape[0]
    blk = pl.BlockSpec((CHUNK, 128), lambda i: (i, 0))
    return pl.pallas_call(
        _chunk_gates_body,
        out_shape=[jax.ShapeDtypeStruct((m, 128), F32)] * 3,
        grid=(m // CHUNK,),
        in_specs=[blk, pl.BlockSpec((2, 128), lambda i: (0, 0)), pl.BlockSpec((CHUNK, CHUNK), lambda i: (0, 0))],
        out_specs=[blk, blk, blk],
        compiler_params=_params(("parallel",)),
        name="gdn_chunk_gates",
    )(ba, gate_params, tril)


def _gdn_prompt_body(q_ref, k_ref, v_ref, z_ref, beta_ref, gcum_ref, gcumt_ref, cwq_ref, cwk_ref, cwv_ref, on_ref,
                     tab32_ref, tab16_ref, og_ref, s_ref, u_s, w_s, qd_s, at_s, kdt_s, gl_s, a_s, rhs_s):
    hq = pl.program_id(1)
    t_len = q_ref.shape[0]
    n_chunks = t_len // CHUNK
    row = lax.broadcasted_iota(jnp.int32, (CHUNK, CHUNK), 0)
    col = lax.broadcasted_iota(jnp.int32, (CHUNK, CHUNK), 1)
    causal = row >= col
    strict = row > col
    cwq = cwq_ref[...]
    cwk = cwk_ref[...]
    cwv = cwv_ref[...]

    nv = HQ_PER_STEP * V_PER_QK
    n_iters = n_chunks // PH1_CHUNKS

    def chunk_prep(n, slot, base):
        r0 = pl.multiple_of(n * CHUNK, CHUNK)
        rows = pl.ds(r0, CHUNK)
        qc = _conv_silu_chunk(q_ref, cwq, n)
        yield
        kc = _conv_silu_chunk(k_ref, cwk, n)
        yield
        vc = _conv_silu_chunk(v_ref, cwv, n)
        yield
        beta_all = beta_ref[rows, :]
        gcum = gcum_ref[rows, :]
        gcum_t = gcumt_ref[rows, :]
        yield
        for q in range(HQ_PER_STEP):
            qh = qc[:, q * DK:(q + 1) * DK]
            kh = kc[:, q * DK:(q + 1) * DK]
            ones = jnp.ones((DK, DK), BF16)
            qn = qh * (lax.rsqrt(_dotb((qh * qh).astype(BF16), ones) + L2_EPS) * (DK ** -0.5))
            kn = kh * lax.rsqrt(_dotb((kh * kh).astype(BF16), ones) + L2_EPS)
            kn_t = kn.T
            kn_tb = kn_t.astype(BF16)
            kk = _dotb(kn.astype(BF16), kn_tb)
            qk = _dotb(qn.astype(BF16), kn_tb)
            yield
            for j in range(V_PER_QK):
                h = q * V_PER_QK + j
                lane_b = (hq * HQ_PER_STEP + q) * V_PER_QK + j
                lane_a = V_HEADS + lane_b
                gcol = jnp.sum(jnp.where(col == lane_a, gcum, 0.0), axis=1, keepdims=True)
                bcol = jnp.sum(jnp.where(col == lane_b, beta_all, 0.0), axis=1, keepdims=True)
                grow = jnp.sum(jnp.where(row == lane_a, gcum_t, 0.0), axis=0, keepdims=True)
                glast = gcol[CHUNK - 1:CHUNK, :]
                decay = jnp.exp(jnp.where(causal, gcol - grow, -jnp.inf))
                a_mat = jnp.where(strict, kk * decay, 0.0) * bcol
                eg = jnp.exp(gcol)
                rhs = jnp.concatenate([vc[:, h * DV:(h + 1) * DV] * bcol, kn * (bcol * eg)], axis=1).astype(BF16)
                qd_s[h, rows, :] = (qn * eg).astype(BF16)
                at_s[h, rows, :] = (qk * decay).astype(BF16)
                kdt_s[h, rows, :] = (kn_t * jnp.exp(glast - grow)).astype(BF16)
                gl_s[h, pl.ds(n, 1), :] = jnp.broadcast_to(jnp.exp(glast), (1, DV))
                a_s[slot, base + h] = a_mat
                rhs_s[slot, base + h] = rhs
                yield

    def prep(it):
        it = jnp.asarray(it, jnp.int32)
        slot = it % 2
        for c in range(PH1_CHUNKS):
            yield from chunk_prep(PH1_CHUNKS * it + c, slot, c * nv)

    def solve(it):
        it = jnp.asarray(it, jnp.int32)
        slot = it % 2
        idx = range(PH1_CHUNKS * nv)
        t_mats = []
        yield from _inv_unit_lower([a_s[slot, i] for i in idx], tab32_ref, tab16_ref, t_mats)
        sols = [_dotb(t_mats[i].astype(BF16), rhs_s[slot, i]) for i in idx]
        yield
        for i in idx:
            rows = pl.ds(pl.multiple_of((PH1_CHUNKS * it + i // nv) * CHUNK, CHUNK), CHUNK)
            u_s[i % nv, rows, :] = sols[i][:, :DV]
            w_s[i % nv, rows, :] = sols[i][:, DV:].astype(BF16)
        yield

    gain = on_ref[...]
    heads = range(nv)

    def recur(it):
        it = jnp.asarray(it, jnp.int32)
        for c in range(PH1_CHUNKS):
            n = PH1_CHUNKS * it + c
            rows = pl.ds(pl.multiple_of(n * CHUNK, CHUNK), CHUNK)
            states = [s_ref[0, h] for h in heads]
            wq = [jnp.concatenate([w_s[h, rows, :], qd_s[h, rows, :]], axis=0) for h in heads]
            ws_qs = [_dotb(wq[h], states[h].astype(BF16)) for h in heads]
            yield
            v_new = [(u_s[h, rows, :] - ws_qs[h][:CHUNK]).astype(BF16) for h in heads]
            o = [ws_qs[h][CHUNK:] + _dotb(at_s[h, rows, :], v_new[h]) for h in heads]
            s_new = [states[h] * gl_s[h, pl.ds(n, 1), :] + _dotb(kdt_s[h, rows, :], v_new[h]) for h in heads]
            yield
            for h in heads:
                s_ref[0, h] = s_new[h]
                zg = z_ref[rows, h * DV:(h + 1) * DV]
                og = o[h] * lax.rsqrt(jnp.mean(o[h] * o[h], axis=-1, keepdims=True) + RMS_EPS) * gain * _silu(zg)
                og_ref[rows, h * DV:(h + 1) * DV] = og.astype(BF16)
            yield

    for h in heads:
        s_ref[0, h] = jnp.zeros((DK, DV), F32)
    _trace_interleaved(prep(0))
    _trace_interleaved(solve(0), prep(1))

    def steady(it, carry):
        _trace_interleaved(solve(it - 1), prep(it), recur(it - 2))
        return carry

    lax.fori_loop(2, n_iters, steady, 0)
    _trace_interleaved(solve(n_iters - 1), recur(n_iters - 2))
    _trace_interleaved(recur(n_iters - 1))


def _gdn_prompt(proj, ba, conv_w, gate_params, o_gain, *, batch, t_len):
    wq = HQ_PER_STEP * DK
    wv = HQ_PER_STEP * V_PER_QK * DV
    nv = HQ_PER_STEP * V_PER_QK
    gate_blk = pl.BlockSpec((t_len, 128), lambda b, h: (b, 0))
    in_specs = [
        pl.BlockSpec((t_len, wq), lambda b, h: (b, h)),
        pl.BlockSpec((t_len, wq), lambda b, h: (b, KEY_DIM // wq + h)),
        pl.BlockSpec((t_len, wv), lambda b, h: (b, (2 * KEY_DIM) // wv + h)),
        pl.BlockSpec((t_len, wv), lambda b, h: (b, CONV_DIM // wv + h)),
        gate_blk,
        gate_blk,
        gate_blk,
        pl.BlockSpec((CONV_W, wq), lambda b, h: (0, h)),
        pl.BlockSpec((CONV_W, wq), lambda b, h: (0, KEY_DIM // wq + h)),
        pl.BlockSpec((CONV_W, wv), lambda b, h: (0, (2 * KEY_DIM) // wv + h)),
        pl.BlockSpec((1, DV), lambda b, h: (0, 0)),
        pl.BlockSpec((2, CHUNK, CHUNK), lambda b, h: (0, 0, 0)),
        pl.BlockSpec((1 + N_LEVELS, CHUNK, CHUNK), lambda b, h: (0, 0, 0)),
    ]
    tab32, tab16 = _chunk_tables()
    beta, gcum, gcum_t = _chunk_gates(ba, gate_params, tab16[0])
    out_shape = [
        jax.ShapeDtypeStruct((batch * t_len, VAL_DIM), BF16),
        jax.ShapeDtypeStruct((batch, V_HEADS, DK, DV), F32),
    ]
    out_specs = [
        pl.BlockSpec((t_len, wv), lambda b, h: (b, h)),
        pl.BlockSpec((1, nv, DK, DV), lambda b, h: (b, h, 0, 0)),
    ]
    scratch = [
        pltpu.VMEM((nv, t_len, DV), F32),
        pltpu.VMEM((nv, t_len, DK), BF16),
        pltpu.VMEM((nv, t_len, DK), BF16),
        pltpu.VMEM((nv, t_len, CHUNK), BF16),
        pltpu.VMEM((nv, t_len, CHUNK), BF16),
        pltpu.VMEM((nv, t_len // CHUNK, DV), F32),
        pltpu.VMEM((2, PH1_CHUNKS * nv, CHUNK, CHUNK), F32),
        pltpu.VMEM((2, PH1_CHUNKS * nv, CHUNK, DV + DK), BF16),
    ]
    return pl.pallas_call(
        _gdn_prompt_body,
        out_shape=out_shape,
        grid=(batch, QK_HEADS // HQ_PER_STEP),
        in_specs=in_specs,
        out_specs=out_specs,
        scratch_shapes=scratch,
        compiler_params=_params(("parallel", "arbitrary")),
        name="gdn_prompt",
    )(proj, proj, proj, proj, beta, gcum, gcum_t, conv_w, conv_w, conv_w, o_gain.reshape(1, DV), tab32, tab16)


def _gdn_gates_body(ba_ref, gp_ref, beta_ref, eg_ref):
    ba = ba_ref[...]
    beta_ref[...] = jax.nn.sigmoid(ba)
    xa = ba + gp_ref[1:2, :]
    g = -jnp.exp(gp_ref[0:1, :]) * (jnp.maximum(xa, 0.0) + jnp.log1p(jnp.exp(-jnp.abs(xa))))
    eg_ref[...] = jnp.exp(g)


def _gdn_gates(ba, gate_params):
    m = ba.shape[0]
    return pl.pallas_call(
        _gdn_gates_body,
        out_shape=[jax.ShapeDtypeStruct((m, 128), F32)] * 2,
        name="gdn_gates",
    )(ba, gate_params)


def _gdn_decode_body(beta_ref, eg_ref, xs_ref, cw_ref, z_ref, on_ref, s_ref, so_ref, og_ref):
    for si in range(DEC_SAMPLES):
        b = pl.program_id(0) * DEC_SAMPLES + si
        y = xs_ref[si, 0] * cw_ref[0]
        for j in range(1, CONV_W):
            y = y + xs_ref[si, j] * cw_ref[j]
        y = _silu(y)
        q = y[0:QK_HEADS]
        k = y[QK_HEADS:2 * QK_HEADS]
        v = y[2 * QK_HEADS:]
        qn = q * lax.rsqrt(jnp.sum(q * q, axis=-1, keepdims=True) + L2_EPS) * (DK ** -0.5)
        kn = k * lax.rsqrt(jnp.sum(k * k, axis=-1, keepdims=True) + L2_EPS)
        qk_t = jnp.concatenate([qn, kn, jnp.zeros((DK - 2 * QK_HEADS, DK), F32)], axis=0).T
        for g0 in range(0, V_HEADS, DEC_GROUP):
            hs = range(g0, g0 + DEC_GROUP)
            qcol = {h: qk_t[:, h // V_PER_QK:h // V_PER_QK + 1] for h in hs}
            kcol = {h: qk_t[:, QK_HEADS + h // V_PER_QK:QK_HEADS + h // V_PER_QK + 1] for h in hs}
            sd = {h: s_ref[si, h] * eg_ref[b, h] for h in hs}
            ks = {h: jnp.sum(sd[h] * kcol[h], axis=0, keepdims=True) for h in hs}
            v_new = {h: beta_ref[b, h] * (v[h:h + 1, :] - ks[h]) for h in hs}
            sn = {h: sd[h] + kcol[h] * v_new[h] for h in hs}
            for h in hs:
                so_ref[si, h] = sn[h]
            o = {h: jnp.sum(sn[h] * qcol[h], axis=0, keepdims=True) for h in hs}
            for h in hs:
                og_ref[si, h:h + 1, :] = o[h]
        o_all = og_ref[si]
        og_ref[si] = (o_all * lax.rsqrt(jnp.mean(o_all * o_all, axis=-1, keepdims=True) + RMS_EPS)
                      * on_ref[...] * _silu(z_ref[si]))


def _gdn_decode(beta, eg, xs, conv_w3, z3, o_gain, state):
    nb = xs.shape[0]
    slots = CONV_DIM // DK
    smem = pl.BlockSpec(memory_space=pltpu.SMEM)
    return pl.pallas_call(
        _gdn_decode_body,
        out_shape=[
            jax.ShapeDtypeStruct((nb, V_HEADS, DK, DV), F32),
            jax.ShapeDtypeStruct((nb, V_HEADS, DV), F32),
        ],
        grid=(nb // DEC_SAMPLES,),
        in_specs=[
            smem,
            smem,
            pl.BlockSpec((DEC_SAMPLES, CONV_W, slots, DK), lambda b: (b, 0, 0, 0)),
            pl.BlockSpec((CONV_W, slots, DK), lambda b: (0, 0, 0)),
            pl.BlockSpec((DEC_SAMPLES, V_HEADS, DV), lambda b: (b, 0, 0)),
            pl.BlockSpec((1, DV), lambda b: (0, 0)),
            pl.BlockSpec((DEC_SAMPLES, V_HEADS, DK, DV), lambda b: (b, 0, 0, 0)),
        ],
        out_specs=[
            pl.BlockSpec((DEC_SAMPLES, V_HEADS, DK, DV), lambda b: (b, 0, 0, 0)),
            pl.BlockSpec((DEC_SAMPLES, V_HEADS, DV), lambda b: (b, 0, 0)),
        ],
        compiler_params=_params(("parallel",)),
        name="gdn_decode",
    )(beta, eg, xs, conv_w3, z3, o_gain.reshape(1, DV), state)


def _ssm_pow_body(lr_ref, li_ref, ldt_ref, pr_ref, pi_ref):
    n = jnp.where(lax.broadcasted_iota(jnp.int32, pr_ref.shape, 0) == 0, 1.0, float(SCAN_SEG))
    dt = jnp.exp(ldt_ref[...])
    lr = jnp.minimum(lr_ref[...], -1e-4) * dt
    li = li_ref[...] * dt
    mag = jnp.exp(n * lr)
    pr_ref[...] = mag * jnp.cos(n * li)
    pi_ref[...] = mag * jnp.sin(n * li)


def _ssm_pow(lr, li, ldt):
    gp = lr.shape[1]
    tl = 2048
    spec = pl.BlockSpec((1, tl), lambda i: (0, i))
    ospec = pl.BlockSpec((POW_ROWS, tl), lambda i: (0, i))
    return pl.pallas_call(
        _ssm_pow_body,
        out_shape=[jax.ShapeDtypeStruct((POW_ROWS, gp), F32)] * 2,
        grid=(gp // tl,),
        in_specs=[spec, spec, spec],
        out_specs=[ospec, ospec],
        name="ssm_pow",
    )(lr, li, ldt)


def _ssm_blocks_body(lr_ref, li_ref, ldt_ref, br_ref, bi_ref, cr_ref, ci_ref, bblk_ref, cblk_ref):
    dt = jnp.exp(ldt_ref[...])
    lr = jnp.minimum(lr_ref[...], -1e-4)
    li = li_ref[...]
    mag = jnp.exp(lr * dt)
    xr = mag * jnp.cos(li * dt) - 1.0
    xi = mag * jnp.sin(li * dt)
    den = 1.0 / (lr * lr + li * li)
    fr = (xr * lr + xi * li) * den
    fi = (xi * lr - xr * li) * den
    bblk_ref[...] = jnp.zeros_like(bblk_ref)
    cblk_ref[...] = jnp.zeros_like(cblk_ref)
    per_tile = 128 // SSM_STATE
    for g in range(GROUPS_PER_STEP):
        rows = slice(g * SSM_GROUP, (g + 1) * SSM_GROUP)
        off = (g // per_tile) * 256 + (g % per_tile) * SSM_STATE
        cols = slice(off, off + SSM_STATE)
        cols_im = slice(off + 128, off + 128 + SSM_STATE)
        br = br_ref[g]
        bi = bi_ref[g]
        frg = fr[g:g + 1, :]
        fig = fi[g:g + 1, :]
        bblk_ref[rows, cols] = (frg * br - fig * bi).astype(BF16)
        bblk_ref[rows, cols_im] = (frg * bi + fig * br).astype(BF16)
        cblk_ref[rows, cols] = cr_ref[g].astype(BF16)
        cblk_ref[rows, cols_im] = (-ci_ref[g]).astype(BF16)


def _ssm_blocks(lr, li, ldt, bt_re, bt_im, c_re, c_im):
    g, p = lr.shape
    k = GROUPS_PER_STEP
    s2 = pl.BlockSpec((k, p), lambda i: (i, 0))
    s1 = pl.BlockSpec((k, 1), lambda i: (i, 0))
    s3 = pl.BlockSpec((k, SSM_GROUP, p), lambda i: (i, 0, 0))
    blk = pl.BlockSpec((None, CH_PER_STEP, 2 * ST_PER_STEP), lambda i: (i, 0, 0))
    return pl.pallas_call(
        _ssm_blocks_body,
        out_shape=[jax.ShapeDtypeStruct((g // k, CH_PER_STEP, 2 * ST_PER_STEP), BF16)] * 2,
        grid=(g // k,),
        in_specs=[s2, s2, s1, s3, s3, s3, s3],
        out_specs=[blk, blk],
        name="ssm_blocks",
    )(lr, li, ldt, bt_re, bt_im, c_re, c_im)


def _dot_nt(a, b):
    return lax.dot_general(a, b, (((1,), (1,)), ((), ())), preferred_element_type=F32)


def _gelu_tanh(x):
    return 0.5 * x * (1.0 + jnp.tanh(math.sqrt(2.0 / math.pi) * (x + 0.044715 * (x * x * x))))


def _ssm_scan_body(ua_ref, ub_ref, b_ref, c_ref, pr_ref, pi_ref, d_ref, y_ref, hr_ref, hi_ref,
                   h_s, y_s, cr_s, ci_s):
    tb = pl.program_id(2)
    nt = ST_PER_STEP // 128
    u_perm = jnp.concatenate(
        [jnp.concatenate([ua_ref[pl.ds(i, 8, stride=SCAN_SEG), :], ub_ref[pl.ds(i, 8, stride=SCAN_SEG), :]], axis=1)
         for i in range(SCAN_SEG)], axis=0).astype(BF16)

    @pl.when(tb == 0)
    def _():
        cr_s[...] = jnp.zeros_like(cr_s)
        ci_s[...] = jnp.zeros_like(ci_s)

    def b_proj(g):
        for c in range(g * SCAN_TILES, (g + 1) * SCAN_TILES):
            bu = jnp.dot(u_perm, b_ref[:, c * 256:(c + 1) * 256], preferred_element_type=F32)
            h_s[c] = bu[:, :128]
            h_s[nt + c] = bu[:, 128:]

    row8 = lax.broadcasted_iota(jnp.int32, (8, 128), 0)

    def tile_scan(c):
        lanes = slice(c * 128, (c + 1) * 128)
        ar = jnp.broadcast_to(pr_ref[0:1, lanes], (8, 128))
        ai = jnp.broadcast_to(pi_ref[0:1, lanes], (8, 128))
        hr = jnp.zeros((8, 128), F32)
        hi = jnp.zeros((8, 128), F32)
        for i in range(SCAN_SEG):
            xr = h_s[c, i * 8:(i + 1) * 8, :]
            xi = h_s[nt + c, i * 8:(i + 1) * 8, :]
            hr, hi = ar * hr - ai * hi + xr, ar * hi + ai * hr + xi
            if i % SCAN_YIELD == SCAN_YIELD - 1:
                yield

        asr = pr_ref[1:2, lanes]
        asi = pi_ref[1:2, lanes]
        c_r = cr_s[:, lanes]
        c_i = ci_s[:, lanes]
        cin_r = jnp.zeros((8, 128), F32)
        cin_i = jnp.zeros((8, 128), F32)
        for s in range(8):
            cin_r = jnp.where(row8 == s, c_r, cin_r)
            cin_i = jnp.where(row8 == s, c_i, cin_i)
            e_r = hr[s:s + 1, :]
            e_i = hi[s:s + 1, :]
            c_r, c_i = e_r + asr * c_r - asi * c_i, e_i + asr * c_i + asi * c_r
        cr_s[:, lanes] = c_r
        ci_s[:, lanes] = c_i
        yield

        hr, hi = cin_r, cin_i
        for i in range(SCAN_SEG):
            xr = h_s[c, i * 8:(i + 1) * 8, :]
            xi = h_s[nt + c, i * 8:(i + 1) * 8, :]
            hr, hi = ar * hr - ai * hi + xr, ar * hi + ai * hr + xi
            h_s[c, i * 8:(i + 1) * 8, :] = hr
            h_s[nt + c, i * 8:(i + 1) * 8, :] = hi
            if i % SCAN_YIELD == SCAN_YIELD - 1:
                yield

    n_groups = nt // SCAN_TILES
    b_proj(0)
    y = None
    for g in range(n_groups):
        if g + 1 < n_groups:
            b_proj(g + 1)
        tiles = range(g * SCAN_TILES, (g + 1) * SCAN_TILES)
        _trace_interleaved(*[tile_scan(c) for c in tiles])
        for c in tiles:
            h_cat = jnp.concatenate([h_s[c], h_s[nt + c]], axis=1).astype(BF16)
            y_c = _dot_nt(h_cat, c_ref[:, c * 256:(c + 1) * 256])
            y = y_c if y is None else y + y_c

    y_s[0] = y[:, :128]
    y_s[1] = y[:, 128:]
    per_seg = SCAN_SEG // 8
    y_nat = jnp.concatenate(
        [jnp.concatenate([y_s[h, pl.ds((j % per_seg) * 64 + j // per_seg, 8, stride=8), :] for h in range(2)], axis=1)
         for j in range(SCAN_BLOCK // 8)], axis=0)
    u = jnp.concatenate([ua_ref[...], ub_ref[...]], axis=1)
    y_ref[...] = _gelu_tanh(y_nat + d_ref[...] * u)

    @pl.when(tb == pl.num_programs(2) - 1)
    def _():
        hr_ref[...] = cr_s[...]
        hi_ref[...] = ci_s[...]


def _ssm_scan(uz, bblk, cblk, pw_r, pw_i, d_skip, *, batch, t_len):
    nt = t_len // SCAN_BLOCK
    return pl.pallas_call(
        _ssm_scan_body,
        out_shape=[
            jax.ShapeDtypeStruct((batch * t_len, SSM_WIDTH), F32),
            jax.ShapeDtypeStruct((batch, 1, SSM_GROUPS * SSM_STATE), F32),
            jax.ShapeDtypeStruct((batch, 1, SSM_GROUPS * SSM_STATE), F32),
        ],
        grid=(batch, N_GSTEPS, nt),
        in_specs=[
            pl.BlockSpec((SCAN_BLOCK, 128), lambda b, g, t: (b * nt + t, 2 * g)),
            pl.BlockSpec((SCAN_BLOCK, 128), lambda b, g, t: (b * nt + t, 2 * g + 1)),
            pl.BlockSpec((None, CH_PER_STEP, 2 * ST_PER_STEP), lambda b, g, t: (g, 0, 0)),
            pl.BlockSpec((None, CH_PER_STEP, 2 * ST_PER_STEP), lambda b, g, t: (g, 0, 0)),
            pl.BlockSpec((POW_ROWS, ST_PER_STEP), lambda b, g, t: (0, g)),
            pl.BlockSpec((POW_ROWS, ST_PER_STEP), lambda b, g, t: (0, g)),
            pl.BlockSpec((1, CH_PER_STEP), lambda b, g, t: (0, g)),
        ],
        out_specs=[
            pl.BlockSpec((SCAN_BLOCK, CH_PER_STEP), lambda b, g, t: (b * nt + t, g)),
            pl.BlockSpec((None, 1, ST_PER_STEP), lambda b, g, t: (b, 0, g)),
            pl.BlockSpec((None, 1, ST_PER_STEP), lambda b, g, t: (b, 0, g)),
        ],
        scratch_shapes=[
            pltpu.VMEM((2 * ST_PER_STEP // 128, SCAN_BLOCK, 128), F32),
            pltpu.VMEM((CH_PER_STEP // 128, SCAN_BLOCK, 128), F32),
            pltpu.VMEM((1, ST_PER_STEP), F32),
            pltpu.VMEM((1, ST_PER_STEP), F32),
        ],
        compiler_params=_params(("parallel", "parallel", "arbitrary")),
        name="ssm_scan",
    )(uz, uz, bblk, cblk, pw_r, pw_i, d_skip.reshape(1, SSM_WIDTH))


def _ssm_decode_body(u_ref, b_ref, c_ref, pr_ref, pi_ref, d_ref, h0r_ref, h0i_ref, y_ref, hr_ref, hi_ref):
    nt = ST_PER_STEP // 128
    u = u_ref[...]
    bu = jnp.dot(u.astype(BF16), b_ref[...], preferred_element_type=F32)
    h_cat = []
    for c in range(nt):
        lanes = slice(c * 128, (c + 1) * 128)
        ar = pr_ref[0:1, lanes]
        ai = pi_ref[0:1, lanes]
        h0r = h0r_ref[:, lanes]
        h0i = h0i_ref[:, lanes]
        hr = bu[:, c * 256:c * 256 + 128] + ar * h0r - ai * h0i
        hi = bu[:, c * 256 + 128:(c + 1) * 256] + ar * h0i + ai * h0r
        hr_ref[:, lanes] = hr
        hi_ref[:, lanes] = hi
        h_cat += [hr, hi]
    y = _dot_nt(jnp.concatenate(h_cat, axis=1).astype(BF16), c_ref[...])
    y_ref[...] = _gelu_tanh(y + d_ref[...] * u)


def _ssm_decode(uz, bblk, cblk, pw_r, pw_i, d_skip, h0r, h0i):
    nb = uz.shape[0]
    return pl.pallas_call(
        _ssm_decode_body,
        out_shape=[
            jax.ShapeDtypeStruct((nb, SSM_WIDTH), F32),
            jax.ShapeDtypeStruct((nb, SSM_GROUPS * SSM_STATE), F32),
            jax.ShapeDtypeStruct((nb, SSM_GROUPS * SSM_STATE), F32),
        ],
        grid=(N_GSTEPS,),
        in_specs=[
            pl.BlockSpec((nb, CH_PER_STEP), lambda g: (0, g)),
            pl.BlockSpec((None, CH_PER_STEP, 2 * ST_PER_STEP), lambda g: (g, 0, 0)),
            pl.BlockSpec((None, CH_PER_STEP, 2 * ST_PER_STEP), lambda g: (g, 0, 0)),
            pl.BlockSpec((POW_ROWS, ST_PER_STEP), lambda g: (0, g)),
            pl.BlockSpec((POW_ROWS, ST_PER_STEP), lambda g: (0, g)),
            pl.BlockSpec((1, CH_PER_STEP), lambda g: (0, g)),
            pl.BlockSpec((nb, ST_PER_STEP), lambda g: (0, g)),
            pl.BlockSpec((nb, ST_PER_STEP), lambda g: (0, g)),
        ],
        out_specs=[
            pl.BlockSpec((nb, CH_PER_STEP), lambda g: (0, g)),
            pl.BlockSpec((nb, ST_PER_STEP), lambda g: (0, g)),
            pl.BlockSpec((nb, ST_PER_STEP), lambda g: (0, g)),
        ],
        compiler_params=_params(("parallel",)),
        name="ssm_decode",
    )(uz, bblk, cblk, pw_r, pw_i, d_skip.reshape(1, SSM_WIDTH), h0r, h0i)


GLU_SLAB = 256


def _glu_body(y_ref, w_ref, b_ref, z_ref, o_ref, yb_scr, *, tn):
    j = pl.program_id(1)

    @pl.when(j == 0)
    def _():
        yb_scr[...] = y_ref[...].astype(BF16)

    for c in range(tn // GLU_SLAB):
        cols = slice(c * GLU_SLAB, (c + 1) * GLU_SLAB)
        t = jnp.dot(yb_scr[...], w_ref[:, cols], preferred_element_type=F32) + b_ref[:, cols]
        y = y_ref[:, pl.ds(pl.multiple_of(j * tn + c * GLU_SLAB, GLU_SLAB), GLU_SLAB)]
        o_ref[:, cols] = (y * jax.nn.sigmoid(t) * _silu(z_ref[:, cols])).astype(BF16)


def _glu(y, w, b, uz, *, tm, tn):
    m, d = y.shape
    zoff = SSM_WIDTH // tn
    return pl.pallas_call(
        functools.partial(_glu_body, tn=tn),
        out_shape=jax.ShapeDtypeStruct((m, d), BF16),
        grid=(m // tm, d // tn),
        in_specs=[
            pl.BlockSpec((tm, d), lambda i, j: (i, 0)),
            pl.BlockSpec((d, tn), lambda i, j: (0, j)),
            pl.BlockSpec((1, tn), lambda i, j: (0, j)),
            pl.BlockSpec((tm, tn), lambda i, j: (i, zoff + j)),
        ],
        out_specs=pl.BlockSpec((tm, tn), lambda i, j: (i, j)),
        scratch_shapes=[pltpu.VMEM((tm, d), BF16)],
        compiler_params=_params(("parallel", "arbitrary")),
        name="ssm_glu",
    )(y, w, b.reshape(1, d), uz)


def _prep_weights(norm_gdn, w_in_gdn, conv_gdn, a_log_gdn, dt_bias_gdn, onorm_gdn, w_out_gdn,
                  norm_ssm, w_in_ssm, lam_re, lam_im, b_re, b_im, c_re, c_im, d_ssm, log_dt_ssm,
                  w_glu_ssm, b_glu_ssm, w_out_ssm, norm_final):
    w_in_t = jnp.swapaxes(w_in_gdn[0], 0, 1)
    w_ba = jnp.pad(w_in_t[MAIN_DIM:], ((0, 128 - 2 * V_HEADS), (0, 0))).astype(BF16)
    zeros = jnp.zeros((V_HEADS,), F32)
    pad = jnp.zeros((128 - 2 * V_HEADS,), F32)
    gate_params = jnp.stack([jnp.concatenate([zeros, a_log_gdn[0], pad]),
                             jnp.concatenate([zeros, dt_bias_gdn[0], pad])])

    lr, li, ldt = lam_re[0], lam_im[0], log_dt_ssm[0]
    gp = SSM_GROUPS * SSM_STATE
    pw_r, pw_i = _ssm_pow(lr.reshape(1, gp), li.reshape(1, gp),
                          jnp.repeat(ldt, SSM_STATE).reshape(1, gp))
    bblk, cblk = _ssm_blocks(lr, li, ldt.reshape(SSM_GROUPS, 1),
                                 jnp.swapaxes(b_re[0], 1, 2), jnp.swapaxes(b_im[0], 1, 2), c_re[0], c_im[0])
    return dict(
        norm_gdn=norm_gdn[0], w_main=w_in_t, w_ba=w_ba, conv_w=conv_gdn[0], gate_params=gate_params,
        o_gain=onorm_gdn[0], w_out_gdn=w_out_gdn[0].astype(BF16),
        norm_ssm=norm_ssm[0], w_in_ssm=w_in_ssm[0], pw_r=pw_r, pw_i=pw_i, bblk=bblk, cblk=cblk,
        d_skip=d_ssm[0], w_glu=w_glu_ssm[0].astype(BF16), b_glu=b_glu_ssm[0], w_out_ssm=w_out_ssm[0].astype(BF16),
        norm_final=norm_final)


def _ssm_tail(p, x1, y, uz, *, tm, tm_glu):
    y3 = _glu(y, p["w_glu"], p["b_glu"], uz, tm=tm_glu, tn=1024)
    return _outproj(y3, p["w_out_ssm"], x1, p["norm_final"], tm=tm, name="ssm_out")


def _prompt_path(p, x_prompt):
    batch, t_len, d = x_prompt.shape
    x = x_prompt.reshape(batch * t_len, d)
    proj, ba = _norm_matmul(x, p["norm_gdn"], p["w_main"], p["w_ba"], n=MAIN_DIM, tm=2048, tn=512, name="gdn_in",
                            w_transposed=True)
    og, delta = _gdn_prompt(proj, ba, p["conv_w"], p["gate_params"], p["o_gain"], batch=batch, t_len=t_len)
    conv_state = proj.reshape(batch, t_len, MAIN_DIM)[:, t_len - (CONV_W - 1):, :CONV_DIM]
    x1 = _outproj(og, p["w_out_gdn"], x, tm=512, name="gdn_out")
    uz = _norm_matmul(x1, p["norm_ssm"], p["w_in_ssm"], n=2 * SSM_WIDTH, tm=2048, tn=512, name="ssm_in")
    y, h_re, h_im = _ssm_scan(uz, p["bblk"], p["cblk"], p["pw_r"], p["pw_i"], p["d_skip"],
                              batch=batch, t_len=t_len)
    out = _ssm_tail(p, x1, y, uz, tm=512, tm_glu=512)
    return (out.reshape(batch, t_len, d), conv_state[None], delta[None],
            h_re.reshape(1, batch, SSM_GROUPS, SSM_STATE), h_im.reshape(1, batch, SSM_GROUPS, SSM_STATE))


def _sample_path(p, x_sample, state_conv, state_delta, state_re, state_im):
    nb, _, d = x_sample.shape
    x = x_sample.reshape(nb, d)
    proj, ba = _norm_matmul(x, p["norm_gdn"], p["w_main"], p["w_ba"], n=MAIN_DIM, tm=nb, tn=512, name="gdn_in_s",
                            w_transposed=True)
    beta, eg = _gdn_gates(ba, p["gate_params"])
    xs = jnp.concatenate([state_conv[0], proj[:, None, :CONV_DIM]], axis=1)
    slots = CONV_DIM // DK
    delta, og = _gdn_decode(beta[:, :V_HEADS], eg[:, V_HEADS:2 * V_HEADS],
                            xs.reshape(nb, CONV_W, slots, DK), p["conv_w"].reshape(CONV_W, slots, DK),
                            proj[:, CONV_DIM:].reshape(nb, V_HEADS, DV), p["o_gain"], state_delta[0])
    x1 = _outproj(og.reshape(nb, VAL_DIM).astype(BF16), p["w_out_gdn"], x, tm=nb, name="gdn_out_s")
    uz = _norm_matmul(x1, p["norm_ssm"], p["w_in_ssm"], n=2 * SSM_WIDTH, tm=nb, tn=512, name="ssm_in_s")
    gp = SSM_GROUPS * SSM_STATE
    y, h_re, h_im = _ssm_decode(uz, p["bblk"], p["cblk"], p["pw_r"], p["pw_i"], p["d_skip"],
                                state_re[0].reshape(nb, gp), state_im[0].reshape(nb, gp))
    out = _ssm_tail(p, x1, y, uz, tm=nb, tm_glu=nb)
    return (out.reshape(nb, 1, d), xs[:, 1:][None], delta[None],
            h_re.reshape(1, nb, SSM_GROUPS, SSM_STATE), h_im.reshape(1, nb, SSM_GROUPS, SSM_STATE))


def kernel(x_prompt, x_sample, state_gdn_conv, state_gdn_delta, state_ssm_re, state_ssm_im, norm_gdn, w_in_gdn, conv_gdn, a_log_gdn, dt_bias_gdn, onorm_gdn, w_out_gdn, norm_ssm, w_in_ssm, lam_re, lam_im, b_re, b_im, c_re, c_im, d_ssm, log_dt_ssm, w_glu_ssm, b_glu_ssm, w_out_ssm, norm_final):
    p = _prep_weights(norm_gdn, w_in_gdn, conv_gdn, a_log_gdn, dt_bias_gdn, onorm_gdn, w_out_gdn,
                      norm_ssm, w_in_ssm, lam_re, lam_im, b_re, b_im, c_re, c_im, d_ssm, log_dt_ssm,
                      w_glu_ssm, b_glu_ssm, w_out_ssm, norm_final)
    y_p, conv_p, delta_p, re_p, im_p = _prompt_path(p, x_prompt)
    y_s, conv_s, delta_s, re_s, im_s = _sample_path(p, x_sample, state_gdn_conv, state_gdn_delta,
                                                    state_ssm_re, state_ssm_im)
    return (y_p, y_s, conv_p, delta_p, re_p, im_p, conv_s, delta_s, re_s, im_s)
```

```python
import functools
import math

import jax
import jax.numpy as jnp
from jax import lax
from jax.experimental import pallas as pl
from jax.experimental.pallas import tpu as pltpu

F32 = jnp.float32
BF16 = jnp.bfloat16

RMS_EPS = 1e-6
L2_EPS = 1e-6

D_MODEL = 2048
QK_HEADS = 16
V_HEADS = 32
DK = 128
DV = 128
KEY_DIM = QK_HEADS * DK
VAL_DIM = V_HEADS * DV
CONV_DIM = 2 * KEY_DIM + VAL_DIM
CONV_W = 4
MAIN_DIM = CONV_DIM + VAL_DIM

SSM_WIDTH = 4096
SSM_GROUP = 16
SSM_GROUPS = SSM_WIDTH // SSM_GROUP
SSM_STATE = 64
GROUPS_PER_STEP = 16
CH_PER_STEP = GROUPS_PER_STEP * SSM_GROUP
ST_PER_STEP = GROUPS_PER_STEP * SSM_STATE
N_GSTEPS = SSM_GROUPS // GROUPS_PER_STEP

CHUNK = 128
SCAN_BLOCK = 1024
SCAN_SEG = SCAN_BLOCK // 8
POW_ROWS = 8
SCAN_TILES = 1
SCAN_YIELD = 4

VMEM_LIMIT = 56 * 1024 * 1024


def _params(sem):
    return pltpu.CompilerParams(dimension_semantics=sem, vmem_limit_bytes=VMEM_LIMIT)


def _mm(a, b):
    return jnp.dot(a.astype(BF16), b.astype(BF16), preferred_element_type=F32)


def _mm_hi(a, b):
    return jnp.dot(a, b, precision=lax.Precision.HIGHEST, preferred_element_type=F32)


def _silu(x):
    return x * jax.nn.sigmoid(x)


def _rms(x, g):
    ms = jnp.mean(x * x, axis=-1, keepdims=True)
    return x * lax.rsqrt(ms + RMS_EPS) * g


def _norm_matmul_body(x_ref, g_ref, w_ref, *rest, has_aux, w_transposed):
    if has_aux:
        waux_ref, o_ref, aux_ref, h_scr = rest
    else:
        o_ref, h_scr = rest
    mm = _dot_nt if w_transposed else _dotb

    @pl.when(pl.program_id(1) == 0)
    def _():
        h = _rms(x_ref[...], g_ref[...]).astype(BF16)
        h_scr[...] = h
        if has_aux:
            aux_ref[...] = mm(h, waux_ref[...])

    o_ref[...] = mm(h_scr[...], w_ref[...].astype(BF16))


def _norm_matmul(x, g, w, w_aux=None, *, n, tm, tn, name, w_transposed=False):
    m, d = x.shape
    has_aux = w_aux is not None
    in_specs = [
        pl.BlockSpec((tm, d), lambda i, j: (i, 0), pipeline_mode=pl.Buffered(1)),
        pl.BlockSpec((1, d), lambda i, j: (0, 0)),
        pl.BlockSpec((tn, d), lambda i, j: (j, 0)) if w_transposed else pl.BlockSpec((d, tn), lambda i, j: (0, j)),
    ]
    out_shape = [jax.ShapeDtypeStruct((m, n), F32)]
    out_specs = [pl.BlockSpec((tm, tn), lambda i, j: (i, j))]
    args = [x, g.reshape(1, d), w]
    if has_aux:
        na = w_aux.shape[0] if w_transposed else w_aux.shape[1]
        in_specs.append(pl.BlockSpec(w_aux.shape, lambda i, j: (0, 0)))
        out_shape.append(jax.ShapeDtypeStruct((m, na), F32))
        out_specs.append(pl.BlockSpec((tm, na), lambda i, j: (i, 0)))
        args.append(w_aux)
    res = pl.pallas_call(
        functools.partial(_norm_matmul_body, has_aux=has_aux, w_transposed=w_transposed),
        out_shape=out_shape,
        grid=(m // tm, n // tn),
        in_specs=in_specs,
        out_specs=out_specs,
        scratch_shapes=[pltpu.VMEM((tm, d), BF16)],
        compiler_params=_params(("parallel", "arbitrary")),
        name=name,
    )(*args)
    return res if has_aux else res[0]


def _outproj_body(a_ref, w_ref, x_ref, *rest, final_norm):
    if final_norm:
        g_ref, o_ref = rest
    else:
        (o_ref,) = rest
    y = x_ref[...] + jnp.dot(a_ref[...], w_ref[...], preferred_element_type=F32)
    o_ref[...] = _rms(y, g_ref[...]) if final_norm else y


def _outproj(a, w, x, g=None, *, tm, name):
    m, kd = a.shape
    n = w.shape[1]
    final_norm = g is not None
    in_specs = [
        pl.BlockSpec((tm, kd), lambda i: (i, 0)),
        pl.BlockSpec((kd, n), lambda i: (0, 0), pipeline_mode=pl.Buffered(1)),
        pl.BlockSpec((tm, n), lambda i: (i, 0)),
    ]
    args = [a, w, x]
    if final_norm:
        in_specs.append(pl.BlockSpec((1, n), lambda i: (0, 0)))
        args.append(g.reshape(1, n))
    return pl.pallas_call(
        functools.partial(_outproj_body, final_norm=final_norm),
        out_shape=jax.ShapeDtypeStruct((m, n), F32),
        grid=(m // tm,),
        in_specs=in_specs,
        out_specs=pl.BlockSpec((tm, n), lambda i: (i, 0)),
        compiler_params=_params(("parallel",)),
        name=name,
    )(*args)


def _conv_silu_chunk(x_ref, cw, n):
    cc = x_ref.shape[1]
    r0 = pl.multiple_of(n * CHUNK, CHUNK)
    cur = x_ref[pl.ds(r0, CHUNK), :]
    p0 = pl.multiple_of(jnp.maximum(r0 - 8, 0), 8)
    prev = jnp.where(n > 0, x_ref[pl.ds(p0, 8), :], 0.0)
    row8 = lax.broadcasted_iota(jnp.int32, (8, cc), 0)
    y = cur * cw[CONV_W - 1:CONV_W, :]
    for s in range(1, CONV_W):
        sh = pltpu.roll(cur, s, 0)
        top = jnp.where(row8 < s, pltpu.roll(prev, s, 0), sh[0:8, :])
        sh = jnp.concatenate([top, sh[8:, :]], axis=0)
        y = y + sh * cw[CONV_W - 1 - s:CONV_W - s, :]
    return _silu(y)


INV_BASE_SHIFT = 3
V_PER_QK = V_HEADS // QK_HEADS
HQ_PER_STEP = 2
DEC_GROUP = 8
DEC_SAMPLES = 2
PH1_CHUNKS = 2


def _split2(x):
    hi = x.astype(BF16)
    return hi, (x - hi.astype(F32)).astype(BF16)


def _split3(x):
    hi = x.astype(BF16)
    r = x - hi.astype(F32)
    mid = r.astype(BF16)
    return hi, mid, (r - mid.astype(F32)).astype(BF16)


def _dotb(a, b):
    return jnp.dot(a, b, preferred_element_type=F32)


N_LEVELS = CHUNK.bit_length() - 1 - INV_BASE_SHIFT


def _chunk_tables():
    row = lax.broadcasted_iota(jnp.int32, (CHUNK, CHUNK), 0)
    col = lax.broadcasted_iota(jnp.int32, (CHUNK, CHUNK), 1)
    same = [(row >> s) == (col >> s) for s in range(INV_BASE_SHIFT, CHUNK.bit_length())]
    tab32 = jnp.stack([row == col, same[0]]).astype(F32)
    levels = [jnp.logical_and(hi, jnp.logical_not(lo)) for lo, hi in zip(same[:-1], same[1:])]
    tab16 = jnp.stack([row >= col] + levels).astype(BF16)
    return tab32, tab16


def _inv_unit_lower(mats, tab32_ref, tab16_ref, out):
    n = range(len(mats))
    eye = tab32_ref[0]
    d = [mats[i] * tab32_ref[1] for i in n]
    db = [d[i].astype(BF16) for i in n]
    t = [eye - d[i] for i in n]
    ab = [mats[i].astype(BF16) for i in n]
    p = [_dotb(db[i], db[i]) for i in n]
    yield
    t = [t[i] + _dotb(t[i].astype(BF16), p[i].astype(BF16)) for i in n]
    p = [_dotb(p[i].astype(BF16), p[i].astype(BF16)) for i in n]
    yield
    t = [t[i] + _dotb(t[i].astype(BF16), p[i].astype(BF16)) for i in n]
    yield
    for lvl in range(N_LEVELS):
        tb = [t[i].astype(BF16) for i in n]
        x = [_dotb(tb[i], ab[i] * tab16_ref[1 + lvl]) for i in n]
        yield
        t = [t[i] - _dotb(x[i].astype(BF16), tb[i]) for i in n]
        yield
    out.extend(t)


def _trace_interleaved(*gens):
    live = list(gens)
    while live:
        for g in list(live):
            try:
                next(g)
            except StopIteration:
                live.remove(g)


def _gdn_prompt_body(q_ref, k_ref, v_ref, z_ref, ba_ref, cwq_ref, cwk_ref, cwv_ref, gp_ref, on_ref,
                     tab32_ref, tab16_ref, og_ref, s_ref, u_s, w_s, qd_s, at_s, kdt_s, gl_s, a_s, rhs_s):
    hq = pl.program_id(1)
    t_len = q_ref.shape[0]
    n_chunks = t_len // CHUNK
    row = lax.broadcasted_iota(jnp.int32, (CHUNK, CHUNK), 0)
    col = lax.broadcasted_iota(jnp.int32, (CHUNK, CHUNK), 1)
    causal = row >= col
    strict = row > col
    cwq = cwq_ref[...]
    cwk = cwk_ref[...]
    cwv = cwv_ref[...]
    neg_a = -jnp.exp(gp_ref[0:1, :])
    dt_bias = gp_ref[1:2, :]

    nv = HQ_PER_STEP * V_PER_QK
    n_iters = n_chunks // PH1_CHUNKS

    def chunk_prep(n, slot, base):
        r0 = pl.multiple_of(n * CHUNK, CHUNK)
        rows = pl.ds(r0, CHUNK)
        qc = _conv_silu_chunk(q_ref, cwq, n)
        yield
        kc = _conv_silu_chunk(k_ref, cwk, n)
        yield
        vc = _conv_silu_chunk(v_ref, cwv, n)
        yield
        ba = ba_ref[rows, :]
        beta_all = jax.nn.sigmoid(ba)
        xa = ba + dt_bias
        g_all = neg_a * (jnp.maximum(xa, 0.0) + jnp.log1p(jnp.exp(-jnp.abs(xa))))
        g_hi, g_mid, g_lo = _split3(g_all)
        tril = tab16_ref[0]
        gcum = _dotb(tril, g_hi) + (_dotb(tril, g_mid) + _dotb(tril, g_lo))
        gcum_t = gcum.T
        yield
        for q in range(HQ_PER_STEP):
            qh = qc[:, q * DK:(q + 1) * DK]
            kh = kc[:, q * DK:(q + 1) * DK]
            qn = qh * lax.rsqrt(jnp.sum(qh * qh, axis=-1, keepdims=True) + L2_EPS) * (DK ** -0.5)
            kn = kh * lax.rsqrt(jnp.sum(kh * kh, axis=-1, keepdims=True) + L2_EPS)
            kn_t = kn.T
            kn_tb = kn_t.astype(BF16)
            kk = _dotb(kn.astype(BF16), kn_tb)
            qk = _dotb(qn.astype(BF16), kn_tb)
            yield
            for j in range(V_PER_QK):
                h = q * V_PER_QK + j
                lane_b = (hq * HQ_PER_STEP + q) * V_PER_QK + j
                lane_a = V_HEADS + lane_b
                gcol = jnp.sum(jnp.where(col == lane_a, gcum, 0.0), axis=1, keepdims=True)
                bcol = jnp.sum(jnp.where(col == lane_b, beta_all, 0.0), axis=1, keepdims=True)
                grow = jnp.sum(jnp.where(row == lane_a, gcum_t, 0.0), axis=0, keepdims=True)
                glast = gcol[CHUNK - 1:CHUNK, :]
                decay = jnp.exp(jnp.where(causal, gcol - grow, -jnp.inf))
                a_mat = jnp.where(strict, kk * decay, 0.0) * bcol
                eg = jnp.exp(gcol)
                rhs = jnp.concatenate([vc[:, h * DV:(h + 1) * DV] * bcol, kn * (bcol * eg)], axis=1).astype(BF16)
                qd_s[h, rows, :] = (qn * eg).astype(BF16)
                at_s[h, rows, :] = (qk * decay).astype(BF16)
                kdt_s[h, rows, :] = (kn_t * jnp.exp(glast - grow)).astype(BF16)
                gl_s[h, pl.ds(n, 1), :] = jnp.broadcast_to(jnp.exp(glast), (1, DV))
                a_s[slot, base + h] = a_mat
                rhs_s[slot, base + h] = rhs
                yield

    def prep(it):
        it = jnp.asarray(it, jnp.int32)
        slot = it % 2
        for c in range(PH1_CHUNKS):
            yield from chunk_prep(PH1_CHUNKS * it + c, slot, c * nv)

    def solve(it):
        it = jnp.asarray(it, jnp.int32)
        slot = it % 2
        idx = range(PH1_CHUNKS * nv)
        t_mats = []
        yield from _inv_unit_lower([a_s[slot, i] for i in idx], tab32_ref, tab16_ref, t_mats)
        sols = [_dotb(t_mats[i].astype(BF16), rhs_s[slot, i]) for i in idx]
        yield
        for i in idx:
            rows = pl.ds(pl.multiple_of((PH1_CHUNKS * it + i // nv) * CHUNK, CHUNK), CHUNK)
            u_s[i % nv, rows, :] = sols[i][:, :DV]
            w_s[i % nv, rows, :] = sols[i][:, DV:].astype(BF16)
        yield

    gain = on_ref[...]
    heads = range(nv)

    def recur(it):
        it = jnp.asarray(it, jnp.int32)
        for c in range(PH1_CHUNKS):
            n = PH1_CHUNKS * it + c
            rows = pl.ds(pl.multiple_of(n * CHUNK, CHUNK), CHUNK)
            states = [s_ref[0, h] for h in heads]
            wq = [jnp.concatenate([w_s[h, rows, :], qd_s[h, rows, :]], axis=0) for h in heads]
            ws_qs = [_dotb(wq[h], states[h].astype(BF16)) for h in heads]
            yield
            v_new = [(u_s[h, rows, :] - ws_qs[h][:CHUNK]).astype(BF16) for h in heads]
            o = [ws_qs[h][CHUNK:] + _dotb(at_s[h, rows, :], v_new[h]) for h in heads]
            s_new = [states[h] * gl_s[h, pl.ds(n, 1), :] + _dotb(kdt_s[h, rows, :], v_new[h]) for h in heads]
            yield
            for h in heads:
                s_ref[0, h] = s_new[h]
                zg = z_ref[rows, h * DV:(h + 1) * DV]
                og = o[h] * lax.rsqrt(jnp.mean(o[h] * o[h], axis=-1, keepdims=True) + RMS_EPS) * gain * _silu(zg)
                og_ref[rows, h * DV:(h + 1) * DV] = og.astype(BF16)
            yield

    for h in heads:
        s_ref[0, h] = jnp.zeros((DK, DV), F32)
    _trace_interleaved(prep(0))
    _trace_interleaved(solve(0), prep(1))

    def steady(it, carry):
        _trace_interleaved(solve(it - 1), prep(it), recur(it - 2))
        return carry

    lax.fori_loop(2, n_iters, steady, 0)
    _trace_interleaved(solve(n_iters - 1), recur(n_iters - 2))
    _trace_interleaved(recur(n_iters - 1))


def _gdn_prompt(proj, ba, conv_w, gate_params, o_gain, *, batch, t_len):
    wq = HQ_PER_STEP * DK
    wv = HQ_PER_STEP * V_PER_QK * DV
    nv = HQ_PER_STEP * V_PER_QK
    in_specs = [
        pl.BlockSpec((t_len, wq), lambda b, h: (b, h)),
        pl.BlockSpec((t_len, wq), lambda b, h: (b, KEY_DIM // wq + h)),
        pl.BlockSpec((t_len, wv), lambda b, h: (b, (2 * KEY_DIM) // wv + h)),
        pl.BlockSpec((t_len, wv), lambda b, h: (b, CONV_DIM // wv + h)),
        pl.BlockSpec((t_len, 128), lambda b, h: (b, 0)),
        pl.BlockSpec((CONV_W, wq), lambda b, h: (0, h)),
        pl.BlockSpec((CONV_W, wq), lambda b, h: (0, KEY_DIM // wq + h)),
        pl.BlockSpec((CONV_W, wv), lambda b, h: (0, (2 * KEY_DIM) // wv + h)),
        pl.BlockSpec((2, 128), lambda b, h: (0, 0)),
        pl.BlockSpec((1, DV), lambda b, h: (0, 0)),
        pl.BlockSpec((2, CHUNK, CHUNK), lambda b, h: (0, 0, 0)),
        pl.BlockSpec((1 + N_LEVELS, CHUNK, CHUNK), lambda b, h: (0, 0, 0)),
    ]
    tab32, tab16 = _chunk_tables()
    out_shape = [
        jax.ShapeDtypeStruct((batch * t_len, VAL_DIM), BF16),
        jax.ShapeDtypeStruct((batch, V_HEADS, DK, DV), F32),
    ]
    out_specs = [
        pl.BlockSpec((t_len, wv), lambda b, h: (b, h)),
        pl.BlockSpec((1, nv, DK, DV), lambda b, h: (b, h, 0, 0)),
    ]
    scratch = [
        pltpu.VMEM((nv, t_len, DV), F32),
        pltpu.VMEM((nv, t_len, DK), BF16),
        pltpu.VMEM((nv, t_len, DK), BF16),
        pltpu.VMEM((nv, t_len, CHUNK), BF16),
        pltpu.VMEM((nv, t_len, CHUNK), BF16),
        pltpu.VMEM((nv, t_len // CHUNK, DV), F32),
        pltpu.VMEM((2, PH1_CHUNKS * nv, CHUNK, CHUNK), F32),
        pltpu.VMEM((2, PH1_CHUNKS * nv, CHUNK, DV + DK), BF16),
    ]
    return pl.pallas_call(
        _gdn_prompt_body,
        out_shape=out_shape,
        grid=(batch, QK_HEADS // HQ_PER_STEP),
        in_specs=in_specs,
        out_specs=out_specs,
        scratch_shapes=scratch,
        compiler_params=_params(("parallel", "arbitrary")),
        name="gdn_prompt",
    )(proj, proj, proj, proj, ba, conv_w, conv_w, conv_w, gate_params, o_gain.reshape(1, DV), tab32, tab16)


def _gdn_gates_body(ba_ref, gp_ref, beta_ref, eg_ref):
    ba = ba_ref[...]
    beta_ref[...] = jax.nn.sigmoid(ba)
    xa = ba + gp_ref[1:2, :]
    g = -jnp.exp(gp_ref[0:1, :]) * (jnp.maximum(xa, 0.0) + jnp.log1p(jnp.exp(-jnp.abs(xa))))
    eg_ref[...] = jnp.exp(g)


def _gdn_gates(ba, gate_params):
    m = ba.shape[0]
    return pl.pallas_call(
        _gdn_gates_body,
        out_shape=[jax.ShapeDtypeStruct((m, 128), F32)] * 2,
        name="gdn_gates",
    )(ba, gate_params)


def _gdn_decode_body(beta_ref, eg_ref, xs_ref, cw_ref, z_ref, on_ref, s_ref, so_ref, og_ref):
    for si in range(DEC_SAMPLES):
        b = pl.program_id(0) * DEC_SAMPLES + si
        y = xs_ref[si, 0] * cw_ref[0]
        for j in range(1, CONV_W):
            y = y + xs_ref[si, j] * cw_ref[j]
        y = _silu(y)
        q = y[0:QK_HEADS]
        k = y[QK_HEADS:2 * QK_HEADS]
        v = y[2 * QK_HEADS:]
        qn = q * lax.rsqrt(jnp.sum(q * q, axis=-1, keepdims=True) + L2_EPS) * (DK ** -0.5)
        kn = k * lax.rsqrt(jnp.sum(k * k, axis=-1, keepdims=True) + L2_EPS)
        qk_t = jnp.concatenate([qn, kn, jnp.zeros((DK - 2 * QK_HEADS, DK), F32)], axis=0).T
        for g0 in range(0, V_HEADS, DEC_GROUP):
            hs = range(g0, g0 + DEC_GROUP)
            qcol = {h: qk_t[:, h // V_PER_QK:h // V_PER_QK + 1] for h in hs}
            kcol = {h: qk_t[:, QK_HEADS + h // V_PER_QK:QK_HEADS + h // V_PER_QK + 1] for h in hs}
            sd = {h: s_ref[si, h] * eg_ref[b, h] for h in hs}
            ks = {h: jnp.sum(sd[h] * kcol[h], axis=0, keepdims=True) for h in hs}
            v_new = {h: beta_ref[b, h] * (v[h:h + 1, :] - ks[h]) for h in hs}
            sn = {h: sd[h] + kcol[h] * v_new[h] for h in hs}
            for h in hs:
                so_ref[si, h] = sn[h]
            o = {h: jnp.sum(sn[h] * qcol[h], axis=0, keepdims=True) for h in hs}
            for h in hs:
                og_ref[si, h:h + 1, :] = o[h]
        o_all = og_ref[si]
        og_ref[si] = (o_all * lax.rsqrt(jnp.mean(o_all * o_all, axis=-1, keepdims=True) + RMS_EPS)
                      * on_ref[...] * _silu(z_ref[si]))


def _gdn_decode(beta, eg, xs, conv_w3, z3, o_gain, state):
    nb = xs.shape[0]
    slots = CONV_DIM // DK
    smem = pl.BlockSpec(memory_space=pltpu.SMEM)
    return pl.pallas_call(
        _gdn_decode_body,
        out_shape=[
            jax.ShapeDtypeStruct((nb, V_HEADS, DK, DV), F32),
            jax.ShapeDtypeStruct((nb, V_HEADS, DV), F32),
        ],
        grid=(nb // DEC_SAMPLES,),
        in_specs=[
            smem,
            smem,
            pl.BlockSpec((DEC_SAMPLES, CONV_W, slots, DK), lambda b: (b, 0, 0, 0)),
            pl.BlockSpec((CONV_W, slots, DK), lambda b: (0, 0, 0)),
            pl.BlockSpec((DEC_SAMPLES, V_HEADS, DV), lambda b: (b, 0, 0)),
            pl.BlockSpec((1, DV), lambda b: (0, 0)),
            pl.BlockSpec((DEC_SAMPLES, V_HEADS, DK, DV), lambda b: (b, 0, 0, 0)),
        ],
        out_specs=[
            pl.BlockSpec((DEC_SAMPLES, V_HEADS, DK, DV), lambda b: (b, 0, 0, 0)),
            pl.BlockSpec((DEC_SAMPLES, V_HEADS, DV), lambda b: (b, 0, 0)),
        ],
        compiler_params=_params(("parallel",)),
        name="gdn_decode",
    )(beta, eg, xs, conv_w3, z3, o_gain.reshape(1, DV), state)


def _ssm_pow_body(lr_ref, li_ref, ldt_ref, pr_ref, pi_ref):
    n = jnp.where(lax.broadcasted_iota(jnp.int32, pr_ref.shape, 0) == 0, 1.0, float(SCAN_SEG))
    dt = jnp.exp(ldt_ref[...])
    lr = jnp.minimum(lr_ref[...], -1e-4) * dt
    li = li_ref[...] * dt
    mag = jnp.exp(n * lr)
    pr_ref[...] = mag * jnp.cos(n * li)
    pi_ref[...] = mag * jnp.sin(n * li)


def _ssm_pow(lr, li, ldt):
    gp = lr.shape[1]
    tl = 2048
    spec = pl.BlockSpec((1, tl), lambda i: (0, i))
    ospec = pl.BlockSpec((POW_ROWS, tl), lambda i: (0, i))
    return pl.pallas_call(
        _ssm_pow_body,
        out_shape=[jax.ShapeDtypeStruct((POW_ROWS, gp), F32)] * 2,
        grid=(gp // tl,),
        in_specs=[spec, spec, spec],
        out_specs=[ospec, ospec],
        name="ssm_pow",
    )(lr, li, ldt)


def _ssm_blocks_body(lr_ref, li_ref, ldt_ref, br_ref, bi_ref, cr_ref, ci_ref, bblk_ref, cblk_ref):
    dt = jnp.exp(ldt_ref[...])
    lr = jnp.minimum(lr_ref[...], -1e-4)
    li = li_ref[...]
    mag = jnp.exp(lr * dt)
    xr = mag * jnp.cos(li * dt) - 1.0
    xi = mag * jnp.sin(li * dt)
    den = 1.0 / (lr * lr + li * li)
    fr = (xr * lr + xi * li) * den
    fi = (xi * lr - xr * li) * den
    bblk_ref[...] = jnp.zeros_like(bblk_ref)
    cblk_ref[...] = jnp.zeros_like(cblk_ref)
    per_tile = 128 // SSM_STATE
    for g in range(GROUPS_PER_STEP):
        rows = slice(g * SSM_GROUP, (g + 1) * SSM_GROUP)
        off = (g // per_tile) * 256 + (g % per_tile) * SSM_STATE
        cols = slice(off, off + SSM_STATE)
        cols_im = slice(off + 128, off + 128 + SSM_STATE)
        br = br_ref[g]
        bi = bi_ref[g]
        frg = fr[g:g + 1, :]
        fig = fi[g:g + 1, :]
        bblk_ref[rows, cols] = (frg * br - fig * bi).astype(BF16)
        bblk_ref[rows, cols_im] = (frg * bi + fig * br).astype(BF16)
        cblk_ref[rows, cols] = cr_ref[g].astype(BF16)
        cblk_ref[rows, cols_im] = (-ci_ref[g]).astype(BF16)


def _ssm_blocks(lr, li, ldt, bt_re, bt_im, c_re, c_im):
    g, p = lr.shape
    k = GROUPS_PER_STEP
    s2 = pl.BlockSpec((k, p), lambda i: (i, 0))
    s1 = pl.BlockSpec((k, 1), lambda i: (i, 0))
    s3 = pl.BlockSpec((k, SSM_GROUP, p), lambda i: (i, 0, 0))
    blk = pl.BlockSpec((None, CH_PER_STEP, 2 * ST_PER_STEP), lambda i: (i, 0, 0))
    return pl.pallas_call(
        _ssm_blocks_body,
        out_shape=[jax.ShapeDtypeStruct((g // k, CH_PER_STEP, 2 * ST_PER_STEP), BF16)] * 2,
        grid=(g // k,),
        in_specs=[s2, s2, s1, s3, s3, s3, s3],
        out_specs=[blk, blk],
        name="ssm_blocks",
    )(lr, li, ldt, bt_re, bt_im, c_re, c_im)


def _dot_nt(a, b):
    return lax.dot_general(a, b, (((1,), (1,)), ((), ())), preferred_element_type=F32)


def _gelu_tanh(x):
    return 0.5 * x * (1.0 + jnp.tanh(math.sqrt(2.0 / math.pi) * (x + 0.044715 * (x * x * x))))


def _ssm_scan_body(ua_ref, ub_ref, b_ref, c_ref, pr_ref, pi_ref, d_ref, y_ref, hr_ref, hi_ref,
                   h_s, y_s, cr_s, ci_s):
    tb = pl.program_id(2)
    nt = ST_PER_STEP // 128
    u_perm = jnp.concatenate(
        [jnp.concatenate([ua_ref[pl.ds(i, 8, stride=SCAN_SEG), :], ub_ref[pl.ds(i, 8, stride=SCAN_SEG), :]], axis=1)
         for i in range(SCAN_SEG)], axis=0).astype(BF16)

    @pl.when(tb == 0)
    def _():
        cr_s[...] = jnp.zeros_like(cr_s)
        ci_s[...] = jnp.zeros_like(ci_s)

    def b_proj(g):
        for c in range(g * SCAN_TILES, (g + 1) * SCAN_TILES):
            bu = jnp.dot(u_perm, b_ref[:, c * 256:(c + 1) * 256], preferred_element_type=F32)
            h_s[c] = bu[:, :128]
            h_s[nt + c] = bu[:, 128:]

    row8 = lax.broadcasted_iota(jnp.int32, (8, 128), 0)

    def tile_scan(c):
        lanes = slice(c * 128, (c + 1) * 128)
        ar = jnp.broadcast_to(pr_ref[0:1, lanes], (8, 128))
        ai = jnp.broadcast_to(pi_ref[0:1, lanes], (8, 128))
        hr = jnp.zeros((8, 128), F32)
        hi = jnp.zeros((8, 128), F32)
        for i in range(SCAN_SEG):
            xr = h_s[c, i * 8:(i + 1) * 8, :]
            xi = h_s[nt + c, i * 8:(i + 1) * 8, :]
            hr, hi = ar * hr - ai * hi + xr, ar * hi + ai * hr + xi
            if i % SCAN_YIELD == SCAN_YIELD - 1:
                yield

        asr = pr_ref[1:2, lanes]
        asi = pi_ref[1:2, lanes]
        c_r = cr_s[:, lanes]
        c_i = ci_s[:, lanes]
        cin_r = jnp.zeros((8, 128), F32)
        cin_i = jnp.zeros((8, 128), F32)
        for s in range(8):
            cin_r = jnp.where(row8 == s, c_r, cin_r)
            cin_i = jnp.where(row8 == s, c_i, cin_i)
            e_r = hr[s:s + 1, :]
            e_i = hi[s:s + 1, :]
            c_r, c_i = e_r + asr * c_r - asi * c_i, e_i + asr * c_i + asi * c_r
        cr_s[:, lanes] = c_r
        ci_s[:, lanes] = c_i
        yield

        hr, hi = cin_r, cin_i
        for i in range(SCAN_SEG):
            xr = h_s[c, i * 8:(i + 1) * 8, :]
            xi = h_s[nt + c, i * 8:(i + 1) * 8, :]
            hr, hi = ar * hr - ai * hi + xr, ar * hi + ai * hr + xi
            h_s[c, i * 8:(i + 1) * 8, :] = hr
            h_s[nt + c, i * 8:(i + 1) * 8, :] = hi
            if i % SCAN_YIELD == SCAN_YIELD - 1:
                yield

    n_groups = nt // SCAN_TILES
    b_proj(0)
    y = None
    for g in range(n_groups):
        if g + 1 < n_groups:
            b_proj(g + 1)
        tiles = range(g * SCAN_TILES, (g + 1) * SCAN_TILES)
        _trace_interleaved(*[tile_scan(c) for c in tiles])
        for c in tiles:
            h_cat = jnp.concatenate([h_s[c], h_s[nt + c]], axis=1).astype(BF16)
            y_c = _dot_nt(h_cat, c_ref[:, c * 256:(c + 1) * 256])
            y = y_c if y is None else y + y_c

    y_s[0] = y[:, :128]
    y_s[1] = y[:, 128:]
    per_seg = SCAN_SEG // 8
    y_nat = jnp.concatenate(
        [jnp.concatenate([y_s[h, pl.ds((j % per_seg) * 64 + j // per_seg, 8, stride=8), :] for h in range(2)], axis=1)
         for j in range(SCAN_BLOCK // 8)], axis=0)
    u = jnp.concatenate([ua_ref[...], ub_ref[...]], axis=1)
    y_ref[...] = _gelu_tanh(y_nat + d_ref[...] * u)

    @pl.when(tb == pl.num_programs(2) - 1)
    def _():
        hr_ref[...] = cr_s[...]
        hi_ref[...] = ci_s[...]


def _ssm_scan(uz, bblk, cblk, pw_r, pw_i, d_skip, *, batch, t_len):
    nt = t_len // SCAN_BLOCK
    return pl.pallas_call(
        _ssm_scan_body,
        out_shape=[
            jax.ShapeDtypeStruct((batch * t_len, SSM_WIDTH), F32),
            jax.ShapeDtypeStruct((batch, 1, SSM_GROUPS * SSM_STATE), F32),
            jax.ShapeDtypeStruct((batch, 1, SSM_GROUPS * SSM_STATE), F32),
        ],
        grid=(batch, N_GSTEPS, nt),
        in_specs=[
            pl.BlockSpec((SCAN_BLOCK, 128), lambda b, g, t: (b * nt + t, 2 * g)),
            pl.BlockSpec((SCAN_BLOCK, 128), lambda b, g, t: (b * nt + t, 2 * g + 1)),
            pl.BlockSpec((None, CH_PER_STEP, 2 * ST_PER_STEP), lambda b, g, t: (g, 0, 0)),
            pl.BlockSpec((None, CH_PER_STEP, 2 * ST_PER_STEP), lambda b, g, t: (g, 0, 0)),
            pl.BlockSpec((POW_ROWS, ST_PER_STEP), lambda b, g, t: (0, g)),
            pl.BlockSpec((POW_ROWS, ST_PER_STEP), lambda b, g, t: (0, g)),
            pl.BlockSpec((1, CH_PER_STEP), lambda b, g, t: (0, g)),
        ],
        out_specs=[
            pl.BlockSpec((SCAN_BLOCK, CH_PER_STEP), lambda b, g, t: (b * nt + t, g)),
            pl.BlockSpec((None, 1, ST_PER_STEP), lambda b, g, t: (b, 0, g)),
            pl.BlockSpec((None, 1, ST_PER_STEP), lambda b, g, t: (b, 0, g)),
        ],
        scratch_shapes=[
            pltpu.VMEM((2 * ST_PER_STEP // 128, SCAN_BLOCK, 128), F32),
            pltpu.VMEM((CH_PER_STEP // 128, SCAN_BLOCK, 128), F32),
            pltpu.VMEM((1, ST_PER_STEP), F32),
            pltpu.VMEM((1, ST_PER_STEP), F32),
        ],
        compiler_params=_params(("parallel", "parallel", "arbitrary")),
        name="ssm_scan",
    )(uz, uz, bblk, cblk, pw_r, pw_i, d_skip.reshape(1, SSM_WIDTH))


def _ssm_decode_body(u_ref, b_ref, c_ref, pr_ref, pi_ref, d_ref, h0r_ref, h0i_ref, y_ref, hr_ref, hi_ref):
    nt = ST_PER_STEP // 128
    u = u_ref[...]
    bu = jnp.dot(u.astype(BF16), b_ref[...], preferred_element_type=F32)
    h_cat = []
    for c in range(nt):
        lanes = slice(c * 128, (c + 1) * 128)
        ar = pr_ref[0:1, lanes]
        ai = pi_ref[0:1, lanes]
        h0r = h0r_ref[:, lanes]
        h0i = h0i_ref[:, lanes]
        hr = bu[:, c * 256:c * 256 + 128] + ar * h0r - ai * h0i
        hi = bu[:, c * 256 + 128:(c + 1) * 256] + ar * h0i + ai * h0r
        hr_ref[:, lanes] = hr
        hi_ref[:, lanes] = hi
        h_cat += [hr, hi]
    y = _dot_nt(jnp.concatenate(h_cat, axis=1).astype(BF16), c_ref[...])
    y_ref[...] = _gelu_tanh(y + d_ref[...] * u)


def _ssm_decode(uz, bblk, cblk, pw_r, pw_i, d_skip, h0r, h0i):
    nb = uz.shape[0]
    return pl.pallas_call(
        _ssm_decode_body,
        out_shape=[
            jax.ShapeDtypeStruct((nb, SSM_WIDTH), F32),
            jax.ShapeDtypeStruct((nb, SSM_GROUPS * SSM_STATE), F32),
            jax.ShapeDtypeStruct((nb, SSM_GROUPS * SSM_STATE), F32),
        ],
        grid=(N_GSTEPS,),
        in_specs=[
            pl.BlockSpec((nb, CH_PER_STEP), lambda g: (0, g)),
            pl.BlockSpec((None, CH_PER_STEP, 2 * ST_PER_STEP), lambda g: (g, 0, 0)),
            pl.BlockSpec((None, CH_PER_STEP, 2 * ST_PER_STEP), lambda g: (g, 0, 0)),
            pl.BlockSpec((POW_ROWS, ST_PER_STEP), lambda g: (0, g)),
            pl.BlockSpec((POW_ROWS, ST_PER_STEP), lambda g: (0, g)),
            pl.BlockSpec((1, CH_PER_STEP), lambda g: (0, g)),
            pl.BlockSpec((nb, ST_PER_STEP), lambda g: (0, g)),
            pl.BlockSpec((nb, ST_PER_STEP), lambda g: (0, g)),
        ],
        out_specs=[
            pl.BlockSpec((nb, CH_PER_STEP), lambda g: (0, g)),
            pl.BlockSpec((nb, ST_PER_STEP), lambda g: (0, g)),
            pl.BlockSpec((nb, ST_PER_STEP), lambda g: (0, g)),
        ],
        compiler_params=_params(("parallel",)),
        name="ssm_decode",
    )(uz, bblk, cblk, pw_r, pw_i, d_skip.reshape(1, SSM_WIDTH), h0r, h0i)


GLU_SLAB = 256


def _glu_body(y_ref, w_ref, b_ref, z_ref, o_ref, yb_scr, *, tn):
    j = pl.program_id(1)

    @pl.when(j == 0)
    def _():
        yb_scr[...] = y_ref[...].astype(BF16)

    for c in range(tn // GLU_SLAB):
        cols = slice(c * GLU_SLAB, (c + 1) * GLU_SLAB)
        t = jnp.dot(yb_scr[...], w_ref[:, cols], preferred_element_type=F32) + b_ref[:, cols]
        y = y_ref[:, pl.ds(pl.multiple_of(j * tn + c * GLU_SLAB, GLU_SLAB), GLU_SLAB)]
        o_ref[:, cols] = (y * jax.nn.sigmoid(t) * _silu(z_ref[:, cols])).astype(BF16)


def _glu(y, w, b, uz, *, tm, tn):
    m, d = y.shape
    zoff = SSM_WIDTH // tn
    return pl.pallas_call(
        functools.partial(_glu_body, tn=tn),
        out_shape=jax.ShapeDtypeStruct((m, d), BF16),
        grid=(m // tm, d // tn),
        in_specs=[
            pl.BlockSpec((tm, d), lambda i, j: (i, 0)),
            pl.BlockSpec((d, tn), lambda i, j: (0, j)),
            pl.BlockSpec((1, tn), lambda i, j: (0, j)),
            pl.BlockSpec((tm, tn), lambda i, j: (i, zoff + j)),
        ],
        out_specs=pl.BlockSpec((tm, tn), lambda i, j: (i, j)),
        scratch_shapes=[pltpu.VMEM((tm, d), BF16)],
        compiler_params=_params(("parallel", "arbitrary")),
        name="ssm_glu",
    )(y, w, b.reshape(1, d), uz)


def _prep_weights(norm_gdn, w_in_gdn, conv_gdn, a_log_gdn, dt_bias_gdn, onorm_gdn, w_out_gdn,
                  norm_ssm, w_in_ssm, lam_re, lam_im, b_re, b_im, c_re, c_im, d_ssm, log_dt_ssm,
                  w_glu_ssm, b_glu_ssm, w_out_ssm, norm_final):
    w_in_t = jnp.swapaxes(w_in_gdn[0], 0, 1)
    w_ba = jnp.pad(w_in_t[MAIN_DIM:], ((0, 128 - 2 * V_HEADS), (0, 0))).astype(BF16)
    zeros = jnp.zeros((V_HEADS,), F32)
    pad = jnp.zeros((128 - 2 * V_HEADS,), F32)
    gate_params = jnp.stack([jnp.concatenate([zeros, a_log_gdn[0], pad]),
                             jnp.concatenate([zeros, dt_bias_gdn[0], pad])])

    lr, li, ldt = lam_re[0], lam_im[0], log_dt_ssm[0]
    gp = SSM_GROUPS * SSM_STATE
    pw_r, pw_i = _ssm_pow(lr.reshape(1, gp), li.reshape(1, gp),
                          jnp.repeat(ldt, SSM_STATE).reshape(1, gp))
    bblk, cblk = _ssm_blocks(lr, li, ldt.reshape(SSM_GROUPS, 1),
                                 jnp.swapaxes(b_re[0], 1, 2), jnp.swapaxes(b_im[0], 1, 2), c_re[0], c_im[0])
    return dict(
        norm_gdn=norm_gdn[0], w_main=w_in_t, w_ba=w_ba, conv_w=conv_gdn[0], gate_params=gate_params,
        o_gain=onorm_gdn[0], w_out_gdn=w_out_gdn[0].astype(BF16),
        norm_ssm=norm_ssm[0], w_in_ssm=w_in_ssm[0], pw_r=pw_r, pw_i=pw_i, bblk=bblk, cblk=cblk,
        d_skip=d_ssm[0], w_glu=w_glu_ssm[0].astype(BF16), b_glu=b_glu_ssm[0], w_out_ssm=w_out_ssm[0].astype(BF16),
        norm_final=norm_final)


def _ssm_tail(p, x1, y, uz, *, tm, tm_glu):
    y3 = _glu(y, p["w_glu"], p["b_glu"], uz, tm=tm_glu, tn=1024)
    return _outproj(y3, p["w_out_ssm"], x1, p["norm_final"], tm=tm, name="ssm_out")


def _prompt_path(p, x_prompt):
    batch, t_len, d = x_prompt.shape
    x = x_prompt.reshape(batch * t_len, d)
    proj, ba = _norm_matmul(x, p["norm_gdn"], p["w_main"], p["w_ba"], n=MAIN_DIM, tm=2048, tn=512, name="gdn_in",
                            w_transposed=True)
    og, delta = _gdn_prompt(proj, ba, p["conv_w"], p["gate_params"], p["o_gain"], batch=batch, t_len=t_len)
    conv_state = proj.reshape(batch, t_len, MAIN_DIM)[:, t_len - (CONV_W - 1):, :CONV_DIM]
    x1 = _outproj(og, p["w_out_gdn"], x, tm=512, name="gdn_out")
    uz = _norm_matmul(x1, p["norm_ssm"], p["w_in_ssm"], n=2 * SSM_WIDTH, tm=2048, tn=512, name="ssm_in")
    y, h_re, h_im = _ssm_scan(uz, p["bblk"], p["cblk"], p["pw_r"], p["pw_i"], p["d_skip"],
                              batch=batch, t_len=t_len)
    out = _ssm_tail(p, x1, y, uz, tm=512, tm_glu=512)
    return (out.reshape(batch, t_len, d), conv_state[None], delta[None],
            h_re.reshape(1, batch, SSM_GROUPS, SSM_STATE), h_im.reshape(1, batch, SSM_GROUPS, SSM_STATE))


def _sample_path(p, x_sample, state_conv, state_delta, state_re, state_im):
    nb, _, d = x_sample.shape
    x = x_sample.reshape(nb, d)
    proj, ba = _norm_matmul(x, p["norm_gdn"], p["w_main"], p["w_ba"], n=MAIN_DIM, tm=nb, tn=512, name="gdn_in_s",
                            w_transposed=True)
    beta, eg = _gdn_gates(ba, p["gate_params"])
    xs = jnp.concatenate([state_conv[0], proj[:, None, :CONV_DIM]], axis=1)
    slots = CONV_DIM // DK
    delta, og = _gdn_decode(beta[:, :V_HEADS], eg[:, V_HEADS:2 * V_HEADS],
                            xs.reshape(nb, CONV_W, slots, DK), p["conv_w"].reshape(CONV_W, slots, DK),
                            proj[:, CONV_DIM:].reshape(nb, V_HEADS, DV), p["o_gain"], state_delta[0])
    x1 = _outproj(og.reshape(nb, VAL_DIM).astype(BF16), p["w_out_gdn"], x, tm=nb, name="gdn_out_s")
    uz = _norm_matmul(x1, p["norm_ssm"], p["w_in_ssm"], n=2 * SSM_WIDTH, tm=nb, tn=512, name="ssm_in_s")
    gp = SSM_GROUPS * SSM_STATE
    y, h_re, h_im = _ssm_decode(uz, p["bblk"], p["cblk"], p["pw_r"], p["pw_i"], p["d_skip"],
                                state_re[0].reshape(nb, gp), state_im[0].reshape(nb, gp))
    out = _ssm_tail(p, x1, y, uz, tm=nb, tm_glu=nb)
    return (out.reshape(nb, 1, d), xs[:, 1:][None], delta[None],
            h_re.reshape(1, nb, SSM_GROUPS, SSM_STATE), h_im.reshape(1, nb, SSM_GROUPS, SSM_STATE))


def kernel(x_prompt, x_sample, state_gdn_conv, state_gdn_delta, state_ssm_re, state_ssm_im, norm_gdn, w_in_gdn, conv_gdn, a_log_gdn, dt_bias_gdn, onorm_gdn, w_out_gdn, norm_ssm, w_in_ssm, lam_re, lam_im, b_re, b_im, c_re, c_im, d_ssm, log_dt_ssm, w_glu_ssm, b_glu_ssm, w_out_ssm, norm_final):
    p = _prep_weights(norm_gdn, w_in_gdn, conv_gdn, a_log_gdn, dt_bias_gdn, onorm_gdn, w_out_gdn,
                      norm_ssm, w_in_ssm, lam_re, lam_im, b_re, b_im, c_re, c_im, d_ssm, log_dt_ssm,
                      w_glu_ssm, b_glu_ssm, w_out_ssm, norm_final)
    y_p, conv_p, delta_p, re_p, im_p = _prompt_path(p, x_prompt)
    y_s, conv_s, delta_s, re_s, im_s = _sample_path(p, x_sample, state_gdn_conv, state_gdn_delta,
                                                    state_ssm_re, state_ssm_im)
    return (y_p, y_s, conv_p, delta_p, re_p, im_p, conv_s, delta_s, re_s, im_s)
```

```python
import functools
import math

import jax
import jax.numpy as jnp
from jax import lax
from jax.experimental import pallas as pl
from jax.experimental.pallas import tpu as pltpu

F32 = jnp.float32
BF16 = jnp.bfloat16

RMS_EPS = 1e-6
L2_EPS = 1e-6

D_MODEL = 2048
QK_HEADS = 16
V_HEADS = 32
DK = 128
DV = 128
KEY_DIM = QK_HEADS * DK
VAL_DIM = V_HEADS * DV
CONV_DIM = 2 * KEY_DIM + VAL_DIM
CONV_W = 4
MAIN_DIM = CONV_DIM + VAL_DIM

SSM_WIDTH = 4096
SSM_GROUP = 16
SSM_GROUPS = SSM_WIDTH // SSM_GROUP
SSM_STATE = 64
GROUPS_PER_STEP = 16
CH_PER_STEP = GROUPS_PER_STEP * SSM_GROUP
ST_PER_STEP = GROUPS_PER_STEP * SSM_STATE
N_GSTEPS = SSM_GROUPS // GROUPS_PER_STEP

CHUNK = 128
SCAN_BLOCK = 2048
SCAN_SEG = SCAN_BLOCK // 8
POW_ROWS = 8
SCAN_TILES = 1
SCAN_YIELD = 4

VMEM_LIMIT = 56 * 1024 * 1024


def _params(sem):
    return pltpu.CompilerParams(dimension_semantics=sem, vmem_limit_bytes=VMEM_LIMIT)


def _mm(a, b):
    return jnp.dot(a.astype(BF16), b.astype(BF16), preferred_element_type=F32)


def _mm_hi(a, b):
    return jnp.dot(a, b, precision=lax.Precision.HIGHEST, preferred_element_type=F32)


def _silu(x):
    return x * jax.nn.sigmoid(x)


def _rms(x, g):
    ms = jnp.mean(x * x, axis=-1, keepdims=True)
    return x * lax.rsqrt(ms + RMS_EPS) * g


def _norm_matmul_body(x_ref, g_ref, w_ref, *rest, has_aux, w_transposed):
    if has_aux:
        waux_ref, o_ref, aux_ref, h_scr = rest
    else:
        o_ref, h_scr = rest
    mm = _dot_nt if w_transposed else _dotb

    @pl.when(pl.program_id(1) == 0)
    def _():
        h = _rms(x_ref[...], g_ref[...]).astype(BF16)
        h_scr[...] = h
        if has_aux:
            aux_ref[...] = mm(h, waux_ref[...])

    o_ref[...] = mm(h_scr[...], w_ref[...].astype(BF16))


def _norm_matmul(x, g, w, w_aux=None, *, n, tm, tn, name, w_transposed=False):
    m, d = x.shape
    has_aux = w_aux is not None
    in_specs = [
        pl.BlockSpec((tm, d), lambda i, j: (i, 0), pipeline_mode=pl.Buffered(1)),
        pl.BlockSpec((1, d), lambda i, j: (0, 0)),
        pl.BlockSpec((tn, d), lambda i, j: (j, 0)) if w_transposed else pl.BlockSpec((d, tn), lambda i, j: (0, j)),
    ]
    out_shape = [jax.ShapeDtypeStruct((m, n), F32)]
    out_specs = [pl.BlockSpec((tm, tn), lambda i, j: (i, j))]
    args = [x, g.reshape(1, d), w]
    if has_aux:
        na = w_aux.shape[0] if w_transposed else w_aux.shape[1]
        in_specs.append(pl.BlockSpec(w_aux.shape, lambda i, j: (0, 0)))
        out_shape.append(jax.ShapeDtypeStruct((m, na), F32))
        out_specs.append(pl.BlockSpec((tm, na), lambda i, j: (i, 0)))
        args.append(w_aux)
    res = pl.pallas_call(
        functools.partial(_norm_matmul_body, has_aux=has_aux, w_transposed=w_transposed),
        out_shape=out_shape,
        grid=(m // tm, n // tn),
        in_specs=in_specs,
        out_specs=out_specs,
        scratch_shapes=[pltpu.VMEM((tm, d), BF16)],
        compiler_params=_params(("parallel", "arbitrary")),
        name=name,
    )(*args)
    return res if has_aux else res[0]


def _outproj_body(a_ref, w_ref, x_ref, *rest, final_norm):
    if final_norm:
        g_ref, o_ref = rest
    else:
        (o_ref,) = rest
    y = x_ref[...] + jnp.dot(a_ref[...], w_ref[...], preferred_element_type=F32)
    o_ref[...] = _rms(y, g_ref[...]) if final_norm else y


def _outproj(a, w, x, g=None, *, tm, name):
    m, kd = a.shape
    n = w.shape[1]
    final_norm = g is not None
    in_specs = [
        pl.BlockSpec((tm, kd), lambda i: (i, 0)),
        pl.BlockSpec((kd, n), lambda i: (0, 0), pipeline_mode=pl.Buffered(1)),
        pl.BlockSpec((tm, n), lambda i: (i, 0)),
    ]
    args = [a, w, x]
    if final_norm:
        in_specs.append(pl.BlockSpec((1, n), lambda i: (0, 0)))
        args.append(g.reshape(1, n))
    return pl.pallas_call(
        functools.partial(_outproj_body, final_norm=final_norm),
        out_shape=jax.ShapeDtypeStruct((m, n), F32),
        grid=(m // tm,),
        in_specs=in_specs,
        out_specs=pl.BlockSpec((tm, n), lambda i: (i, 0)),
        compiler_params=_params(("parallel",)),
        name=name,
    )(*args)


def _conv_silu_chunk(x_ref, cw, n):
    cc = x_ref.shape[1]
    r0 = pl.multiple_of(n * CHUNK, CHUNK)
    cur = x_ref[pl.ds(r0, CHUNK), :]
    p0 = pl.multiple_of(jnp.maximum(r0 - 8, 0), 8)
    prev = jnp.where(n > 0, x_ref[pl.ds(p0, 8), :], 0.0)
    row8 = lax.broadcasted_iota(jnp.int32, (8, cc), 0)
    y = cur * cw[CONV_W - 1:CONV_W, :]
    for s in range(1, CONV_W):
        sh = pltpu.roll(cur, s, 0)
        top = jnp.where(row8 < s, pltpu.roll(prev, s, 0), sh[0:8, :])
        sh = jnp.concatenate([top, sh[8:, :]], axis=0)
        y = y + sh * cw[CONV_W - 1 - s:CONV_W - s, :]
    return _silu(y)


INV_BASE_SHIFT = 3
V_PER_QK = V_HEADS // QK_HEADS
HQ_PER_STEP = 2
DEC_GROUP = 8
DEC_SAMPLES = 4
PH1_CHUNKS = 2
PIPE_STEPS = (1, 2, 1)


def _split2(x):
    hi = x.astype(BF16)
    return hi, (x - hi.astype(F32)).astype(BF16)


def _split3(x):
    hi = x.astype(BF16)
    r = x - hi.astype(F32)
    mid = r.astype(BF16)
    return hi, mid, (r - mid.astype(F32)).astype(BF16)


def _dotb(a, b):
    return jnp.dot(a, b, preferred_element_type=F32)


N_LEVELS = CHUNK.bit_length() - 1 - INV_BASE_SHIFT


def _chunk_tables():
    row = lax.broadcasted_iota(jnp.int32, (CHUNK, CHUNK), 0)
    col = lax.broadcasted_iota(jnp.int32, (CHUNK, CHUNK), 1)
    same = [(row >> s) == (col >> s) for s in range(INV_BASE_SHIFT, CHUNK.bit_length())]
    tab32 = jnp.stack([row == col, same[0]]).astype(F32)
    levels = [jnp.logical_and(hi, jnp.logical_not(lo)) for lo, hi in zip(same[:-1], same[1:])]
    tab16 = jnp.stack([row >= col] + levels).astype(BF16)
    return tab32, tab16


def _inv_unit_lower(mats, tab32_ref, tab16_ref, out):
    n = range(len(mats))
    eye = tab32_ref[0]
    d = [mats[i] * tab32_ref[1] for i in n]
    db = [d[i].astype(BF16) for i in n]
    t = [eye - d[i] for i in n]
    ab = [mats[i].astype(BF16) for i in n]
    p = [_dotb(db[i], db[i]) for i in n]
    yield
    t = [t[i] + _dotb(t[i].astype(BF16), p[i].astype(BF16)) for i in n]
    p = [_dotb(p[i].astype(BF16), p[i].astype(BF16)) for i in n]
    yield
    t = [t[i] + _dotb(t[i].astype(BF16), p[i].astype(BF16)) for i in n]
    yield
    for lvl in range(N_LEVELS):
        tb = [t[i].astype(BF16) for i in n]
        x = [_dotb(tb[i], ab[i] * tab16_ref[1 + lvl]) for i in n]
        yield
        t = [t[i] - _dotb(x[i].astype(BF16), tb[i]) for i in n]
        yield
    out.extend(t)


def _trace_interleaved(*gens, steps=None):
    live = list(zip(gens, steps or [1] * len(gens)))
    while live:
        for item in list(live):
            g, k = item
            try:
                for _ in range(k):
                    next(g)
            except StopIteration:
                live.remove(item)


def _gdn_prompt_body(q_ref, k_ref, v_ref, z_ref, ba_ref, cwq_ref, cwk_ref, cwv_ref, gp_ref, on_ref,
                     tab32_ref, tab16_ref, og_ref, s_ref, u_s, w_s, qd_s, at_s, kdt_s, gl_s, a_s, rhs_s):
    hq = pl.program_id(1)
    t_len = q_ref.shape[0]
    n_chunks = t_len // CHUNK
    row = lax.broadcasted_iota(jnp.int32, (CHUNK, CHUNK), 0)
    col = lax.broadcasted_iota(jnp.int32, (CHUNK, CHUNK), 1)
    causal = row >= col
    strict = row > col
    cwq = cwq_ref[...]
    cwk = cwk_ref[...]
    cwv = cwv_ref[...]
    neg_a = -jnp.exp(gp_ref[0:1, :])
    dt_bias = gp_ref[1:2, :]

    nv = HQ_PER_STEP * V_PER_QK
    n_iters = n_chunks // PH1_CHUNKS

    def chunk_prep(n, slot, base):
        r0 = pl.multiple_of(n * CHUNK, CHUNK)
        rows = pl.ds(r0, CHUNK)
        qc = _conv_silu_chunk(q_ref, cwq, n)
        yield
        kc = _conv_silu_chunk(k_ref, cwk, n)
        yield
        vc = _conv_silu_chunk(v_ref, cwv, n)
        yield
        ba = ba_ref[rows, :]
        beta_all = jax.nn.sigmoid(ba)
        xa = ba + dt_bias
        g_all = neg_a * (jnp.maximum(xa, 0.0) + jnp.log1p(jnp.exp(-jnp.abs(xa))))
        g_hi, g_mid, g_lo = _split3(g_all)
        tril = tab16_ref[0]
        gcum = _dotb(tril, g_hi) + (_dotb(tril, g_mid) + _dotb(tril, g_lo))
        gcum_t = gcum.T
        yield
        for q in range(HQ_PER_STEP):
            qh = qc[:, q * DK:(q + 1) * DK]
            kh = kc[:, q * DK:(q + 1) * DK]
            qn = qh * lax.rsqrt(jnp.sum(qh * qh, axis=-1, keepdims=True) + L2_EPS) * (DK ** -0.5)
            kn = kh * lax.rsqrt(jnp.sum(kh * kh, axis=-1, keepdims=True) + L2_EPS)
            kn_t = kn.T
            kn_tb = kn_t.astype(BF16)
            kk = _dotb(kn.astype(BF16), kn_tb)
            qk = _dotb(qn.astype(BF16), kn_tb)
            yield
            for j in range(V_PER_QK):
                h = q * V_PER_QK + j
                lane_b = (hq * HQ_PER_STEP + q) * V_PER_QK + j
                lane_a = V_HEADS + lane_b
                gcol = jnp.sum(jnp.where(col == lane_a, gcum, 0.0), axis=1, keepdims=True)
                bcol = jnp.sum(jnp.where(col == lane_b, beta_all, 0.0), axis=1, keepdims=True)
                grow = jnp.sum(jnp.where(row == lane_a, gcum_t, 0.0), axis=0, keepdims=True)
                glast = gcol[CHUNK - 1:CHUNK, :]
                decay = jnp.exp(jnp.where(causal, gcol - grow, -jnp.inf))
                a_mat = jnp.where(strict, kk * decay, 0.0) * bcol
                eg = jnp.exp(gcol)
                rhs = jnp.concatenate([vc[:, h * DV:(h + 1) * DV] * bcol, kn * (bcol * eg)], axis=1).astype(BF16)
                qd_s[h, rows, :] = (qn * eg).astype(BF16)
                at_s[h, rows, :] = (qk * decay).astype(BF16)
                kdt_s[h, rows, :] = (kn_t * jnp.exp(glast - grow)).astype(BF16)
                gl_s[h, pl.ds(n, 1), :] = jnp.broadcast_to(jnp.exp(glast), (1, DV))
                a_s[slot, base + h] = a_mat
                rhs_s[slot, base + h] = rhs
                yield

    def prep(it):
        it = jnp.asarray(it, jnp.int32)
        slot = it % 2
        for c in range(PH1_CHUNKS):
            yield from chunk_prep(PH1_CHUNKS * it + c, slot, c * nv)

    def solve(it):
        it = jnp.asarray(it, jnp.int32)
        slot = it % 2
        idx = range(PH1_CHUNKS * nv)
        t_mats = []
        yield from _inv_unit_lower([a_s[slot, i] for i in idx], tab32_ref, tab16_ref, t_mats)
        sols = [_dotb(t_mats[i].astype(BF16), rhs_s[slot, i]) for i in idx]
        yield
        for i in idx:
            rows = pl.ds(pl.multiple_of((PH1_CHUNKS * it + i // nv) * CHUNK, CHUNK), CHUNK)
            u_s[i % nv, rows, :] = sols[i][:, :DV]
            w_s[i % nv, rows, :] = sols[i][:, DV:].astype(BF16)
        yield

    gain = on_ref[...]
    heads = range(nv)

    def recur(it):
        it = jnp.asarray(it, jnp.int32)
        for c in range(PH1_CHUNKS):
            n = PH1_CHUNKS * it + c
            rows = pl.ds(pl.multiple_of(n * CHUNK, CHUNK), CHUNK)
            states = [s_ref[0, h] for h in heads]
            wq = [jnp.concatenate([w_s[h, rows, :], qd_s[h, rows, :]], axis=0) for h in heads]
            ws_qs = [_dotb(wq[h], states[h].astype(BF16)) for h in heads]
            yield
            v_new = [(u_s[h, rows, :] - ws_qs[h][:CHUNK]).astype(BF16) for h in heads]
            o = [ws_qs[h][CHUNK:] + _dotb(at_s[h, rows, :], v_new[h]) for h in heads]
            s_new = [states[h] * gl_s[h, pl.ds(n, 1), :] + _dotb(kdt_s[h, rows, :], v_new[h]) for h in heads]
            yield
            for h in heads:
                s_ref[0, h] = s_new[h]
                zg = z_ref[rows, h * DV:(h + 1) * DV]
                og = o[h] * lax.rsqrt(jnp.mean(o[h] * o[h], axis=-1, keepdims=True) + RMS_EPS) * gain * _silu(zg)
                og_ref[rows, h * DV:(h + 1) * DV] = og.astype(BF16)
            yield

    for h in heads:
        s_ref[0, h] = jnp.zeros((DK, DV), F32)
    _trace_interleaved(prep(0))
    _trace_interleaved(solve(0), prep(1), steps=PIPE_STEPS[:2])

    def steady(it, carry):
        _trace_interleaved(solve(it - 1), prep(it), recur(it - 2), steps=PIPE_STEPS)
        return carry

    lax.fori_loop(2, n_iters, steady, 0)
    _trace_interleaved(solve(n_iters - 1), recur(n_iters - 2), steps=(2, 1))
    _trace_interleaved(recur(n_iters - 1))


def _gdn_prompt(proj, ba, conv_w, gate_params, o_gain, *, batch, t_len):
    wq = HQ_PER_STEP * DK
    wv = HQ_PER_STEP * V_PER_QK * DV
    nv = HQ_PER_STEP * V_PER_QK
    in_specs = [
        pl.BlockSpec((t_len, wq), lambda b, h: (b, h)),
        pl.BlockSpec((t_len, wq), lambda b, h: (b, KEY_DIM // wq + h)),
        pl.BlockSpec((t_len, wv), lambda b, h: (b, (2 * KEY_DIM) // wv + h)),
        pl.BlockSpec((t_len, wv), lambda b, h: (b, CONV_DIM // wv + h)),
        pl.BlockSpec((t_len, 128), lambda b, h: (b, 0)),
        pl.BlockSpec((CONV_W, wq), lambda b, h: (0, h)),
        pl.BlockSpec((CONV_W, wq), lambda b, h: (0, KEY_DIM // wq + h)),
        pl.BlockSpec((CONV_W, wv), lambda b, h: (0, (2 * KEY_DIM) // wv + h)),
        pl.BlockSpec((2, 128), lambda b, h: (0, 0)),
        pl.BlockSpec((1, DV), lambda b, h: (0, 0)),
        pl.BlockSpec((2, CHUNK, CHUNK), lambda b, h: (0, 0, 0)),
        pl.BlockSpec((1 + N_LEVELS, CHUNK, CHUNK), lambda b, h: (0, 0, 0)),
    ]
    tab32, tab16 = _chunk_tables()
    out_shape = [
        jax.ShapeDtypeStruct((batch * t_len, VAL_DIM), BF16),
        jax.ShapeDtypeStruct((batch, V_HEADS, DK, DV), F32),
    ]
    out_specs = [
        pl.BlockSpec((t_len, wv), lambda b, h: (b, h)),
        pl.BlockSpec((1, nv, DK, DV), lambda b, h: (b, h, 0, 0)),
    ]
    scratch = [
        pltpu.VMEM((nv, t_len, DV), F32),
        pltpu.VMEM((nv, t_len, DK), BF16),
        pltpu.VMEM((nv, t_len, DK), BF16),
        pltpu.VMEM((nv, t_len, CHUNK), BF16),
        pltpu.VMEM((nv, t_len, CHUNK), BF16),
        pltpu.VMEM((nv, t_len // CHUNK, DV), F32),
        pltpu.VMEM((2, PH1_CHUNKS * nv, CHUNK, CHUNK), F32),
        pltpu.VMEM((2, PH1_CHUNKS * nv, CHUNK, DV + DK), BF16),
    ]
    return pl.pallas_call(
        _gdn_prompt_body,
        out_shape=out_shape,
        grid=(batch, QK_HEADS // HQ_PER_STEP),
        in_specs=in_specs,
        out_specs=out_specs,
        scratch_shapes=scratch,
        compiler_params=_params(("parallel", "arbitrary")),
        name="gdn_prompt",
    )(proj, proj, proj, proj, ba, conv_w, conv_w, conv_w, gate_params, o_gain.reshape(1, DV), tab32, tab16)


def _gdn_gates_body(ba_ref, gp_ref, beta_ref, eg_ref):
    ba = ba_ref[...]
    beta_ref[...] = jax.nn.sigmoid(ba)
    xa = ba + gp_ref[1:2, :]
    g = -jnp.exp(gp_ref[0:1, :]) * (jnp.maximum(xa, 0.0) + jnp.log1p(jnp.exp(-jnp.abs(xa))))
    eg_ref[...] = jnp.exp(g)


def _gdn_gates(ba, gate_params):
    m = ba.shape[0]
    return pl.pallas_call(
        _gdn_gates_body,
        out_shape=[jax.ShapeDtypeStruct((m, 128), F32)] * 2,
        name="gdn_gates",
    )(ba, gate_params)


def _gdn_decode_body(beta_ref, eg_ref, xs_ref, cw_ref, z_ref, on_ref, s_ref, so_ref, og_ref):
    for si in range(DEC_SAMPLES):
        b = pl.program_id(0) * DEC_SAMPLES + si
        y = xs_ref[si, 0] * cw_ref[0]
        for j in range(1, CONV_W):
            y = y + xs_ref[si, j] * cw_ref[j]
        y = _silu(y)
        q = y[0:QK_HEADS]
        k = y[QK_HEADS:2 * QK_HEADS]
        v = y[2 * QK_HEADS:]
        qn = q * lax.rsqrt(jnp.sum(q * q, axis=-1, keepdims=True) + L2_EPS) * (DK ** -0.5)
        kn = k * lax.rsqrt(jnp.sum(k * k, axis=-1, keepdims=True) + L2_EPS)
        qk_t = jnp.concatenate([qn, kn, jnp.zeros((DK - 2 * QK_HEADS, DK), F32)], axis=0).T
        for g0 in range(0, V_HEADS, DEC_GROUP):
            hs = range(g0, g0 + DEC_GROUP)
            qcol = {h: qk_t[:, h // V_PER_QK:h // V_PER_QK + 1] for h in hs}
            kcol = {h: qk_t[:, QK_HEADS + h // V_PER_QK:QK_HEADS + h // V_PER_QK + 1] for h in hs}
            sd = {h: s_ref[si, h] * eg_ref[b, h] for h in hs}
            ks = {h: jnp.sum(sd[h] * kcol[h], axis=0, keepdims=True) for h in hs}
            v_new = {h: beta_ref[b, h] * (v[h:h + 1, :] - ks[h]) for h in hs}
            sn = {h: sd[h] + kcol[h] * v_new[h] for h in hs}
            for h in hs:
                so_ref[si, h] = sn[h]
            o = {h: jnp.sum(sn[h] * qcol[h], axis=0, keepdims=True) for h in hs}
            for h in hs:
                og_ref[si, h:h + 1, :] = o[h]
        o_all = og_ref[si]
        og_ref[si] = (o_all * lax.rsqrt(jnp.mean(o_all * o_all, axis=-1, keepdims=True) + RMS_EPS)
                      * on_ref[...] * _silu(z_ref[si]))


def _gdn_decode(beta, eg, xs, conv_w3, z3, o_gain, state):
    nb = xs.shape[0]
    slots = CONV_DIM // DK
    smem = pl.BlockSpec(memory_space=pltpu.SMEM)
    return pl.pallas_call(
        _gdn_decode_body,
        out_shape=[
            jax.ShapeDtypeStruct((nb, V_HEADS, DK, DV), F32),
            jax.ShapeDtypeStruct((nb, V_HEADS, DV), F32),
        ],
        grid=(nb // DEC_SAMPLES,),
        in_specs=[
            smem,
            smem,
            pl.BlockSpec((DEC_SAMPLES, CONV_W, slots, DK), lambda b: (b, 0, 0, 0)),
            pl.BlockSpec((CONV_W, slots, DK), lambda b: (0, 0, 0)),
            pl.BlockSpec((DEC_SAMPLES, V_HEADS, DV), lambda b: (b, 0, 0)),
            pl.BlockSpec((1, DV), lambda b: (0, 0)),
            pl.BlockSpec((DEC_SAMPLES, V_HEADS, DK, DV), lambda b: (b, 0, 0, 0)),
        ],
        out_specs=[
            pl.BlockSpec((DEC_SAMPLES, V_HEADS, DK, DV), lambda b: (b, 0, 0, 0)),
            pl.BlockSpec((DEC_SAMPLES, V_HEADS, DV), lambda b: (b, 0, 0)),
        ],
        compiler_params=_params(("parallel",)),
        name="gdn_decode",
    )(beta, eg, xs, conv_w3, z3, o_gain.reshape(1, DV), state)


def _ssm_pow_body(lr_ref, li_ref, ldt_ref, pr_ref, pi_ref):
    n = jnp.where(lax.broadcasted_iota(jnp.int32, pr_ref.shape, 0) == 0, 1.0, float(SCAN_SEG))
    dt = jnp.exp(ldt_ref[...])
    lr = jnp.minimum(lr_ref[...], -1e-4) * dt
    li = li_ref[...] * dt
    mag = jnp.exp(n * lr)
    pr_ref[...] = mag * jnp.cos(n * li)
    pi_ref[...] = mag * jnp.sin(n * li)


def _ssm_pow(lr, li, ldt):
    gp = lr.shape[1]
    tl = 2048
    spec = pl.BlockSpec((1, tl), lambda i: (0, i))
    ospec = pl.BlockSpec((POW_ROWS, tl), lambda i: (0, i))
    return pl.pallas_call(
        _ssm_pow_body,
        out_shape=[jax.ShapeDtypeStruct((POW_ROWS, gp), F32)] * 2,
        grid=(gp // tl,),
        in_specs=[spec, spec, spec],
        out_specs=[ospec, ospec],
        name="ssm_pow",
    )(lr, li, ldt)


def _ssm_blocks_body(lr_ref, li_ref, ldt_ref, br_ref, bi_ref, cr_ref, ci_ref, bblk_ref, cblk_ref):
    dt = jnp.exp(ldt_ref[...])
    lr = jnp.minimum(lr_ref[...], -1e-4)
    li = li_ref[...]
    mag = jnp.exp(lr * dt)
    xr = mag * jnp.cos(li * dt) - 1.0
    xi = mag * jnp.sin(li * dt)
    den = 1.0 / (lr * lr + li * li)
    fr = (xr * lr + xi * li) * den
    fi = (xi * lr - xr * li) * den
    bblk_ref[...] = jnp.zeros_like(bblk_ref)
    cblk_ref[...] = jnp.zeros_like(cblk_ref)
    per_tile = 128 // SSM_STATE
    for g in range(GROUPS_PER_STEP):
        rows = slice(g * SSM_GROUP, (g + 1) * SSM_GROUP)
        off = (g // per_tile) * 256 + (g % per_tile) * SSM_STATE
        cols = slice(off, off + SSM_STATE)
        cols_im = slice(off + 128, off + 128 + SSM_STATE)
        br = br_ref[g]
        bi = bi_ref[g]
        frg = fr[g:g + 1, :]
        fig = fi[g:g + 1, :]
        bblk_ref[rows, cols] = (frg * br - fig * bi).astype(BF16)
        bblk_ref[rows, cols_im] = (frg * bi + fig * br).astype(BF16)
        cblk_ref[rows, cols] = cr_ref[g].astype(BF16)
        cblk_ref[rows, cols_im] = (-ci_ref[g]).astype(BF16)


def _ssm_blocks(lr, li, ldt, bt_re, bt_im, c_re, c_im):
    g, p = lr.shape
    k = GROUPS_PER_STEP
    s2 = pl.BlockSpec((k, p), lambda i: (i, 0))
    s1 = pl.BlockSpec((k, 1), lambda i: (i, 0))
    s3 = pl.BlockSpec((k, SSM_GROUP, p), lambda i: (i, 0, 0))
    blk = pl.BlockSpec((None, CH_PER_STEP, 2 * ST_PER_STEP), lambda i: (i, 0, 0))
    return pl.pallas_call(
        _ssm_blocks_body,
        out_shape=[jax.ShapeDtypeStruct((g // k, CH_PER_STEP, 2 * ST_PER_STEP), BF16)] * 2,
        grid=(g // k,),
        in_specs=[s2, s2, s1, s3, s3, s3, s3],
        out_specs=[blk, blk],
        name="ssm_blocks",
    )(lr, li, ldt, bt_re, bt_im, c_re, c_im)


def _dot_nt(a, b):
    return lax.dot_general(a, b, (((1,), (1,)), ((), ())), preferred_element_type=F32)


def _gelu_tanh(x):
    return 0.5 * x * (1.0 + jnp.tanh(math.sqrt(2.0 / math.pi) * (x + 0.044715 * (x * x * x))))


def _ssm_scan_body(ua_ref, ub_ref, b_ref, c_ref, pr_ref, pi_ref, d_ref, y_ref, hr_ref, hi_ref,
                   h_s, y_s, cr_s, ci_s):
    tb = pl.program_id(2)
    nt = ST_PER_STEP // 128
    u_perm = jnp.concatenate(
        [jnp.concatenate([ua_ref[pl.ds(i, 8, stride=SCAN_SEG), :], ub_ref[pl.ds(i, 8, stride=SCAN_SEG), :]], axis=1)
         for i in range(SCAN_SEG)], axis=0).astype(BF16)

    @pl.when(tb == 0)
    def _():
        cr_s[...] = jnp.zeros_like(cr_s)
        ci_s[...] = jnp.zeros_like(ci_s)

    def b_proj(g):
        for c in range(g * SCAN_TILES, (g + 1) * SCAN_TILES):
            bu = jnp.dot(u_perm, b_ref[:, c * 256:(c + 1) * 256], preferred_element_type=F32)
            h_s[c] = bu[:, :128]
            h_s[nt + c] = bu[:, 128:]

    row8 = lax.broadcasted_iota(jnp.int32, (8, 128), 0)

    def tile_scan(c):
        lanes = slice(c * 128, (c + 1) * 128)
        ar = jnp.broadcast_to(pr_ref[0:1, lanes], (8, 128))
        ai = jnp.broadcast_to(pi_ref[0:1, lanes], (8, 128))
        hr = jnp.zeros((8, 128), F32)
        hi = jnp.zeros((8, 128), F32)
        for i in range(SCAN_SEG):
            xr = h_s[c, i * 8:(i + 1) * 8, :]
            xi = h_s[nt + c, i * 8:(i + 1) * 8, :]
            hr, hi = ar * hr - ai * hi + xr, ar * hi + ai * hr + xi
            if i % SCAN_YIELD == SCAN_YIELD - 1:
                yield

        asr = pr_ref[1:2, lanes]
        asi = pi_ref[1:2, lanes]
        c_r = cr_s[:, lanes]
        c_i = ci_s[:, lanes]
        cin_r = jnp.zeros((8, 128), F32)
        cin_i = jnp.zeros((8, 128), F32)
        for s in range(8):
            cin_r = jnp.where(row8 == s, c_r, cin_r)
            cin_i = jnp.where(row8 == s, c_i, cin_i)
            e_r = hr[s:s + 1, :]
            e_i = hi[s:s + 1, :]
            c_r, c_i = e_r + asr * c_r - asi * c_i, e_i + asr * c_i + asi * c_r
        cr_s[:, lanes] = c_r
        ci_s[:, lanes] = c_i
        yield

        hr, hi = cin_r, cin_i
        for i in range(SCAN_SEG):
            xr = h_s[c, i * 8:(i + 1) * 8, :]
            xi = h_s[nt + c, i * 8:(i + 1) * 8, :]
            hr, hi = ar * hr - ai * hi + xr, ar * hi + ai * hr + xi
            h_s[c, i * 8:(i + 1) * 8, :] = hr
            h_s[nt + c, i * 8:(i + 1) * 8, :] = hi
            if i % SCAN_YIELD == SCAN_YIELD - 1:
                yield

    n_groups = nt // SCAN_TILES
    b_proj(0)
    y = None
    for g in range(n_groups):
        if g + 1 < n_groups:
            b_proj(g + 1)
        tiles = range(g * SCAN_TILES, (g + 1) * SCAN_TILES)
        _trace_interleaved(*[tile_scan(c) for c in tiles])
        for c in tiles:
            h_cat = jnp.concatenate([h_s[c], h_s[nt + c]], axis=1).astype(BF16)
            y_c = _dot_nt(h_cat, c_ref[:, c * 256:(c + 1) * 256])
            y = y_c if y is None else y + y_c

    y_s[0] = y[:, :128]
    y_s[1] = y[:, 128:]
    per_seg = SCAN_SEG // 8
    y_nat = jnp.concatenate(
        [jnp.concatenate([y_s[h, pl.ds((j % per_seg) * 64 + j // per_seg, 8, stride=8), :] for h in range(2)], axis=1)
         for j in range(SCAN_BLOCK // 8)], axis=0)
    u = jnp.concatenate([ua_ref[...], ub_ref[...]], axis=1)
    y_ref[...] = _gelu_tanh(y_nat + d_ref[...] * u)

    @pl.when(tb == pl.num_programs(2) - 1)
    def _():
        hr_ref[...] = cr_s[...]
        hi_ref[...] = ci_s[...]


def _ssm_scan(uz, bblk, cblk, pw_r, pw_i, d_skip, *, batch, t_len):
    nt = t_len // SCAN_BLOCK
    return pl.pallas_call(
        _ssm_scan_body,
        out_shape=[
            jax.ShapeDtypeStruct((batch * t_len, SSM_WIDTH), F32),
            jax.ShapeDtypeStruct((batch, 1, SSM_GROUPS * SSM_STATE), F32),
            jax.ShapeDtypeStruct((batch, 1, SSM_GROUPS * SSM_STATE), F32),
        ],
        grid=(batch, N_GSTEPS, nt),
        in_specs=[
            pl.BlockSpec((SCAN_BLOCK, 128), lambda b, g, t: (b * nt + t, 2 * g)),
            pl.BlockSpec((SCAN_BLOCK, 128), lambda b, g, t: (b * nt + t, 2 * g + 1)),
            pl.BlockSpec((None, CH_PER_STEP, 2 * ST_PER_STEP), lambda b, g, t: (g, 0, 0)),
            pl.BlockSpec((None, CH_PER_STEP, 2 * ST_PER_STEP), lambda b, g, t: (g, 0, 0)),
            pl.BlockSpec((POW_ROWS, ST_PER_STEP), lambda b, g, t: (0, g)),
            pl.BlockSpec((POW_ROWS, ST_PER_STEP), lambda b, g, t: (0, g)),
            pl.BlockSpec((1, CH_PER_STEP), lambda b, g, t: (0, g)),
        ],
        out_specs=[
            pl.BlockSpec((SCAN_BLOCK, CH_PER_STEP), lambda b, g, t: (b * nt + t, g)),
            pl.BlockSpec((None, 1, ST_PER_STEP), lambda b, g, t: (b, 0, g)),
            pl.BlockSpec((None, 1, ST_PER_STEP), lambda b, g, t: (b, 0, g)),
        ],
        scratch_shapes=[
            pltpu.VMEM((2 * ST_PER_STEP // 128, SCAN_BLOCK, 128), F32),
            pltpu.VMEM((CH_PER_STEP // 128, SCAN_BLOCK, 128), F32),
            pltpu.VMEM((1, ST_PER_STEP), F32),
            pltpu.VMEM((1, ST_PER_STEP), F32),
        ],
        compiler_params=_params(("parallel", "parallel", "arbitrary")),
        name="ssm_scan",
    )(uz, uz, bblk, cblk, pw_r, pw_i, d_skip.reshape(1, SSM_WIDTH))


def _ssm_decode_body(u_ref, b_ref, c_ref, pr_ref, pi_ref, d_ref, h0r_ref, h0i_ref, y_ref, hr_ref, hi_ref):
    nt = ST_PER_STEP // 128
    u = u_ref[...]
    bu = jnp.dot(u.astype(BF16), b_ref[...], preferred_element_type=F32)
    h_cat = []
    for c in range(nt):
        lanes = slice(c * 128, (c + 1) * 128)
        ar = pr_ref[0:1, lanes]
        ai = pi_ref[0:1, lanes]
        h0r = h0r_ref[:, lanes]
        h0i = h0i_ref[:, lanes]
        hr = bu[:, c * 256:c * 256 + 128] + ar * h0r - ai * h0i
        hi = bu[:, c * 256 + 128:(c + 1) * 256] + ar * h0i + ai * h0r
        hr_ref[:, lanes] = hr
        hi_ref[:, lanes] = hi
        h_cat += [hr, hi]
    y = _dot_nt(jnp.concatenate(h_cat, axis=1).astype(BF16), c_ref[...])
    y_ref[...] = _gelu_tanh(y + d_ref[...] * u)


def _ssm_decode(uz, bblk, cblk, pw_r, pw_i, d_skip, h0r, h0i):
    nb = uz.shape[0]
    return pl.pallas_call(
        _ssm_decode_body,
        out_shape=[
            jax.ShapeDtypeStruct((nb, SSM_WIDTH), F32),
            jax.ShapeDtypeStruct((nb, SSM_GROUPS * SSM_STATE), F32),
            jax.ShapeDtypeStruct((nb, SSM_GROUPS * SSM_STATE), F32),
        ],
        grid=(N_GSTEPS,),
        in_specs=[
            pl.BlockSpec((nb, CH_PER_STEP), lambda g: (0, g)),
            pl.BlockSpec((None, CH_PER_STEP, 2 * ST_PER_STEP), lambda g: (g, 0, 0)),
            pl.BlockSpec((None, CH_PER_STEP, 2 * ST_PER_STEP), lambda g: (g, 0, 0)),
            pl.BlockSpec((POW_ROWS, ST_PER_STEP), lambda g: (0, g)),
            pl.BlockSpec((POW_ROWS, ST_PER_STEP), lambda g: (0, g)),
            pl.BlockSpec((1, CH_PER_STEP), lambda g: (0, g)),
            pl.BlockSpec((nb, ST_PER_STEP), lambda g: (0, g)),
            pl.BlockSpec((nb, ST_PER_STEP), lambda g: (0, g)),
        ],
        out_specs=[
            pl.BlockSpec((nb, CH_PER_STEP), lambda g: (0, g)),
            pl.BlockSpec((nb, ST_PER_STEP), lambda g: (0, g)),
            pl.BlockSpec((nb, ST_PER_STEP), lambda g: (0, g)),
        ],
        compiler_params=_params(("parallel",)),
        name="ssm_decode",
    )(uz, bblk, cblk, pw_r, pw_i, d_skip.reshape(1, SSM_WIDTH), h0r, h0i)


GLU_SLAB = 256


def _glu_body(y_ref, w_ref, b_ref, z_ref, o_ref, yb_scr, *, tn):
    j = pl.program_id(1)

    @pl.when(j == 0)
    def _():
        yb_scr[...] = y_ref[...].astype(BF16)

    for c in range(tn // GLU_SLAB):
        cols = slice(c * GLU_SLAB, (c + 1) * GLU_SLAB)
        t = jnp.dot(yb_scr[...], w_ref[:, cols], preferred_element_type=F32) + b_ref[:, cols]
        y = y_ref[:, pl.ds(pl.multiple_of(j * tn + c * GLU_SLAB, GLU_SLAB), GLU_SLAB)]
        o_ref[:, cols] = (y * jax.nn.sigmoid(t) * _silu(z_ref[:, cols])).astype(BF16)


def _glu(y, w, b, uz, *, tm, tn):
    m, d = y.shape
    zoff = SSM_WIDTH // tn
    return pl.pallas_call(
        functools.partial(_glu_body, tn=tn),
        out_shape=jax.ShapeDtypeStruct((m, d), BF16),
        grid=(m // tm, d // tn),
        in_specs=[
            pl.BlockSpec((tm, d), lambda i, j: (i, 0)),
            pl.BlockSpec((d, tn), lambda i, j: (0, j)),
            pl.BlockSpec((1, tn), lambda i, j: (0, j)),
            pl.BlockSpec((tm, tn), lambda i, j: (i, zoff + j)),
        ],
        out_specs=pl.BlockSpec((tm, tn), lambda i, j: (i, j)),
        scratch_shapes=[pltpu.VMEM((tm, d), BF16)],
        compiler_params=_params(("parallel", "arbitrary")),
        name="ssm_glu",
    )(y, w, b.reshape(1, d), uz)


def _prep_weights(norm_gdn, w_in_gdn, conv_gdn, a_log_gdn, dt_bias_gdn, onorm_gdn, w_out_gdn,
                  norm_ssm, w_in_ssm, lam_re, lam_im, b_re, b_im, c_re, c_im, d_ssm, log_dt_ssm,
                  w_glu_ssm, b_glu_ssm, w_out_ssm, norm_final):
    w_in_t = jnp.swapaxes(w_in_gdn[0], 0, 1)
    w_ba = jnp.pad(w_in_t[MAIN_DIM:], ((0, 128 - 2 * V_HEADS), (0, 0))).astype(BF16)
    zeros = jnp.zeros((V_HEADS,), F32)
    pad = jnp.zeros((128 - 2 * V_HEADS,), F32)
    gate_params = jnp.stack([jnp.concatenate([zeros, a_log_gdn[0], pad]),
                             jnp.concatenate([zeros, dt_bias_gdn[0], pad])])

    lr, li, ldt = lam_re[0], lam_im[0], log_dt_ssm[0]
    gp = SSM_GROUPS * SSM_STATE
    pw_r, pw_i = _ssm_pow(lr.reshape(1, gp), li.reshape(1, gp),
                          jnp.repeat(ldt, SSM_STATE).reshape(1, gp))
    bblk, cblk = _ssm_blocks(lr, li, ldt.reshape(SSM_GROUPS, 1),
                                 jnp.swapaxes(b_re[0], 1, 2), jnp.swapaxes(b_im[0], 1, 2), c_re[0], c_im[0])
    return dict(
        norm_gdn=norm_gdn[0], w_main=w_in_t, w_ba=w_ba, conv_w=conv_gdn[0], gate_params=gate_params,
        o_gain=onorm_gdn[0], w_out_gdn=w_out_gdn[0].astype(BF16),
        norm_ssm=norm_ssm[0], w_in_ssm=w_in_ssm[0], pw_r=pw_r, pw_i=pw_i, bblk=bblk, cblk=cblk,
        d_skip=d_ssm[0], w_glu=w_glu_ssm[0].astype(BF16), b_glu=b_glu_ssm[0], w_out_ssm=w_out_ssm[0].astype(BF16),
        norm_final=norm_final)


def _ssm_tail(p, x1, y, uz, *, tm, tm_glu):
    y3 = _glu(y, p["w_glu"], p["b_glu"], uz, tm=tm_glu, tn=1024)
    return _outproj(y3, p["w_out_ssm"], x1, p["norm_final"], tm=tm, name="ssm_out")


def _prompt_path(p, x_prompt):
    batch, t_len, d = x_prompt.shape
    x = x_prompt.reshape(batch * t_len, d)
    proj, ba = _norm_matmul(x, p["norm_gdn"], p["w_main"], p["w_ba"], n=MAIN_DIM, tm=2048, tn=512, name="gdn_in",
                            w_transposed=True)
    og, delta = _gdn_prompt(proj, ba, p["conv_w"], p["gate_params"], p["o_gain"], batch=batch, t_len=t_len)
    conv_state = proj.reshape(batch, t_len, MAIN_DIM)[:, t_len - (CONV_W - 1):, :CONV_DIM]
    x1 = _outproj(og, p["w_out_gdn"], x, tm=512, name="gdn_out")
    uz = _norm_matmul(x1, p["norm_ssm"], p["w_in_ssm"], n=2 * SSM_WIDTH, tm=2048, tn=512, name="ssm_in")
    y, h_re, h_im = _ssm_scan(uz, p["bblk"], p["cblk"], p["pw_r"], p["pw_i"], p["d_skip"],
                              batch=batch, t_len=t_len)
    out = _ssm_tail(p, x1, y, uz, tm=512, tm_glu=512)
    return (out.reshape(batch, t_len, d), conv_state[None], delta[None],
            h_re.reshape(1, batch, SSM_GROUPS, SSM_STATE), h_im.reshape(1, batch, SSM_GROUPS, SSM_STATE))


def _sample_path(p, x_sample, state_conv, state_delta, state_re, state_im):
    nb, _, d = x_sample.shape
    x = x_sample.reshape(nb, d)
    proj, ba = _norm_matmul(x, p["norm_gdn"], p["w_main"], p["w_ba"], n=MAIN_DIM, tm=nb, tn=512, name="gdn_in_s",
                            w_transposed=True)
    beta, eg = _gdn_gates(ba, p["gate_params"])
    xs = jnp.concatenate([state_conv[0], proj[:, None, :CONV_DIM]], axis=1)
    slots = CONV_DIM // DK
    delta, og = _gdn_decode(beta[:, :V_HEADS], eg[:, V_HEADS:2 * V_HEADS],
                            xs.reshape(nb, CONV_W, slots, DK), p["conv_w"].reshape(CONV_W, slots, DK),
                            proj[:, CONV_DIM:].reshape(nb, V_HEADS, DV), p["o_gain"], state_delta[0])
    x1 = _outproj(og.reshape(nb, VAL_DIM).astype(BF16), p["w_out_gdn"], x, tm=nb, name="gdn_out_s")
    uz = _norm_matmul(x1, p["norm_ssm"], p["w_in_ssm"], n=2 * SSM_WIDTH, tm=nb, tn=512, name="ssm_in_s")
    gp = SSM_GROUPS * SSM_STATE
    y, h_re, h_im = _ssm_decode(uz, p["bblk"], p["cblk"], p["pw_r"], p["pw_i"], p["d_skip"],
                                state_re[0].reshape(nb, gp), state_im[0].reshape(nb, gp))
    out = _ssm_tail(p, x1, y, uz, tm=nb, tm_glu=nb)
    return (out.reshape(nb, 1, d), xs[:, 1:][None], delta[None],
            h_re.reshape(1, nb, SSM_GROUPS, SSM_STATE), h_im.reshape(1, nb, SSM_GROUPS, SSM_STATE))


def kernel(x_prompt, x_sample, state_gdn_conv, state_gdn_delta, state_ssm_re, state_ssm_im, norm_gdn, w_in_gdn, conv_gdn, a_log_gdn, dt_bias_gdn, onorm_gdn, w_out_gdn, norm_ssm, w_in_ssm, lam_re, lam_im, b_re, b_im, c_re, c_im, d_ssm, log_dt_ssm, w_glu_ssm, b_glu_ssm, w_out_ssm, norm_final):
    p = _prep_weights(norm_gdn, w_in_gdn, conv_gdn, a_log_gdn, dt_bias_gdn, onorm_gdn, w_out_gdn,
                      norm_ssm, w_in_ssm, lam_re, lam_im, b_re, b_im, c_re, c_im, d_ssm, log_dt_ssm,
                      w_glu_ssm, b_glu_ssm, w_out_ssm, norm_final)
    y_p, conv_p, delta_p, re_p, im_p = _prompt_path(p, x_prompt)
    y_s, conv_s, delta_s, re_s, im_s = _sample_path(p, x_sample, state_gdn_conv, state_gdn_delta,
                                                    state_ssm_re, state_ssm_im)
    return (y_p, y_s, conv_p, delta_p, re_p, im_p, conv_s, delta_s, re_s, im_s)
```

```python
import functools
import math

import jax
import jax.numpy as jnp
from jax import lax
from jax.experimental import pallas as pl
from jax.experimental.pallas import tpu as pltpu

F32 = jnp.float32
BF16 = jnp.bfloat16

RMS_EPS = 1e-6
L2_EPS = 1e-6

D_MODEL = 2048
QK_HEADS = 16
V_HEADS = 32
DK = 128
DV = 128
KEY_DIM = QK_HEADS * DK
VAL_DIM = V_HEADS * DV
CONV_DIM = 2 * KEY_DIM + VAL_DIM
CONV_W = 4
MAIN_DIM = CONV_DIM + VAL_DIM

SSM_WIDTH = 4096
SSM_GROUP = 16
SSM_GROUPS = SSM_WIDTH // SSM_GROUP
SSM_STATE = 64
GROUPS_PER_STEP = 16
CH_PER_STEP = GROUPS_PER_STEP * SSM_GROUP
ST_PER_STEP = GROUPS_PER_STEP * SSM_STATE
N_GSTEPS = SSM_GROUPS // GROUPS_PER_STEP

CHUNK = 128
SCAN_BLOCK = 2048
SCAN_SEG = SCAN_BLOCK // 8
POW_ROWS = 8

VMEM_LIMIT = 56 * 1024 * 1024


def _params(sem):
    return pltpu.CompilerParams(dimension_semantics=sem, vmem_limit_bytes=VMEM_LIMIT)


def _silu(x):
    return x * jax.nn.sigmoid(x)


def _rms(x, g):
    ms = jnp.mean(x * x, axis=-1, keepdims=True)
    return x * lax.rsqrt(ms + RMS_EPS) * g


def _norm_matmul_body(x_ref, g_ref, w_ref, *rest, has_aux, w_transposed):
    if has_aux:
        waux_ref, o_ref, aux_ref, h_scr = rest
    else:
        o_ref, h_scr = rest
    mm = _dot_nt if w_transposed else _dotb

    @pl.when(pl.program_id(1) == 0)
    def _():
        h = _rms(x_ref[...], g_ref[...]).astype(BF16)
        h_scr[...] = h
        if has_aux:
            aux_ref[...] = mm(h, waux_ref[...])

    o_ref[...] = mm(h_scr[...], w_ref[...].astype(BF16))


def _norm_matmul(x, g, w, w_aux=None, *, n, tm, tn, name, w_transposed=False):
    m, d = x.shape
    has_aux = w_aux is not None
    in_specs = [
        pl.BlockSpec((tm, d), lambda i, j: (i, 0), pipeline_mode=pl.Buffered(1)),
        pl.BlockSpec((1, d), lambda i, j: (0, 0)),
        pl.BlockSpec((tn, d), lambda i, j: (j, 0)) if w_transposed else pl.BlockSpec((d, tn), lambda i, j: (0, j)),
    ]
    out_shape = [jax.ShapeDtypeStruct((m, n), F32)]
    out_specs = [pl.BlockSpec((tm, tn), lambda i, j: (i, j))]
    args = [x, g.reshape(1, d), w]
    if has_aux:
        na = w_aux.shape[0] if w_transposed else w_aux.shape[1]
        in_specs.append(pl.BlockSpec(w_aux.shape, lambda i, j: (0, 0)))
        out_shape.append(jax.ShapeDtypeStruct((m, na), F32))
        out_specs.append(pl.BlockSpec((tm, na), lambda i, j: (i, 0)))
        args.append(w_aux)
    res = pl.pallas_call(
        functools.partial(_norm_matmul_body, has_aux=has_aux, w_transposed=w_transposed),
        out_shape=out_shape,
        grid=(m // tm, n // tn),
        in_specs=in_specs,
        out_specs=out_specs,
        scratch_shapes=[pltpu.VMEM((tm, d), BF16)],
        compiler_params=_params(("parallel", "arbitrary")),
        name=name,
    )(*args)
    return res if has_aux else res[0]


def _outproj_body(a_ref, w_ref, x_ref, *rest, final_norm):
    if final_norm:
        g_ref, o_ref = rest
    else:
        (o_ref,) = rest
    y = x_ref[...] + jnp.dot(a_ref[...], w_ref[...], preferred_element_type=F32)
    o_ref[...] = _rms(y, g_ref[...]) if final_norm else y


def _outproj(a, w, x, g=None, *, tm, name):
    m, kd = a.shape
    n = w.shape[1]
    final_norm = g is not None
    in_specs = [
        pl.BlockSpec((tm, kd), lambda i: (i, 0)),
        pl.BlockSpec((kd, n), lambda i: (0, 0), pipeline_mode=pl.Buffered(1)),
        pl.BlockSpec((tm, n), lambda i: (i, 0)),
    ]
    args = [a, w, x]
    if final_norm:
        in_specs.append(pl.BlockSpec((1, n), lambda i: (0, 0)))
        args.append(g.reshape(1, n))
    return pl.pallas_call(
        functools.partial(_outproj_body, final_norm=final_norm),
        out_shape=jax.ShapeDtypeStruct((m, n), F32),
        grid=(m // tm,),
        in_specs=in_specs,
        out_specs=pl.BlockSpec((tm, n), lambda i: (i, 0)),
        compiler_params=_params(("parallel",)),
        name=name,
    )(*args)


def _conv_silu_chunk(x_ref, cw, n):
    cc = x_ref.shape[1]
    r0 = pl.multiple_of(n * CHUNK, CHUNK)
    cur = x_ref[pl.ds(r0, CHUNK), :]
    p0 = pl.multiple_of(jnp.maximum(r0 - 8, 0), 8)
    prev = jnp.where(n > 0, x_ref[pl.ds(p0, 8), :], 0.0)
    row8 = lax.broadcasted_iota(jnp.int32, (8, cc), 0)
    y = cur * cw[CONV_W - 1:CONV_W, :]
    for s in range(1, CONV_W):
        sh = pltpu.roll(cur, s, 0)
        top = jnp.where(row8 < s, pltpu.roll(prev, s, 0), sh[0:8, :])
        sh = jnp.concatenate([top, sh[8:, :]], axis=0)
        y = y + sh * cw[CONV_W - 1 - s:CONV_W - s, :]
    return _silu(y)


INV_BASE_SHIFT = 3
V_PER_QK = V_HEADS // QK_HEADS
HQ_PER_STEP = 2
DEC_GROUP = 8
DEC_SAMPLES = 4
PH1_CHUNKS = 2
PIPE_STEPS = (1, 2, 1)


def _split3(x):
    hi = x.astype(BF16)
    r = x - hi.astype(F32)
    mid = r.astype(BF16)
    return hi, mid, (r - mid.astype(F32)).astype(BF16)


def _dotb(a, b):
    return jnp.dot(a, b, preferred_element_type=F32)


N_LEVELS = CHUNK.bit_length() - 1 - INV_BASE_SHIFT


def _chunk_tables():
    row = lax.broadcasted_iota(jnp.int32, (CHUNK, CHUNK), 0)
    col = lax.broadcasted_iota(jnp.int32, (CHUNK, CHUNK), 1)
    same = [(row >> s) == (col >> s) for s in range(INV_BASE_SHIFT, CHUNK.bit_length())]
    tab32 = jnp.stack([row == col, same[0]]).astype(F32)
    levels = [jnp.logical_and(hi, jnp.logical_not(lo)) for lo, hi in zip(same[:-1], same[1:])]
    tab16 = jnp.stack([row >= col] + levels).astype(BF16)
    return tab32, tab16


def _inv_unit_lower(mats, tab32_ref, tab16_ref, out):
    n = range(len(mats))
    eye = tab32_ref[0]
    d = [mats[i] * tab32_ref[1] for i in n]
    db = [d[i].astype(BF16) for i in n]
    t = [eye - d[i] for i in n]
    ab = [mats[i].astype(BF16) for i in n]
    p = [_dotb(db[i], db[i]) for i in n]
    yield
    t = [t[i] + _dotb(t[i].astype(BF16), p[i].astype(BF16)) for i in n]
    p = [_dotb(p[i].astype(BF16), p[i].astype(BF16)) for i in n]
    yield
    t = [t[i] + _dotb(t[i].astype(BF16), p[i].astype(BF16)) for i in n]
    yield
    for lvl in range(N_LEVELS):
        tb = [t[i].astype(BF16) for i in n]
        x = [_dotb(tb[i], ab[i] * tab16_ref[1 + lvl]) for i in n]
        yield
        t = [t[i] - _dotb(x[i].astype(BF16), tb[i]) for i in n]
        yield
    out.extend(t)


def _trace_interleaved(*gens, steps=None):
    live = list(zip(gens, steps or [1] * len(gens)))
    while live:
        for item in list(live):
            g, k = item
            try:
                for _ in range(k):
                    next(g)
            except StopIteration:
                live.remove(item)


def _gdn_prompt_body(q_ref, k_ref, v_ref, z_ref, ba_ref, cwq_ref, cwk_ref, cwv_ref, gp_ref, on_ref,
                     tab32_ref, tab16_ref, og_ref, s_ref, u_s, w_s, qd_s, at_s, kdt_s, gl_s, a_s, rhs_s):
    hq = pl.program_id(1)
    t_len = q_ref.shape[0]
    n_chunks = t_len // CHUNK
    row = lax.broadcasted_iota(jnp.int32, (CHUNK, CHUNK), 0)
    col = lax.broadcasted_iota(jnp.int32, (CHUNK, CHUNK), 1)
    causal = row >= col
    strict = row > col
    cwq = cwq_ref[...]
    cwk = cwk_ref[...]
    cwv = cwv_ref[...]
    neg_a = -jnp.exp(gp_ref[0:1, :])
    dt_bias = gp_ref[1:2, :]

    nv = HQ_PER_STEP * V_PER_QK
    n_iters = n_chunks // PH1_CHUNKS

    def chunk_prep(n, slot, base):
        r0 = pl.multiple_of(n * CHUNK, CHUNK)
        rows = pl.ds(r0, CHUNK)
        qc = _conv_silu_chunk(q_ref, cwq, n)
        yield
        kc = _conv_silu_chunk(k_ref, cwk, n)
        yield
        vc = _conv_silu_chunk(v_ref, cwv, n)
        yield
        ba = ba_ref[rows, :]
        beta_all = jax.nn.sigmoid(ba)
        xa = ba + dt_bias
        g_all = neg_a * (jnp.maximum(xa, 0.0) + jnp.log1p(jnp.exp(-jnp.abs(xa))))
        g_hi, g_mid, g_lo = _split3(g_all)
        tril = tab16_ref[0]
        gcum = _dotb(tril, g_hi) + (_dotb(tril, g_mid) + _dotb(tril, g_lo))
        gcum_t = gcum.T
        yield
        for q in range(HQ_PER_STEP):
            qh = qc[:, q * DK:(q + 1) * DK]
            kh = kc[:, q * DK:(q + 1) * DK]
            qn = qh * lax.rsqrt(jnp.sum(qh * qh, axis=-1, keepdims=True) + L2_EPS) * (DK ** -0.5)
            kn = kh * lax.rsqrt(jnp.sum(kh * kh, axis=-1, keepdims=True) + L2_EPS)
            kn_t = kn.T
            kn_tb = kn_t.astype(BF16)
            kk = _dotb(kn.astype(BF16), kn_tb)
            qk = _dotb(qn.astype(BF16), kn_tb)
            yield
            for j in range(V_PER_QK):
                h = q * V_PER_QK + j
                lane_b = (hq * HQ_PER_STEP + q) * V_PER_QK + j
                lane_a = V_HEADS + lane_b
                gcol = jnp.sum(jnp.where(col == lane_a, gcum, 0.0), axis=1, keepdims=True)
                bcol = jnp.sum(jnp.where(col == lane_b, beta_all, 0.0), axis=1, keepdims=True)
                grow = jnp.sum(jnp.where(row == lane_a, gcum_t, 0.0), axis=0, keepdims=True)
                glast = gcol[CHUNK - 1:CHUNK, :]
                decay = jnp.exp(jnp.where(causal, gcol - grow, -jnp.inf))
                a_mat = jnp.where(strict, kk * decay, 0.0) * bcol
                eg = jnp.exp(gcol)
                rhs = jnp.concatenate([vc[:, h * DV:(h + 1) * DV] * bcol, kn * (bcol * eg)], axis=1).astype(BF16)
                qd_s[h, rows, :] = (qn * eg).astype(BF16)
                at_s[h, rows, :] = (qk * decay).astype(BF16)
                kdt_s[h, rows, :] = (kn_t * jnp.exp(glast - grow)).astype(BF16)
                gl_s[h, pl.ds(n, 1), :] = jnp.broadcast_to(jnp.exp(glast), (1, DV))
                a_s[slot, base + h] = a_mat
                rhs_s[slot, base + h] = rhs
                yield

    def prep(it):
        it = jnp.asarray(it, jnp.int32)
        slot = it % 2
        for c in range(PH1_CHUNKS):
            yield from chunk_prep(PH1_CHUNKS * it + c, slot, c * nv)

    def solve(it):
        it = jnp.asarray(it, jnp.int32)
        slot = it % 2
        idx = range(PH1_CHUNKS * nv)
        t_mats = []
        yield from _inv_unit_lower([a_s[slot, i] for i in idx], tab32_ref, tab16_ref, t_mats)
        sols = [_dotb(t_mats[i].astype(BF16), rhs_s[slot, i]) for i in idx]
        yield
        for i in idx:
            rows = pl.ds(pl.multiple_of((PH1_CHUNKS * it + i // nv) * CHUNK, CHUNK), CHUNK)
            u_s[i % nv, rows, :] = sols[i][:, :DV]
            w_s[i % nv, rows, :] = sols[i][:, DV:].astype(BF16)
        yield

    gain = on_ref[...]
    heads = range(nv)

    def recur(it):
        it = jnp.asarray(it, jnp.int32)
        for c in range(PH1_CHUNKS):
            n = PH1_CHUNKS * it + c
            rows = pl.ds(pl.multiple_of(n * CHUNK, CHUNK), CHUNK)
            states = [s_ref[0, h] for h in heads]
            wq = [jnp.concatenate([w_s[h, rows, :], qd_s[h, rows, :]], axis=0) for h in heads]
            ws_qs = [_dotb(wq[h], states[h].astype(BF16)) for h in heads]
            yield
            v_new = [(u_s[h, rows, :] - ws_qs[h][:CHUNK]).astype(BF16) for h in heads]
            o = [ws_qs[h][CHUNK:] + _dotb(at_s[h, rows, :], v_new[h]) for h in heads]
            s_new = [states[h] * gl_s[h, pl.ds(n, 1), :] + _dotb(kdt_s[h, rows, :], v_new[h]) for h in heads]
            yield
            for h in heads:
                s_ref[0, h] = s_new[h]
                zg = z_ref[rows, h * DV:(h + 1) * DV]
                og = o[h] * lax.rsqrt(jnp.mean(o[h] * o[h], axis=-1, keepdims=True) + RMS_EPS) * gain * _silu(zg)
                og_ref[rows, h * DV:(h + 1) * DV] = og.astype(BF16)
            yield

    for h in heads:
        s_ref[0, h] = jnp.zeros((DK, DV), F32)
    _trace_interleaved(prep(0))
    _trace_interleaved(solve(0), prep(1), steps=PIPE_STEPS[:2])

    def steady(it, carry):
        _trace_interleaved(solve(it - 1), prep(it), recur(it - 2), steps=PIPE_STEPS)
        return carry

    lax.fori_loop(2, n_iters, steady, 0)
    _trace_interleaved(solve(n_iters - 1), recur(n_iters - 2), steps=(2, 1))
    _trace_interleaved(recur(n_iters - 1))


def _gdn_prompt(proj, ba, conv_w, gate_params, o_gain, *, batch, t_len):
    wq = HQ_PER_STEP * DK
    wv = HQ_PER_STEP * V_PER_QK * DV
    nv = HQ_PER_STEP * V_PER_QK
    in_specs = [
        pl.BlockSpec((t_len, wq), lambda b, h: (b, h)),
        pl.BlockSpec((t_len, wq), lambda b, h: (b, KEY_DIM // wq + h)),
        pl.BlockSpec((t_len, wv), lambda b, h: (b, (2 * KEY_DIM) // wv + h)),
        pl.BlockSpec((t_len, wv), lambda b, h: (b, CONV_DIM // wv + h)),
        pl.BlockSpec((t_len, 128), lambda b, h: (b, 0)),
        pl.BlockSpec((CONV_W, wq), lambda b, h: (0, h)),
        pl.BlockSpec((CONV_W, wq), lambda b, h: (0, KEY_DIM // wq + h)),
        pl.BlockSpec((CONV_W, wv), lambda b, h: (0, (2 * KEY_DIM) // wv + h)),
        pl.BlockSpec((2, 128), lambda b, h: (0, 0)),
        pl.BlockSpec((1, DV), lambda b, h: (0, 0)),
        pl.BlockSpec((2, CHUNK, CHUNK), lambda b, h: (0, 0, 0)),
        pl.BlockSpec((1 + N_LEVELS, CHUNK, CHUNK), lambda b, h: (0, 0, 0)),
    ]
    tab32, tab16 = _chunk_tables()
    out_shape = [
        jax.ShapeDtypeStruct((batch * t_len, VAL_DIM), BF16),
        jax.ShapeDtypeStruct((batch, V_HEADS, DK, DV), F32),
    ]
    out_specs = [
        pl.BlockSpec((t_len, wv), lambda b, h: (b, h)),
        pl.BlockSpec((1, nv, DK, DV), lambda b, h: (b, h, 0, 0)),
    ]
    scratch = [
        pltpu.VMEM((nv, t_len, DV), F32),
        pltpu.VMEM((nv, t_len, DK), BF16),
        pltpu.VMEM((nv, t_len, DK), BF16),
        pltpu.VMEM((nv, t_len, CHUNK), BF16),
        pltpu.VMEM((nv, t_len, CHUNK), BF16),
        pltpu.VMEM((nv, t_len // CHUNK, DV), F32),
        pltpu.VMEM((2, PH1_CHUNKS * nv, CHUNK, CHUNK), F32),
        pltpu.VMEM((2, PH1_CHUNKS * nv, CHUNK, DV + DK), BF16),
    ]
    return pl.pallas_call(
        _gdn_prompt_body,
        out_shape=out_shape,
        grid=(batch, QK_HEADS // HQ_PER_STEP),
        in_specs=in_specs,
        out_specs=out_specs,
        scratch_shapes=scratch,
        compiler_params=_params(("parallel", "arbitrary")),
        name="gdn_prompt",
    )(proj, proj, proj, proj, ba, conv_w, conv_w, conv_w, gate_params, o_gain.reshape(1, DV), tab32, tab16)


def _gdn_gates_body(ba_ref, gp_ref, beta_ref, eg_ref):
    ba = ba_ref[...]
    beta_ref[...] = jax.nn.sigmoid(ba)
    xa = ba + gp_ref[1:2, :]
    g = -jnp.exp(gp_ref[0:1, :]) * (jnp.maximum(xa, 0.0) + jnp.log1p(jnp.exp(-jnp.abs(xa))))
    eg_ref[...] = jnp.exp(g)


def _gdn_gates(ba, gate_params):
    m = ba.shape[0]
    return pl.pallas_call(
        _gdn_gates_body,
        out_shape=[jax.ShapeDtypeStruct((m, 128), F32)] * 2,
        name="gdn_gates",
    )(ba, gate_params)


def _gdn_decode_body(beta_ref, eg_ref, xs_ref, cw_ref, z_ref, on_ref, s_ref, so_ref, og_ref):
    for si in range(DEC_SAMPLES):
        b = pl.program_id(0) * DEC_SAMPLES + si
        y = xs_ref[si, 0] * cw_ref[0]
        for j in range(1, CONV_W):
            y = y + xs_ref[si, j] * cw_ref[j]
        y = _silu(y)
        q = y[0:QK_HEADS]
        k = y[QK_HEADS:2 * QK_HEADS]
        v = y[2 * QK_HEADS:]
        qn = q * lax.rsqrt(jnp.sum(q * q, axis=-1, keepdims=True) + L2_EPS) * (DK ** -0.5)
        kn = k * lax.rsqrt(jnp.sum(k * k, axis=-1, keepdims=True) + L2_EPS)
        qk_t = jnp.concatenate([qn, kn, jnp.zeros((DK - 2 * QK_HEADS, DK), F32)], axis=0).T
        for g0 in range(0, V_HEADS, DEC_GROUP):
            hs = range(g0, g0 + DEC_GROUP)
            qcol = {h: qk_t[:, h // V_PER_QK:h // V_PER_QK + 1] for h in hs}
            kcol = {h: qk_t[:, QK_HEADS + h // V_PER_QK:QK_HEADS + h // V_PER_QK + 1] for h in hs}
            sd = {h: s_ref[si, h] * eg_ref[b, h] for h in hs}
            ks = {h: jnp.sum(sd[h] * kcol[h], axis=0, keepdims=True) for h in hs}
            v_new = {h: beta_ref[b, h] * (v[h:h + 1, :] - ks[h]) for h in hs}
            sn = {h: sd[h] + kcol[h] * v_new[h] for h in hs}
            for h in hs:
                so_ref[si, h] = sn[h]
            o = {h: jnp.sum(sn[h] * qcol[h], axis=0, keepdims=True) for h in hs}
            for h in hs:
                og_ref[si, h:h + 1, :] = o[h]
        o_all = og_ref[si]
        og_ref[si] = (o_all * lax.rsqrt(jnp.mean(o_all * o_all, axis=-1, keepdims=True) + RMS_EPS)
                      * on_ref[...] * _silu(z_ref[si]))


def _gdn_decode(beta, eg, xs, conv_w3, z3, o_gain, state):
    nb = xs.shape[0]
    slots = CONV_DIM // DK
    smem = pl.BlockSpec(memory_space=pltpu.SMEM)
    return pl.pallas_call(
        _gdn_decode_body,
        out_shape=[
            jax.ShapeDtypeStruct((nb, V_HEADS, DK, DV), F32),
            jax.ShapeDtypeStruct((nb, V_HEADS, DV), F32),
        ],
        grid=(nb // DEC_SAMPLES,),
        in_specs=[
            smem,
            smem,
            pl.BlockSpec((DEC_SAMPLES, CONV_W, slots, DK), lambda b: (b, 0, 0, 0)),
            pl.BlockSpec((CONV_W, slots, DK), lambda b: (0, 0, 0)),
            pl.BlockSpec((DEC_SAMPLES, V_HEADS, DV), lambda b: (b, 0, 0)),
            pl.BlockSpec((1, DV), lambda b: (0, 0)),
            pl.BlockSpec((DEC_SAMPLES, V_HEADS, DK, DV), lambda b: (b, 0, 0, 0)),
        ],
        out_specs=[
            pl.BlockSpec((DEC_SAMPLES, V_HEADS, DK, DV), lambda b: (b, 0, 0, 0)),
            pl.BlockSpec((DEC_SAMPLES, V_HEADS, DV), lambda b: (b, 0, 0)),
        ],
        compiler_params=_params(("parallel",)),
        name="gdn_decode",
    )(beta, eg, xs, conv_w3, z3, o_gain.reshape(1, DV), state)


def _ssm_pow_body(lr_ref, li_ref, ldt_ref, pr_ref, pi_ref):
    n = jnp.where(lax.broadcasted_iota(jnp.int32, pr_ref.shape, 0) == 0, 1.0, float(SCAN_SEG))
    dt = jnp.exp(ldt_ref[...])
    lr = jnp.minimum(lr_ref[...], -1e-4) * dt
    li = li_ref[...] * dt
    mag = jnp.exp(n * lr)
    pr_ref[...] = mag * jnp.cos(n * li)
    pi_ref[...] = mag * jnp.sin(n * li)


def _ssm_pow(lr, li, ldt):
    gp = lr.shape[1]
    tl = 2048
    spec = pl.BlockSpec((1, tl), lambda i: (0, i))
    ospec = pl.BlockSpec((POW_ROWS, tl), lambda i: (0, i))
    return pl.pallas_call(
        _ssm_pow_body,
        out_shape=[jax.ShapeDtypeStruct((POW_ROWS, gp), F32)] * 2,
        grid=(gp // tl,),
        in_specs=[spec, spec, spec],
        out_specs=[ospec, ospec],
        name="ssm_pow",
    )(lr, li, ldt)


def _ssm_blocks_body(lr_ref, li_ref, ldt_ref, br_ref, bi_ref, cr_ref, ci_ref, bblk_ref, cblk_ref):
    dt = jnp.exp(ldt_ref[...])
    lr = jnp.minimum(lr_ref[...], -1e-4)
    li = li_ref[...]
    mag = jnp.exp(lr * dt)
    xr = mag * jnp.cos(li * dt) - 1.0
    xi = mag * jnp.sin(li * dt)
    den = 1.0 / (lr * lr + li * li)
    fr = (xr * lr + xi * li) * den
    fi = (xi * lr - xr * li) * den
    bblk_ref[...] = jnp.zeros_like(bblk_ref)
    cblk_ref[...] = jnp.zeros_like(cblk_ref)
    per_tile = 128 // SSM_STATE
    for g in range(GROUPS_PER_STEP):
        rows = slice(g * SSM_GROUP, (g + 1) * SSM_GROUP)
        off = (g // per_tile) * 256 + (g % per_tile) * SSM_STATE
        cols = slice(off, off + SSM_STATE)
        cols_im = slice(off + 128, off + 128 + SSM_STATE)
        br = br_ref[g]
        bi = bi_ref[g]
        frg = fr[g:g + 1, :]
        fig = fi[g:g + 1, :]
        bblk_ref[rows, cols] = (frg * br - fig * bi).astype(BF16)
        bblk_ref[rows, cols_im] = (frg * bi + fig * br).astype(BF16)
        cblk_ref[rows, cols] = cr_ref[g].astype(BF16)
        cblk_ref[rows, cols_im] = (-ci_ref[g]).astype(BF16)


def _ssm_blocks(lr, li, ldt, bt_re, bt_im, c_re, c_im):
    g, p = lr.shape
    k = GROUPS_PER_STEP
    s2 = pl.BlockSpec((k, p), lambda i: (i, 0))
    s1 = pl.BlockSpec((k, 1), lambda i: (i, 0))
    s3 = pl.BlockSpec((k, SSM_GROUP, p), lambda i: (i, 0, 0))
    blk = pl.BlockSpec((None, CH_PER_STEP, 2 * ST_PER_STEP), lambda i: (i, 0, 0))
    return pl.pallas_call(
        _ssm_blocks_body,
        out_shape=[jax.ShapeDtypeStruct((g // k, CH_PER_STEP, 2 * ST_PER_STEP), BF16)] * 2,
        grid=(g // k,),
        in_specs=[s2, s2, s1, s3, s3, s3, s3],
        out_specs=[blk, blk],
        name="ssm_blocks",
    )(lr, li, ldt, bt_re, bt_im, c_re, c_im)


def _dot_nt(a, b):
    return lax.dot_general(a, b, (((1,), (1,)), ((), ())), preferred_element_type=F32)


def _gelu_tanh(x):
    return 0.5 * x * (1.0 + jnp.tanh(math.sqrt(2.0 / math.pi) * (x + 0.044715 * (x * x * x))))


def _ssm_scan_body(ua_ref, ub_ref, b_ref, c_ref, pr_ref, pi_ref, d_ref, y_ref, hr_ref, hi_ref,
                   h_s, y_s, cr_s, ci_s):
    tb = pl.program_id(2)
    nt = ST_PER_STEP // 128
    u_perm = jnp.concatenate(
        [jnp.concatenate([ua_ref[pl.ds(i, 8, stride=SCAN_SEG), :], ub_ref[pl.ds(i, 8, stride=SCAN_SEG), :]], axis=1)
         for i in range(SCAN_SEG)], axis=0).astype(BF16)

    @pl.when(tb == 0)
    def _():
        cr_s[...] = jnp.zeros_like(cr_s)
        ci_s[...] = jnp.zeros_like(ci_s)

    def b_proj(c):
        bu = jnp.dot(u_perm, b_ref[:, c * 256:(c + 1) * 256], preferred_element_type=F32)
        h_s[c] = bu[:, :128]
        h_s[nt + c] = bu[:, 128:]

    row8 = lax.broadcasted_iota(jnp.int32, (8, 128), 0)
    b_proj(0)
    y = None
    for c in range(nt):
        if c + 1 < nt:
            b_proj(c + 1)
        lanes = slice(c * 128, (c + 1) * 128)
        ar = jnp.broadcast_to(pr_ref[0:1, lanes], (8, 128))
        ai = jnp.broadcast_to(pi_ref[0:1, lanes], (8, 128))
        hr = jnp.zeros((8, 128), F32)
        hi = jnp.zeros((8, 128), F32)
        for i in range(SCAN_SEG):
            xr = h_s[c, i * 8:(i + 1) * 8, :]
            xi = h_s[nt + c, i * 8:(i + 1) * 8, :]
            hr, hi = ar * hr - ai * hi + xr, ar * hi + ai * hr + xi

        asr = pr_ref[1:2, lanes]
        asi = pi_ref[1:2, lanes]
        c_r = cr_s[:, lanes]
        c_i = ci_s[:, lanes]
        cin_r = jnp.zeros((8, 128), F32)
        cin_i = jnp.zeros((8, 128), F32)
        for s in range(8):
            cin_r = jnp.where(row8 == s, c_r, cin_r)
            cin_i = jnp.where(row8 == s, c_i, cin_i)
            e_r = hr[s:s + 1, :]
            e_i = hi[s:s + 1, :]
            c_r, c_i = e_r + asr * c_r - asi * c_i, e_i + asr * c_i + asi * c_r
        cr_s[:, lanes] = c_r
        ci_s[:, lanes] = c_i

        hr, hi = cin_r, cin_i
        for i in range(SCAN_SEG):
            xr = h_s[c, i * 8:(i + 1) * 8, :]
            xi = h_s[nt + c, i * 8:(i + 1) * 8, :]
            hr, hi = ar * hr - ai * hi + xr, ar * hi + ai * hr + xi
            h_s[c, i * 8:(i + 1) * 8, :] = hr
            h_s[nt + c, i * 8:(i + 1) * 8, :] = hi

        h_cat = jnp.concatenate([h_s[c], h_s[nt + c]], axis=1).astype(BF16)
        y_c = _dot_nt(h_cat, c_ref[:, c * 256:(c + 1) * 256])
        y = y_c if y is None else y + y_c

    y_s[0] = y[:, :128]
    y_s[1] = y[:, 128:]
    per_seg = SCAN_SEG // 8
    y_nat = jnp.concatenate(
        [jnp.concatenate([y_s[h, pl.ds((j % per_seg) * 64 + j // per_seg, 8, stride=8), :] for h in range(2)], axis=1)
         for j in range(SCAN_BLOCK // 8)], axis=0)
    u = jnp.concatenate([ua_ref[...], ub_ref[...]], axis=1)
    y_ref[...] = _gelu_tanh(y_nat + d_ref[...] * u)

    @pl.when(tb == pl.num_programs(2) - 1)
    def _():
        hr_ref[...] = cr_s[...]
        hi_ref[...] = ci_s[...]


def _ssm_scan(uz, bblk, cblk, pw_r, pw_i, d_skip, *, batch, t_len):
    nt = t_len // SCAN_BLOCK
    return pl.pallas_call(
        _ssm_scan_body,
        out_shape=[
            jax.ShapeDtypeStruct((batch * t_len, SSM_WIDTH), F32),
            jax.ShapeDtypeStruct((batch, 1, SSM_GROUPS * SSM_STATE), F32),
            jax.ShapeDtypeStruct((batch, 1, SSM_GROUPS * SSM_STATE), F32),
        ],
        grid=(batch, N_GSTEPS, nt),
        in_specs=[
            pl.BlockSpec((SCAN_BLOCK, 128), lambda b, g, t: (b * nt + t, 2 * g)),
            pl.BlockSpec((SCAN_BLOCK, 128), lambda b, g, t: (b * nt + t, 2 * g + 1)),
            pl.BlockSpec((None, CH_PER_STEP, 2 * ST_PER_STEP), lambda b, g, t: (g, 0, 0)),
            pl.BlockSpec((None, CH_PER_STEP, 2 * ST_PER_STEP), lambda b, g, t: (g, 0, 0)),
            pl.BlockSpec((POW_ROWS, ST_PER_STEP), lambda b, g, t: (0, g)),
            pl.BlockSpec((POW_ROWS, ST_PER_STEP), lambda b, g, t: (0, g)),
            pl.BlockSpec((1, CH_PER_STEP), lambda b, g, t: (0, g)),
        ],
        out_specs=[
            pl.BlockSpec((SCAN_BLOCK, CH_PER_STEP), lambda b, g, t: (b * nt + t, g)),
            pl.BlockSpec((None, 1, ST_PER_STEP), lambda b, g, t: (b, 0, g)),
            pl.BlockSpec((None, 1, ST_PER_STEP), lambda b, g, t: (b, 0, g)),
        ],
        scratch_shapes=[
            pltpu.VMEM((2 * ST_PER_STEP // 128, SCAN_BLOCK, 128), F32),
            pltpu.VMEM((CH_PER_STEP // 128, SCAN_BLOCK, 128), F32),
            pltpu.VMEM((1, ST_PER_STEP), F32),
            pltpu.VMEM((1, ST_PER_STEP), F32),
        ],
        compiler_params=_params(("parallel", "parallel", "arbitrary")),
        name="ssm_scan",
    )(uz, uz, bblk, cblk, pw_r, pw_i, d_skip.reshape(1, SSM_WIDTH))


def _ssm_decode_body(u_ref, b_ref, c_ref, pr_ref, pi_ref, d_ref, h0r_ref, h0i_ref, y_ref, hr_ref, hi_ref):
    nt = ST_PER_STEP // 128
    u = u_ref[...]
    bu = jnp.dot(u.astype(BF16), b_ref[...], preferred_element_type=F32)
    h_cat = []
    for c in range(nt):
        lanes = slice(c * 128, (c + 1) * 128)
        ar = pr_ref[0:1, lanes]
        ai = pi_ref[0:1, lanes]
        h0r = h0r_ref[:, lanes]
        h0i = h0i_ref[:, lanes]
        hr = bu[:, c * 256:c * 256 + 128] + ar * h0r - ai * h0i
        hi = bu[:, c * 256 + 128:(c + 1) * 256] + ar * h0i + ai * h0r
        hr_ref[:, lanes] = hr
        hi_ref[:, lanes] = hi
        h_cat += [hr, hi]
    y = _dot_nt(jnp.concatenate(h_cat, axis=1).astype(BF16), c_ref[...])
    y_ref[...] = _gelu_tanh(y + d_ref[...] * u)


def _ssm_decode(uz, bblk, cblk, pw_r, pw_i, d_skip, h0r, h0i):
    nb = uz.shape[0]
    return pl.pallas_call(
        _ssm_decode_body,
        out_shape=[
            jax.ShapeDtypeStruct((nb, SSM_WIDTH), F32),
            jax.ShapeDtypeStruct((nb, SSM_GROUPS * SSM_STATE), F32),
            jax.ShapeDtypeStruct((nb, SSM_GROUPS * SSM_STATE), F32),
        ],
        grid=(N_GSTEPS,),
        in_specs=[
            pl.BlockSpec((nb, CH_PER_STEP), lambda g: (0, g)),
            pl.BlockSpec((None, CH_PER_STEP, 2 * ST_PER_STEP), lambda g: (g, 0, 0)),
            pl.BlockSpec((None, CH_PER_STEP, 2 * ST_PER_STEP), lambda g: (g, 0, 0)),
            pl.BlockSpec((POW_ROWS, ST_PER_STEP), lambda g: (0, g)),
            pl.BlockSpec((POW_ROWS, ST_PER_STEP), lambda g: (0, g)),
            pl.BlockSpec((1, CH_PER_STEP), lambda g: (0, g)),
            pl.BlockSpec((nb, ST_PER_STEP), lambda g: (0, g)),
            pl.BlockSpec((nb, ST_PER_STEP), lambda g: (0, g)),
        ],
        out_specs=[
            pl.BlockSpec((nb, CH_PER_STEP), lambda g: (0, g)),
            pl.BlockSpec((nb, ST_PER_STEP), lambda g: (0, g)),
            pl.BlockSpec((nb, ST_PER_STEP), lambda g: (0, g)),
        ],
        compiler_params=_params(("parallel",)),
        name="ssm_decode",
    )(uz, bblk, cblk, pw_r, pw_i, d_skip.reshape(1, SSM_WIDTH), h0r, h0i)


GLU_SLAB = 256


def _glu_body(y_ref, w_ref, b_ref, z_ref, o_ref, yb_scr, *, tn):
    j = pl.program_id(1)

    @pl.when(j == 0)
    def _():
        yb_scr[...] = y_ref[...].astype(BF16)

    for c in range(tn // GLU_SLAB):
        cols = slice(c * GLU_SLAB, (c + 1) * GLU_SLAB)
        t = jnp.dot(yb_scr[...], w_ref[:, cols], preferred_element_type=F32) + b_ref[:, cols]
        y = y_ref[:, pl.ds(pl.multiple_of(j * tn + c * GLU_SLAB, GLU_SLAB), GLU_SLAB)]
        o_ref[:, cols] = (y * jax.nn.sigmoid(t) * _silu(z_ref[:, cols])).astype(BF16)


def _glu(y, w, b, uz, *, tm, tn):
    m, d = y.shape
    zoff = SSM_WIDTH // tn
    return pl.pallas_call(
        functools.partial(_glu_body, tn=tn),
        out_shape=jax.ShapeDtypeStruct((m, d), BF16),
        grid=(m // tm, d // tn),
        in_specs=[
            pl.BlockSpec((tm, d), lambda i, j: (i, 0)),
            pl.BlockSpec((d, tn), lambda i, j: (0, j)),
            pl.BlockSpec((1, tn), lambda i, j: (0, j)),
            pl.BlockSpec((tm, tn), lambda i, j: (i, zoff + j)),
        ],
        out_specs=pl.BlockSpec((tm, tn), lambda i, j: (i, j)),
        scratch_shapes=[pltpu.VMEM((tm, d), BF16)],
        compiler_params=_params(("parallel", "arbitrary")),
        name="ssm_glu",
    )(y, w, b.reshape(1, d), uz)


def _prep_weights(norm_gdn, w_in_gdn, conv_gdn, a_log_gdn, dt_bias_gdn, onorm_gdn, w_out_gdn,
                  norm_ssm, w_in_ssm, lam_re, lam_im, b_re, b_im, c_re, c_im, d_ssm, log_dt_ssm,
                  w_glu_ssm, b_glu_ssm, w_out_ssm, norm_final):
    w_in_t = jnp.swapaxes(w_in_gdn[0], 0, 1)
    w_ba = jnp.pad(w_in_t[MAIN_DIM:], ((0, 128 - 2 * V_HEADS), (0, 0))).astype(BF16)
    zeros = jnp.zeros((V_HEADS,), F32)
    pad = jnp.zeros((128 - 2 * V_HEADS,), F32)
    gate_params = jnp.stack([jnp.concatenate([zeros, a_log_gdn[0], pad]),
                             jnp.concatenate([zeros, dt_bias_gdn[0], pad])])

    lr, li, ldt = lam_re[0], lam_im[0], log_dt_ssm[0]
    gp = SSM_GROUPS * SSM_STATE
    pw_r, pw_i = _ssm_pow(lr.reshape(1, gp), li.reshape(1, gp),
                          jnp.repeat(ldt, SSM_STATE).reshape(1, gp))
    bblk, cblk = _ssm_blocks(lr, li, ldt.reshape(SSM_GROUPS, 1),
                                 jnp.swapaxes(b_re[0], 1, 2), jnp.swapaxes(b_im[0], 1, 2), c_re[0], c_im[0])
    return dict(
        norm_gdn=norm_gdn[0], w_main=w_in_t, w_ba=w_ba, conv_w=conv_gdn[0], gate_params=gate_params,
        o_gain=onorm_gdn[0], w_out_gdn=w_out_gdn[0].astype(BF16),
        norm_ssm=norm_ssm[0], w_in_ssm=w_in_ssm[0], pw_r=pw_r, pw_i=pw_i, bblk=bblk, cblk=cblk,
        d_skip=d_ssm[0], w_glu=w_glu_ssm[0].astype(BF16), b_glu=b_glu_ssm[0], w_out_ssm=w_out_ssm[0].astype(BF16),
        norm_final=norm_final)


def _ssm_tail(p, x1, y, uz, *, tm, tm_glu):
    y3 = _glu(y, p["w_glu"], p["b_glu"], uz, tm=tm_glu, tn=1024)
    return _outproj(y3, p["w_out_ssm"], x1, p["norm_final"], tm=tm, name="ssm_out")


def _prompt_path(p, x_prompt):
    batch, t_len, d = x_prompt.shape
    x = x_prompt.reshape(batch * t_len, d)
    proj, ba = _norm_matmul(x, p["norm_gdn"], p["w_main"], p["w_ba"], n=MAIN_DIM, tm=2048, tn=512, name="gdn_in",
                            w_transposed=True)
    og, delta = _gdn_prompt(proj, ba, p["conv_w"], p["gate_params"], p["o_gain"], batch=batch, t_len=t_len)
    conv_state = proj.reshape(batch, t_len, MAIN_DIM)[:, t_len - (CONV_W - 1):, :CONV_DIM]
    x1 = _outproj(og, p["w_out_gdn"], x, tm=512, name="gdn_out")
    uz = _norm_matmul(x1, p["norm_ssm"], p["w_in_ssm"], n=2 * SSM_WIDTH, tm=2048, tn=512, name="ssm_in")
    y, h_re, h_im = _ssm_scan(uz, p["bblk"], p["cblk"], p["pw_r"], p["pw_i"], p["d_skip"],
                              batch=batch, t_len=t_len)
    out = _ssm_tail(p, x1, y, uz, tm=512, tm_glu=512)
    return (out.reshape(batch, t_len, d), conv_state[None], delta[None],
            h_re.reshape(1, batch, SSM_GROUPS, SSM_STATE), h_im.reshape(1, batch, SSM_GROUPS, SSM_STATE))


def _sample_path(p, x_sample, state_conv, state_delta, state_re, state_im):
    nb, _, d = x_sample.shape
    x = x_sample.reshape(nb, d)
    proj, ba = _norm_matmul(x, p["norm_gdn"], p["w_main"], p["w_ba"], n=MAIN_DIM, tm=nb, tn=1024, name="gdn_in_s",
                            w_transposed=True)
    beta, eg = _gdn_gates(ba, p["gate_params"])
    xs = jnp.concatenate([state_conv[0], proj[:, None, :CONV_DIM]], axis=1)
    slots = CONV_DIM // DK
    delta, og = _gdn_decode(beta[:, :V_HEADS], eg[:, V_HEADS:2 * V_HEADS],
                            xs.reshape(nb, CONV_W, slots, DK), p["conv_w"].reshape(CONV_W, slots, DK),
                            proj[:, CONV_DIM:].reshape(nb, V_HEADS, DV), p["o_gain"], state_delta[0])
    x1 = _outproj(og.reshape(nb, VAL_DIM).astype(BF16), p["w_out_gdn"], x, tm=nb, name="gdn_out_s")
    uz = _norm_matmul(x1, p["norm_ssm"], p["w_in_ssm"], n=2 * SSM_WIDTH, tm=nb, tn=1024, name="ssm_in_s")
    gp = SSM_GROUPS * SSM_STATE
    y, h_re, h_im = _ssm_decode(uz, p["bblk"], p["cblk"], p["pw_r"], p["pw_i"], p["d_skip"],
                                state_re[0].reshape(nb, gp), state_im[0].reshape(nb, gp))
    out = _ssm_tail(p, x1, y, uz, tm=nb, tm_glu=nb)
    return (out.reshape(nb, 1, d), xs[:, 1:][None], delta[None],
            h_re.reshape(1, nb, SSM_GROUPS, SSM_STATE), h_im.reshape(1, nb, SSM_GROUPS, SSM_STATE))


def kernel(x_prompt, x_sample, state_gdn_conv, state_gdn_delta, state_ssm_re, state_ssm_im, norm_gdn, w_in_gdn, conv_gdn, a_log_gdn, dt_bias_gdn, onorm_gdn, w_out_gdn, norm_ssm, w_in_ssm, lam_re, lam_im, b_re, b_im, c_re, c_im, d_ssm, log_dt_ssm, w_glu_ssm, b_glu_ssm, w_out_ssm, norm_final):
    p = _prep_weights(norm_gdn, w_in_gdn, conv_gdn, a_log_gdn, dt_bias_gdn, onorm_gdn, w_out_gdn,
                      norm_ssm, w_in_ssm, lam_re, lam_im, b_re, b_im, c_re, c_im, d_ssm, log_dt_ssm,
                      w_glu_ssm, b_glu_ssm, w_out_ssm, norm_final)
    y_p, conv_p, delta_p, re_p, im_p = _prompt_path(p, x_prompt)
    y_s, conv_s, delta_s, re_s, im_s = _sample_path(p, x_sample, state_gdn_conv, state_gdn_delta,
                                                    state_ssm_re, state_ssm_im)
    return (y_p, y_s, conv_p, delta_p, re_p, im_p, conv_s, delta_s, re_s, im_s)
```

```python
import functools
import math

import jax
import jax.numpy as jnp
from jax import lax
from jax.experimental import pallas as pl
from jax.experimental.pallas import tpu as pltpu

F32 = jnp.float32
BF16 = jnp.bfloat16

RMS_EPS = 1e-6
L2_EPS = 1e-6

D_MODEL = 2048
QK_HEADS = 16
V_HEADS = 32
DK = 128
DV = 128
KEY_DIM = QK_HEADS * DK
VAL_DIM = V_HEADS * DV
CONV_DIM = 2 * KEY_DIM + VAL_DIM
CONV_W = 4
MAIN_DIM = CONV_DIM + VAL_DIM

SSM_WIDTH = 4096
SSM_GROUP = 16
SSM_GROUPS = SSM_WIDTH // SSM_GROUP
SSM_STATE = 64
GROUPS_PER_STEP = 16
CH_PER_STEP = GROUPS_PER_STEP * SSM_GROUP
ST_PER_STEP = GROUPS_PER_STEP * SSM_STATE
N_GSTEPS = SSM_GROUPS // GROUPS_PER_STEP

CHUNK = 128
SCAN_BLOCK = 2048
SCAN_SEG = SCAN_BLOCK // 8
POW_ROWS = 8
PROJ_ROWS = 1024

VMEM_LIMIT = 56 * 1024 * 1024


def _params(sem):
    return pltpu.CompilerParams(dimension_semantics=sem, vmem_limit_bytes=VMEM_LIMIT)


def _silu(x):
    return x * jax.nn.sigmoid(x)


def _rms(x, g):
    ms = jnp.mean(x * x, axis=-1, keepdims=True)
    return x * lax.rsqrt(ms + RMS_EPS) * g


def _norm_matmul_body(x_ref, g_ref, w_ref, *rest, has_aux, w_transposed):
    if has_aux:
        waux_ref, o_ref, aux_ref, h_scr = rest
    else:
        o_ref, h_scr = rest
    mm = _dot_nt if w_transposed else _dotb

    @pl.when(pl.program_id(1) == 0)
    def _():
        h = _rms(x_ref[...], g_ref[...]).astype(BF16)
        h_scr[...] = h
        if has_aux:
            aux_ref[...] = mm(h, waux_ref[...])

    o_ref[...] = mm(h_scr[...], w_ref[...].astype(BF16))


def _norm_matmul(x, g, w, w_aux=None, *, n, tm, tn, name, w_transposed=False):
    m, d = x.shape
    has_aux = w_aux is not None
    in_specs = [
        pl.BlockSpec((tm, d), lambda i, j: (i, 0), pipeline_mode=pl.Buffered(1)),
        pl.BlockSpec((1, d), lambda i, j: (0, 0)),
        pl.BlockSpec((tn, d), lambda i, j: (j, 0)) if w_transposed else pl.BlockSpec((d, tn), lambda i, j: (0, j)),
    ]
    out_shape = [jax.ShapeDtypeStruct((m, n), F32)]
    out_specs = [pl.BlockSpec((tm, tn), lambda i, j: (i, j))]
    args = [x, g.reshape(1, d), w]
    if has_aux:
        na = w_aux.shape[0] if w_transposed else w_aux.shape[1]
        in_specs.append(pl.BlockSpec(w_aux.shape, lambda i, j: (0, 0)))
        out_shape.append(jax.ShapeDtypeStruct((m, na), F32))
        out_specs.append(pl.BlockSpec((tm, na), lambda i, j: (i, 0)))
        args.append(w_aux)
    res = pl.pallas_call(
        functools.partial(_norm_matmul_body, has_aux=has_aux, w_transposed=w_transposed),
        out_shape=out_shape,
        grid=(m // tm, n // tn),
        in_specs=in_specs,
        out_specs=out_specs,
        scratch_shapes=[pltpu.VMEM((tm, d), BF16)],
        compiler_params=_params(("parallel", "arbitrary")),
        name=name,
    )(*args)
    return res if has_aux else res[0]


def _outproj_body(a_ref, w_ref, x_ref, *rest, final_norm):
    if final_norm:
        g_ref, o_ref = rest
    else:
        (o_ref,) = rest
    y = x_ref[...] + jnp.dot(a_ref[...], w_ref[...], preferred_element_type=F32)
    o_ref[...] = _rms(y, g_ref[...]) if final_norm else y


def _outproj(a, w, x, g=None, *, tm, name):
    m, kd = a.shape
    n = w.shape[1]
    final_norm = g is not None
    in_specs = [
        pl.BlockSpec((tm, kd), lambda i: (i, 0)),
        pl.BlockSpec((kd, n), lambda i: (0, 0), pipeline_mode=pl.Buffered(1)),
        pl.BlockSpec((tm, n), lambda i: (i, 0)),
    ]
    args = [a, w, x]
    if final_norm:
        in_specs.append(pl.BlockSpec((1, n), lambda i: (0, 0)))
        args.append(g.reshape(1, n))
    return pl.pallas_call(
        functools.partial(_outproj_body, final_norm=final_norm),
        out_shape=jax.ShapeDtypeStruct((m, n), F32),
        grid=(m // tm,),
        in_specs=in_specs,
        out_specs=pl.BlockSpec((tm, n), lambda i: (i, 0)),
        compiler_params=_params(("parallel",)),
        name=name,
    )(*args)


def _conv_silu_chunk(x_ref, cw, n):
    cc = x_ref.shape[1]
    r0 = pl.multiple_of(n * CHUNK, CHUNK)
    cur = x_ref[pl.ds(r0, CHUNK), :]
    p0 = pl.multiple_of(jnp.maximum(r0 - 8, 0), 8)
    prev = jnp.where(n > 0, x_ref[pl.ds(p0, 8), :], 0.0)
    row8 = lax.broadcasted_iota(jnp.int32, (8, cc), 0)
    y = cur * cw[CONV_W - 1:CONV_W, :]
    for s in range(1, CONV_W):
        sh = pltpu.roll(cur, s, 0)
        top = jnp.where(row8 < s, pltpu.roll(prev, s, 0), sh[0:8, :])
        sh = jnp.concatenate([top, sh[8:, :]], axis=0)
        y = y + sh * cw[CONV_W - 1 - s:CONV_W - s, :]
    return _silu(y)


INV_BASE_SHIFT = 3
V_PER_QK = V_HEADS // QK_HEADS
HQ_PER_STEP = 2
DEC_GROUP = 8
DEC_SAMPLES = 4
PH1_CHUNKS = 2
PIPE_STEPS = (1, 2, 1)


def _split3(x):
    hi = x.astype(BF16)
    r = x - hi.astype(F32)
    mid = r.astype(BF16)
    return hi, mid, (r - mid.astype(F32)).astype(BF16)


def _dotb(a, b):
    return jnp.dot(a, b, preferred_element_type=F32)


N_LEVELS = CHUNK.bit_length() - 1 - INV_BASE_SHIFT


def _chunk_tables():
    row = lax.broadcasted_iota(jnp.int32, (CHUNK, CHUNK), 0)
    col = lax.broadcasted_iota(jnp.int32, (CHUNK, CHUNK), 1)
    same = [(row >> s) == (col >> s) for s in range(INV_BASE_SHIFT, CHUNK.bit_length())]
    tab32 = jnp.stack([row == col, same[0]]).astype(F32)
    levels = [jnp.logical_and(hi, jnp.logical_not(lo)) for lo, hi in zip(same[:-1], same[1:])]
    tab16 = jnp.stack([row >= col] + levels).astype(BF16)
    return tab32, tab16


def _inv_unit_lower(mats, tab32_ref, tab16_ref, out):
    n = range(len(mats))
    eye = tab32_ref[0]
    d = [mats[i] * tab32_ref[1] for i in n]
    db = [d[i].astype(BF16) for i in n]
    t = [eye - d[i] for i in n]
    ab = [mats[i].astype(BF16) for i in n]
    p = [_dotb(db[i], db[i]) for i in n]
    yield
    t = [t[i] + _dotb(t[i].astype(BF16), p[i].astype(BF16)) for i in n]
    p = [_dotb(p[i].astype(BF16), p[i].astype(BF16)) for i in n]
    yield
    t = [t[i] + _dotb(t[i].astype(BF16), p[i].astype(BF16)) for i in n]
    yield
    for lvl in range(N_LEVELS):
        tb = [t[i].astype(BF16) for i in n]
        x = [_dotb(tb[i], ab[i] * tab16_ref[1 + lvl]) for i in n]
        yield
        t = [t[i] - _dotb(x[i].astype(BF16), tb[i]) for i in n]
        yield
    out.extend(t)


def _trace_interleaved(*gens, steps=None):
    live = list(zip(gens, steps or [1] * len(gens)))
    while live:
        for item in list(live):
            g, k = item
            try:
                for _ in range(k):
                    next(g)
            except StopIteration:
                live.remove(item)


def _gdn_prompt_body(q_ref, k_ref, v_ref, z_ref, ba_ref, cwq_ref, cwk_ref, cwv_ref, gp_ref, on_ref,
                     tab32_ref, tab16_ref, og_ref, s_ref, u_s, w_s, qd_s, at_s, kdt_s, gl_s, a_s, rhs_s):
    hq = pl.program_id(1)
    t_len = q_ref.shape[0]
    n_chunks = t_len // CHUNK
    row = lax.broadcasted_iota(jnp.int32, (CHUNK, CHUNK), 0)
    col = lax.broadcasted_iota(jnp.int32, (CHUNK, CHUNK), 1)
    causal = row >= col
    strict = row > col
    cwq = cwq_ref[...]
    cwk = cwk_ref[...]
    cwv = cwv_ref[...]
    neg_a = -jnp.exp(gp_ref[0:1, :])
    dt_bias = gp_ref[1:2, :]

    nv = HQ_PER_STEP * V_PER_QK
    n_iters = n_chunks // PH1_CHUNKS

    def chunk_prep(n, slot, base):
        r0 = pl.multiple_of(n * CHUNK, CHUNK)
        rows = pl.ds(r0, CHUNK)
        qc = _conv_silu_chunk(q_ref, cwq, n)
        yield
        kc = _conv_silu_chunk(k_ref, cwk, n)
        yield
        vc = _conv_silu_chunk(v_ref, cwv, n)
        yield
        ba = ba_ref[rows, :]
        beta_all = jax.nn.sigmoid(ba)
        xa = ba + dt_bias
        g_all = neg_a * (jnp.maximum(xa, 0.0) + jnp.log1p(jnp.exp(-jnp.abs(xa))))
        g_hi, g_mid, g_lo = _split3(g_all)
        tril = tab16_ref[0]
        gcum = _dotb(tril, g_hi) + (_dotb(tril, g_mid) + _dotb(tril, g_lo))
        gcum_t = gcum.T
        yield
        for q in range(HQ_PER_STEP):
            qh = qc[:, q * DK:(q + 1) * DK]
            kh = kc[:, q * DK:(q + 1) * DK]
            qn = qh * lax.rsqrt(jnp.sum(qh * qh, axis=-1, keepdims=True) + L2_EPS) * (DK ** -0.5)
            kn = kh * lax.rsqrt(jnp.sum(kh * kh, axis=-1, keepdims=True) + L2_EPS)
            kn_t = kn.T
            kn_tb = kn_t.astype(BF16)
            kk = _dotb(kn.astype(BF16), kn_tb)
            qk = _dotb(qn.astype(BF16), kn_tb)
            yield
            for j in range(V_PER_QK):
                h = q * V_PER_QK + j
                lane_b = (hq * HQ_PER_STEP + q) * V_PER_QK + j
                lane_a = V_HEADS + lane_b
                gcol = jnp.sum(jnp.where(col == lane_a, gcum, 0.0), axis=1, keepdims=True)
                bcol = jnp.sum(jnp.where(col == lane_b, beta_all, 0.0), axis=1, keepdims=True)
                grow = jnp.sum(jnp.where(row == lane_a, gcum_t, 0.0), axis=0, keepdims=True)
                glast = gcol[CHUNK - 1:CHUNK, :]
                decay = jnp.exp(jnp.where(causal, gcol - grow, -jnp.inf))
                a_mat = jnp.where(strict, kk * decay, 0.0) * bcol
                eg = jnp.exp(gcol)
                rhs = jnp.concatenate([vc[:, h * DV:(h + 1) * DV] * bcol, kn * (bcol * eg)], axis=1).astype(BF16)
                qd_s[h, rows, :] = (qn * eg).astype(BF16)
                at_s[h, rows, :] = (qk * decay).astype(BF16)
                kdt_s[h, rows, :] = (kn_t * jnp.exp(glast - grow)).astype(BF16)
                gl_s[h, pl.ds(n, 1), :] = jnp.broadcast_to(jnp.exp(glast), (1, DV))
                a_s[slot, base + h] = a_mat
                rhs_s[slot, base + h] = rhs
                yield

    def prep(it):
        it = jnp.asarray(it, jnp.int32)
        slot = it % 2
        for c in range(PH1_CHUNKS):
            yield from chunk_prep(PH1_CHUNKS * it + c, slot, c * nv)

    def solve(it):
        it = jnp.asarray(it, jnp.int32)
        slot = it % 2
        idx = range(PH1_CHUNKS * nv)
        t_mats = []
        yield from _inv_unit_lower([a_s[slot, i] for i in idx], tab32_ref, tab16_ref, t_mats)
        sols = [_dotb(t_mats[i].astype(BF16), rhs_s[slot, i]) for i in idx]
        yield
        for i in idx:
            rows = pl.ds(pl.multiple_of((PH1_CHUNKS * it + i // nv) * CHUNK, CHUNK), CHUNK)
            u_s[i % nv, rows, :] = sols[i][:, :DV]
            w_s[i % nv, rows, :] = sols[i][:, DV:].astype(BF16)
        yield

    gain = on_ref[...]
    heads = range(nv)

    def recur(it):
        it = jnp.asarray(it, jnp.int32)
        for c in range(PH1_CHUNKS):
            n = PH1_CHUNKS * it + c
            rows = pl.ds(pl.multiple_of(n * CHUNK, CHUNK), CHUNK)
            states = [s_ref[0, h] for h in heads]
            wq = [jnp.concatenate([w_s[h, rows, :], qd_s[h, rows, :]], axis=0) for h in heads]
            ws_qs = [_dotb(wq[h], states[h].astype(BF16)) for h in heads]
            yield
            v_new = [(u_s[h, rows, :] - ws_qs[h][:CHUNK]).astype(BF16) for h in heads]
            o = [ws_qs[h][CHUNK:] + _dotb(at_s[h, rows, :], v_new[h]) for h in heads]
            s_new = [states[h] * gl_s[h, pl.ds(n, 1), :] + _dotb(kdt_s[h, rows, :], v_new[h]) for h in heads]
            yield
            for h in heads:
                s_ref[0, h] = s_new[h]
                zg = z_ref[rows, h * DV:(h + 1) * DV]
                og = o[h] * lax.rsqrt(jnp.mean(o[h] * o[h], axis=-1, keepdims=True) + RMS_EPS) * gain * _silu(zg)
                og_ref[rows, h * DV:(h + 1) * DV] = og.astype(BF16)
            yield

    for h in heads:
        s_ref[0, h] = jnp.zeros((DK, DV), F32)
    _trace_interleaved(prep(0))
    _trace_interleaved(solve(0), prep(1), steps=PIPE_STEPS[:2])

    def steady(it, carry):
        _trace_interleaved(solve(it - 1), prep(it), recur(it - 2), steps=PIPE_STEPS)
        return carry

    lax.fori_loop(2, n_iters, steady, 0)
    _trace_interleaved(solve(n_iters - 1), recur(n_iters - 2), steps=(2, 1))
    _trace_interleaved(recur(n_iters - 1))


def _gdn_prompt(proj, ba, conv_w, gate_params, o_gain, *, batch, t_len):
    wq = HQ_PER_STEP * DK
    wv = HQ_PER_STEP * V_PER_QK * DV
    nv = HQ_PER_STEP * V_PER_QK
    in_specs = [
        pl.BlockSpec((t_len, wq), lambda b, h: (b, h)),
        pl.BlockSpec((t_len, wq), lambda b, h: (b, KEY_DIM // wq + h)),
        pl.BlockSpec((t_len, wv), lambda b, h: (b, (2 * KEY_DIM) // wv + h)),
        pl.BlockSpec((t_len, wv), lambda b, h: (b, CONV_DIM // wv + h)),
        pl.BlockSpec((t_len, 128), lambda b, h: (b, 0)),
        pl.BlockSpec((CONV_W, wq), lambda b, h: (0, h)),
        pl.BlockSpec((CONV_W, wq), lambda b, h: (0, KEY_DIM // wq + h)),
        pl.BlockSpec((CONV_W, wv), lambda b, h: (0, (2 * KEY_DIM) // wv + h)),
        pl.BlockSpec((2, 128), lambda b, h: (0, 0)),
        pl.BlockSpec((1, DV), lambda b, h: (0, 0)),
        pl.BlockSpec((2, CHUNK, CHUNK), lambda b, h: (0, 0, 0)),
        pl.BlockSpec((1 + N_LEVELS, CHUNK, CHUNK), lambda b, h: (0, 0, 0)),
    ]
    tab32, tab16 = _chunk_tables()
    out_shape = [
        jax.ShapeDtypeStruct((batch * t_len, VAL_DIM), BF16),
        jax.ShapeDtypeStruct((batch, V_HEADS, DK, DV), F32),
    ]
    out_specs = [
        pl.BlockSpec((t_len, wv), lambda b, h: (b, h)),
        pl.BlockSpec((1, nv, DK, DV), lambda b, h: (b, h, 0, 0)),
    ]
    scratch = [
        pltpu.VMEM((nv, t_len, DV), F32),
        pltpu.VMEM((nv, t_len, DK), BF16),
        pltpu.VMEM((nv, t_len, DK), BF16),
        pltpu.VMEM((nv, t_len, CHUNK), BF16),
        pltpu.VMEM((nv, t_len, CHUNK), BF16),
        pltpu.VMEM((nv, t_len // CHUNK, DV), F32),
        pltpu.VMEM((2, PH1_CHUNKS * nv, CHUNK, CHUNK), F32),
        pltpu.VMEM((2, PH1_CHUNKS * nv, CHUNK, DV + DK), BF16),
    ]
    return pl.pallas_call(
        _gdn_prompt_body,
        out_shape=out_shape,
        grid=(batch, QK_HEADS // HQ_PER_STEP),
        in_specs=in_specs,
        out_specs=out_specs,
        scratch_shapes=scratch,
        compiler_params=_params(("parallel", "arbitrary")),
        name="gdn_prompt",
    )(proj, proj, proj, proj, ba, conv_w, conv_w, conv_w, gate_params, o_gain.reshape(1, DV), tab32, tab16)


def _gdn_gates_body(ba_ref, gp_ref, beta_ref, eg_ref):
    ba = ba_ref[...]
    beta_ref[...] = jax.nn.sigmoid(ba)
    xa = ba + gp_ref[1:2, :]
    g = -jnp.exp(gp_ref[0:1, :]) * (jnp.maximum(xa, 0.0) + jnp.log1p(jnp.exp(-jnp.abs(xa))))
    eg_ref[...] = jnp.exp(g)


def _gdn_gates(ba, gate_params):
    m = ba.shape[0]
    return pl.pallas_call(
        _gdn_gates_body,
        out_shape=[jax.ShapeDtypeStruct((m, 128), F32)] * 2,
        name="gdn_gates",
    )(ba, gate_params)


def _gdn_decode_body(beta_ref, eg_ref, xs_ref, cw_ref, z_ref, on_ref, s_ref, so_ref, og_ref):
    for si in range(DEC_SAMPLES):
        b = pl.program_id(0) * DEC_SAMPLES + si
        y = xs_ref[si, 0] * cw_ref[0]
        for j in range(1, CONV_W):
            y = y + xs_ref[si, j] * cw_ref[j]
        y = _silu(y)
        q = y[0:QK_HEADS]
        k = y[QK_HEADS:2 * QK_HEADS]
        v = y[2 * QK_HEADS:]
        qn = q * lax.rsqrt(jnp.sum(q * q, axis=-1, keepdims=True) + L2_EPS) * (DK ** -0.5)
        kn = k * lax.rsqrt(jnp.sum(k * k, axis=-1, keepdims=True) + L2_EPS)
        qk_t = jnp.concatenate([qn, kn, jnp.zeros((DK - 2 * QK_HEADS, DK), F32)], axis=0).T
        for g0 in range(0, V_HEADS, DEC_GROUP):
            hs = range(g0, g0 + DEC_GROUP)
            qcol = {h: qk_t[:, h // V_PER_QK:h // V_PER_QK + 1] for h in hs}
            kcol = {h: qk_t[:, QK_HEADS + h // V_PER_QK:QK_HEADS + h // V_PER_QK + 1] for h in hs}
            sd = {h: s_ref[si, h] * eg_ref[b, h] for h in hs}
            ks = {h: jnp.sum(sd[h] * kcol[h], axis=0, keepdims=True) for h in hs}
            v_new = {h: beta_ref[b, h] * (v[h:h + 1, :] - ks[h]) for h in hs}
            sn = {h: sd[h] + kcol[h] * v_new[h] for h in hs}
            for h in hs:
                so_ref[si, h] = sn[h]
            o = {h: jnp.sum(sn[h] * qcol[h], axis=0, keepdims=True) for h in hs}
            for h in hs:
                og_ref[si, h:h + 1, :] = o[h]
        o_all = og_ref[si]
        og_ref[si] = (o_all * lax.rsqrt(jnp.mean(o_all * o_all, axis=-1, keepdims=True) + RMS_EPS)
                      * on_ref[...] * _silu(z_ref[si]))


def _gdn_decode(beta, eg, xs, conv_w3, z3, o_gain, state):
    nb = xs.shape[0]
    slots = CONV_DIM // DK
    smem = pl.BlockSpec(memory_space=pltpu.SMEM)
    return pl.pallas_call(
        _gdn_decode_body,
        out_shape=[
            jax.ShapeDtypeStruct((nb, V_HEADS, DK, DV), F32),
            jax.ShapeDtypeStruct((nb, V_HEADS, DV), F32),
        ],
        grid=(nb // DEC_SAMPLES,),
        in_specs=[
            smem,
            smem,
            pl.BlockSpec((DEC_SAMPLES, CONV_W, slots, DK), lambda b: (b, 0, 0, 0)),
            pl.BlockSpec((CONV_W, slots, DK), lambda b: (0, 0, 0)),
            pl.BlockSpec((DEC_SAMPLES, V_HEADS, DV), lambda b: (b, 0, 0)),
            pl.BlockSpec((1, DV), lambda b: (0, 0)),
            pl.BlockSpec((DEC_SAMPLES, V_HEADS, DK, DV), lambda b: (b, 0, 0, 0)),
        ],
        out_specs=[
            pl.BlockSpec((DEC_SAMPLES, V_HEADS, DK, DV), lambda b: (b, 0, 0, 0)),
            pl.BlockSpec((DEC_SAMPLES, V_HEADS, DV), lambda b: (b, 0, 0)),
        ],
        compiler_params=_params(("parallel",)),
        name="gdn_decode",
    )(beta, eg, xs, conv_w3, z3, o_gain.reshape(1, DV), state)


def _ssm_pow_body(lr_ref, li_ref, ldt_ref, pr_ref, pi_ref):
    n = jnp.where(lax.broadcasted_iota(jnp.int32, pr_ref.shape, 0) == 0, 1.0, float(SCAN_SEG))
    dt = jnp.exp(ldt_ref[...])
    lr = jnp.minimum(lr_ref[...], -1e-4) * dt
    li = li_ref[...] * dt
    mag = jnp.exp(n * lr)
    pr_ref[...] = mag * jnp.cos(n * li)
    pi_ref[...] = mag * jnp.sin(n * li)


def _ssm_pow(lr, li, ldt):
    gp = lr.shape[1]
    tl = 2048
    spec = pl.BlockSpec((1, tl), lambda i: (0, i))
    ospec = pl.BlockSpec((POW_ROWS, tl), lambda i: (0, i))
    return pl.pallas_call(
        _ssm_pow_body,
        out_shape=[jax.ShapeDtypeStruct((POW_ROWS, gp), F32)] * 2,
        grid=(gp // tl,),
        in_specs=[spec, spec, spec],
        out_specs=[ospec, ospec],
        name="ssm_pow",
    )(lr, li, ldt)


def _ssm_blocks_body(lr_ref, li_ref, ldt_ref, br_ref, bi_ref, cr_ref, ci_ref, bblk_ref, cblk_ref):
    dt = jnp.exp(ldt_ref[...])
    lr = jnp.minimum(lr_ref[...], -1e-4)
    li = li_ref[...]
    mag = jnp.exp(lr * dt)
    xr = mag * jnp.cos(li * dt) - 1.0
    xi = mag * jnp.sin(li * dt)
    den = 1.0 / (lr * lr + li * li)
    fr = (xr * lr + xi * li) * den
    fi = (xi * lr - xr * li) * den
    bblk_ref[...] = jnp.zeros_like(bblk_ref)
    cblk_ref[...] = jnp.zeros_like(cblk_ref)
    per_tile = 128 // SSM_STATE
    for g in range(GROUPS_PER_STEP):
        rows = slice(g * SSM_GROUP, (g + 1) * SSM_GROUP)
        off = (g // per_tile) * 256 + (g % per_tile) * SSM_STATE
        cols = slice(off, off + SSM_STATE)
        cols_im = slice(off + 128, off + 128 + SSM_STATE)
        br = br_ref[g]
        bi = bi_ref[g]
        frg = fr[g:g + 1, :]
        fig = fi[g:g + 1, :]
        bblk_ref[rows, cols] = (frg * br - fig * bi).astype(BF16)
        bblk_ref[rows, cols_im] = (frg * bi + fig * br).astype(BF16)
        cblk_ref[rows, cols] = cr_ref[g].astype(BF16)
        cblk_ref[rows, cols_im] = (-ci_ref[g]).astype(BF16)


def _ssm_blocks(lr, li, ldt, bt_re, bt_im, c_re, c_im):
    g, p = lr.shape
    k = GROUPS_PER_STEP
    s2 = pl.BlockSpec((k, p), lambda i: (i, 0))
    s1 = pl.BlockSpec((k, 1), lambda i: (i, 0))
    s3 = pl.BlockSpec((k, SSM_GROUP, p), lambda i: (i, 0, 0))
    blk = pl.BlockSpec((None, CH_PER_STEP, 2 * ST_PER_STEP), lambda i: (i, 0, 0))
    return pl.pallas_call(
        _ssm_blocks_body,
        out_shape=[jax.ShapeDtypeStruct((g // k, CH_PER_STEP, 2 * ST_PER_STEP), BF16)] * 2,
        grid=(g // k,),
        in_specs=[s2, s2, s1, s3, s3, s3, s3],
        out_specs=[blk, blk],
        name="ssm_blocks",
    )(lr, li, ldt, bt_re, bt_im, c_re, c_im)


def _dot_nt(a, b):
    return lax.dot_general(a, b, (((1,), (1,)), ((), ())), preferred_element_type=F32)


def _gelu_tanh(x):
    return 0.5 * x * (1.0 + jnp.tanh(math.sqrt(2.0 / math.pi) * (x + 0.044715 * (x * x * x))))


def _ssm_scan_body(ua_ref, ub_ref, b_ref, c_ref, pr_ref, pi_ref, d_ref, y_ref, hr_ref, hi_ref,
                   h_s, y_s, cr_s, ci_s):
    tb = pl.program_id(2)
    nt = ST_PER_STEP // 128
    u_perm = jnp.concatenate(
        [jnp.concatenate([ua_ref[pl.ds(i, 8, stride=SCAN_SEG), :], ub_ref[pl.ds(i, 8, stride=SCAN_SEG), :]], axis=1)
         for i in range(SCAN_SEG)], axis=0).astype(BF16)

    @pl.when(tb == 0)
    def _():
        cr_s[...] = jnp.zeros_like(cr_s)
        ci_s[...] = jnp.zeros_like(ci_s)

    row_slabs = [slice(r, r + PROJ_ROWS) for r in range(0, SCAN_BLOCK, PROJ_ROWS)]

    def b_proj(c):
        for rows in row_slabs:
            bu = jnp.dot(u_perm[rows], b_ref[:, c * 256:(c + 1) * 256], preferred_element_type=F32)
            h_s[c, rows, :] = bu[:, :128]
            h_s[nt + c, rows, :] = bu[:, 128:]

    row8 = lax.broadcasted_iota(jnp.int32, (8, 128), 0)
    b_proj(0)
    y = None
    for c in range(nt):
        if c + 1 < nt:
            b_proj(c + 1)
        lanes = slice(c * 128, (c + 1) * 128)
        ar = jnp.broadcast_to(pr_ref[0:1, lanes], (8, 128))
        ai = jnp.broadcast_to(pi_ref[0:1, lanes], (8, 128))
        hr = jnp.zeros((8, 128), F32)
        hi = jnp.zeros((8, 128), F32)
        for i in range(SCAN_SEG):
            xr = h_s[c, i * 8:(i + 1) * 8, :]
            xi = h_s[nt + c, i * 8:(i + 1) * 8, :]
            hr, hi = ar * hr - ai * hi + xr, ar * hi + ai * hr + xi

        asr = pr_ref[1:2, lanes]
        asi = pi_ref[1:2, lanes]
        c_r = cr_s[:, lanes]
        c_i = ci_s[:, lanes]
        cin_r = jnp.zeros((8, 128), F32)
        cin_i = jnp.zeros((8, 128), F32)
        for s in range(8):
            cin_r = jnp.where(row8 == s, c_r, cin_r)
            cin_i = jnp.where(row8 == s, c_i, cin_i)
            e_r = hr[s:s + 1, :]
            e_i = hi[s:s + 1, :]
            c_r, c_i = e_r + asr * c_r - asi * c_i, e_i + asr * c_i + asi * c_r
        cr_s[:, lanes] = c_r
        ci_s[:, lanes] = c_i

        hr, hi = cin_r, cin_i
        for i in range(SCAN_SEG):
            xr = h_s[c, i * 8:(i + 1) * 8, :]
            xi = h_s[nt + c, i * 8:(i + 1) * 8, :]
            hr, hi = ar * hr - ai * hi + xr, ar * hi + ai * hr + xi
            h_s[c, i * 8:(i + 1) * 8, :] = hr
            h_s[nt + c, i * 8:(i + 1) * 8, :] = hi

        for rows in row_slabs:
            h_cat = jnp.concatenate([h_s[c, rows, :], h_s[nt + c, rows, :]], axis=1).astype(BF16)
            y_c = _dot_nt(h_cat, c_ref[:, c * 256:(c + 1) * 256])
            if c == 0:
                y_s[0, rows, :] = y_c[:, :128]
                y_s[1, rows, :] = y_c[:, 128:]
            else:
                y_s[0, rows, :] += y_c[:, :128]
                y_s[1, rows, :] += y_c[:, 128:]
    per_seg = SCAN_SEG // 8
    y_nat = jnp.concatenate(
        [jnp.concatenate([y_s[h, pl.ds((j % per_seg) * 64 + j // per_seg, 8, stride=8), :] for h in range(2)], axis=1)
         for j in range(SCAN_BLOCK // 8)], axis=0)
    u = jnp.concatenate([ua_ref[...], ub_ref[...]], axis=1)
    y_ref[...] = _gelu_tanh(y_nat + d_ref[...] * u)

    @pl.when(tb == pl.num_programs(2) - 1)
    def _():
        hr_ref[...] = cr_s[...]
        hi_ref[...] = ci_s[...]


def _ssm_scan(uz, bblk, cblk, pw_r, pw_i, d_skip, *, batch, t_len):
    nt = t_len // SCAN_BLOCK
    return pl.pallas_call(
        _ssm_scan_body,
        out_shape=[
            jax.ShapeDtypeStruct((batch * t_len, SSM_WIDTH), F32),
            jax.ShapeDtypeStruct((batch, 1, SSM_GROUPS * SSM_STATE), F32),
            jax.ShapeDtypeStruct((batch, 1, SSM_GROUPS * SSM_STATE), F32),
        ],
        grid=(batch, N_GSTEPS, nt),
        in_specs=[
            pl.BlockSpec((SCAN_BLOCK, 128), lambda b, g, t: (b * nt + t, 2 * g)),
            pl.BlockSpec((SCAN_BLOCK, 128), lambda b, g, t: (b * nt + t, 2 * g + 1)),
            pl.BlockSpec((None, CH_PER_STEP, 2 * ST_PER_STEP), lambda b, g, t: (g, 0, 0)),
            pl.BlockSpec((None, CH_PER_STEP, 2 * ST_PER_STEP), lambda b, g, t: (g, 0, 0)),
            pl.BlockSpec((POW_ROWS, ST_PER_STEP), lambda b, g, t: (0, g)),
            pl.BlockSpec((POW_ROWS, ST_PER_STEP), lambda b, g, t: (0, g)),
            pl.BlockSpec((1, CH_PER_STEP), lambda b, g, t: (0, g)),
        ],
        out_specs=[
            pl.BlockSpec((SCAN_BLOCK, CH_PER_STEP), lambda b, g, t: (b * nt + t, g)),
            pl.BlockSpec((None, 1, ST_PER_STEP), lambda b, g, t: (b, 0, g)),
            pl.BlockSpec((None, 1, ST_PER_STEP), lambda b, g, t: (b, 0, g)),
        ],
        scratch_shapes=[
            pltpu.VMEM((2 * ST_PER_STEP // 128, SCAN_BLOCK, 128), F32),
            pltpu.VMEM((CH_PER_STEP // 128, SCAN_BLOCK, 128), F32),
            pltpu.VMEM((1, ST_PER_STEP), F32),
            pltpu.VMEM((1, ST_PER_STEP), F32),
        ],
        compiler_params=_params(("parallel", "parallel", "arbitrary")),
        name="ssm_scan",
    )(uz, uz, bblk, cblk, pw_r, pw_i, d_skip.reshape(1, SSM_WIDTH))


def _ssm_decode_body(u_ref, b_ref, c_ref, pr_ref, pi_ref, d_ref, h0r_ref, h0i_ref, y_ref, hr_ref, hi_ref):
    nt = ST_PER_STEP // 128
    u = u_ref[...]
    bu = jnp.dot(u.astype(BF16), b_ref[...], preferred_element_type=F32)
    h_cat = []
    for c in range(nt):
        lanes = slice(c * 128, (c + 1) * 128)
        ar = pr_ref[0:1, lanes]
        ai = pi_ref[0:1, lanes]
        h0r = h0r_ref[:, lanes]
        h0i = h0i_ref[:, lanes]
        hr = bu[:, c * 256:c * 256 + 128] + ar * h0r - ai * h0i
        hi = bu[:, c * 256 + 128:(c + 1) * 256] + ar * h0i + ai * h0r
        hr_ref[:, lanes] = hr
        hi_ref[:, lanes] = hi
        h_cat += [hr, hi]
    y = _dot_nt(jnp.concatenate(h_cat, axis=1).astype(BF16), c_ref[...])
    y_ref[...] = _gelu_tanh(y + d_ref[...] * u)


def _ssm_decode(uz, bblk, cblk, pw_r, pw_i, d_skip, h0r, h0i):
    nb = uz.shape[0]
    return pl.pallas_call(
        _ssm_decode_body,
        out_shape=[
            jax.ShapeDtypeStruct((nb, SSM_WIDTH), F32),
            jax.ShapeDtypeStruct((nb, SSM_GROUPS * SSM_STATE), F32),
            jax.ShapeDtypeStruct((nb, SSM_GROUPS * SSM_STATE), F32),
        ],
        grid=(N_GSTEPS,),
        in_specs=[
            pl.BlockSpec((nb, CH_PER_STEP), lambda g: (0, g)),
            pl.BlockSpec((None, CH_PER_STEP, 2 * ST_PER_STEP), lambda g: (g, 0, 0)),
            pl.BlockSpec((None, CH_PER_STEP, 2 * ST_PER_STEP), lambda g: (g, 0, 0)),
            pl.BlockSpec((POW_ROWS, ST_PER_STEP), lambda g: (0, g)),
            pl.BlockSpec((POW_ROWS, ST_PER_STEP), lambda g: (0, g)),
            pl.BlockSpec((1, CH_PER_STEP), lambda g: (0, g)),
            pl.BlockSpec((nb, ST_PER_STEP), lambda g: (0, g)),
            pl.BlockSpec((nb, ST_PER_STEP), lambda g: (0, g)),
        ],
        out_specs=[
            pl.BlockSpec((nb, CH_PER_STEP), lambda g: (0, g)),
            pl.BlockSpec((nb, ST_PER_STEP), lambda g: (0, g)),
            pl.BlockSpec((nb, ST_PER_STEP), lambda g: (0, g)),
        ],
        compiler_params=_params(("parallel",)),
        name="ssm_decode",
    )(uz, bblk, cblk, pw_r, pw_i, d_skip.reshape(1, SSM_WIDTH), h0r, h0i)


GLU_SLAB = 256


def _glu_body(y_ref, w_ref, b_ref, z_ref, o_ref, yb_scr, *, tn):
    j = pl.program_id(1)

    @pl.when(j == 0)
    def _():
        yb_scr[...] = y_ref[...].astype(BF16)

    for c in range(tn // GLU_SLAB):
        cols = slice(c * GLU_SLAB, (c + 1) * GLU_SLAB)
        t = jnp.dot(yb_scr[...], w_ref[:, cols], preferred_element_type=F32) + b_ref[:, cols]
        y = y_ref[:, pl.ds(pl.multiple_of(j * tn + c * GLU_SLAB, GLU_SLAB), GLU_SLAB)]
        o_ref[:, cols] = (y * jax.nn.sigmoid(t) * _silu(z_ref[:, cols])).astype(BF16)


def _glu(y, w, b, uz, *, tm, tn):
    m, d = y.shape
    zoff = SSM_WIDTH // tn
    return pl.pallas_call(
        functools.partial(_glu_body, tn=tn),
        out_shape=jax.ShapeDtypeStruct((m, d), BF16),
        grid=(m // tm, d // tn),
        in_specs=[
            pl.BlockSpec((tm, d), lambda i, j: (i, 0)),
            pl.BlockSpec((d, tn), lambda i, j: (0, j)),
            pl.BlockSpec((1, tn), lambda i, j: (0, j)),
            pl.BlockSpec((tm, tn), lambda i, j: (i, zoff + j)),
        ],
        out_specs=pl.BlockSpec((tm, tn), lambda i, j: (i, j)),
        scratch_shapes=[pltpu.VMEM((tm, d), BF16)],
        compiler_params=_params(("parallel", "arbitrary")),
        name="ssm_glu",
    )(y, w, b.reshape(1, d), uz)


def _prep_weights(norm_gdn, w_in_gdn, conv_gdn, a_log_gdn, dt_bias_gdn, onorm_gdn, w_out_gdn,
                  norm_ssm, w_in_ssm, lam_re, lam_im, b_re, b_im, c_re, c_im, d_ssm, log_dt_ssm,
                  w_glu_ssm, b_glu_ssm, w_out_ssm, norm_final):
    w_in_t = jnp.swapaxes(w_in_gdn[0], 0, 1)
    w_ba = jnp.pad(w_in_t[MAIN_DIM:], ((0, 128 - 2 * V_HEADS), (0, 0))).astype(BF16)
    zeros = jnp.zeros((V_HEADS,), F32)
    pad = jnp.zeros((128 - 2 * V_HEADS,), F32)
    gate_params = jnp.stack([jnp.concatenate([zeros, a_log_gdn[0], pad]),
                             jnp.concatenate([zeros, dt_bias_gdn[0], pad])])

    lr, li, ldt = lam_re[0], lam_im[0], log_dt_ssm[0]
    gp = SSM_GROUPS * SSM_STATE
    pw_r, pw_i = _ssm_pow(lr.reshape(1, gp), li.reshape(1, gp),
                          jnp.repeat(ldt, SSM_STATE).reshape(1, gp))
    bblk, cblk = _ssm_blocks(lr, li, ldt.reshape(SSM_GROUPS, 1),
                                 jnp.swapaxes(b_re[0], 1, 2), jnp.swapaxes(b_im[0], 1, 2), c_re[0], c_im[0])
    return dict(
        norm_gdn=norm_gdn[0], w_main=w_in_t, w_ba=w_ba, conv_w=conv_gdn[0], gate_params=gate_params,
        o_gain=onorm_gdn[0], w_out_gdn=w_out_gdn[0].astype(BF16),
        norm_ssm=norm_ssm[0], w_in_ssm=w_in_ssm[0], pw_r=pw_r, pw_i=pw_i, bblk=bblk, cblk=cblk,
        d_skip=d_ssm[0], w_glu=w_glu_ssm[0].astype(BF16), b_glu=b_glu_ssm[0], w_out_ssm=w_out_ssm[0].astype(BF16),
        norm_final=norm_final)


def _ssm_tail(p, x1, y, uz, *, tm, tm_glu):
    y3 = _glu(y, p["w_glu"], p["b_glu"], uz, tm=tm_glu, tn=1024)
    return _outproj(y3, p["w_out_ssm"], x1, p["norm_final"], tm=tm, name="ssm_out")


def _prompt_path(p, x_prompt):
    batch, t_len, d = x_prompt.shape
    x = x_prompt.reshape(batch * t_len, d)
    proj, ba = _norm_matmul(x, p["norm_gdn"], p["w_main"], p["w_ba"], n=MAIN_DIM, tm=2048, tn=512, name="gdn_in",
                            w_transposed=True)
    og, delta = _gdn_prompt(proj, ba, p["conv_w"], p["gate_params"], p["o_gain"], batch=batch, t_len=t_len)
    conv_state = proj.reshape(batch, t_len, MAIN_DIM)[:, t_len - (CONV_W - 1):, :CONV_DIM]
    x1 = _outproj(og, p["w_out_gdn"], x, tm=512, name="gdn_out")
    uz = _norm_matmul(x1, p["norm_ssm"], p["w_in_ssm"], n=2 * SSM_WIDTH, tm=2048, tn=512, name="ssm_in")
    y, h_re, h_im = _ssm_scan(uz, p["bblk"], p["cblk"], p["pw_r"], p["pw_i"], p["d_skip"],
                              batch=batch, t_len=t_len)
    out = _ssm_tail(p, x1, y, uz, tm=512, tm_glu=512)
    return (out.reshape(batch, t_len, d), conv_state[None], delta[None],
            h_re.reshape(1, batch, SSM_GROUPS, SSM_STATE), h_im.reshape(1, batch, SSM_GROUPS, SSM_STATE))


def _sample_path(p, x_sample, state_conv, state_delta, state_re, state_im):
    nb, _, d = x_sample.shape
    x = x_sample.reshape(nb, d)
    proj, ba = _norm_matmul(x, p["norm_gdn"], p["w_main"], p["w_ba"], n=MAIN_DIM, tm=nb, tn=1024, name="gdn_in_s",
                            w_transposed=True)
    beta, eg = _gdn_gates(ba, p["gate_params"])
    xs = jnp.concatenate([state_conv[0], proj[:, None, :CONV_DIM]], axis=1)
    slots = CONV_DIM // DK
    delta, og = _gdn_decode(beta[:, :V_HEADS], eg[:, V_HEADS:2 * V_HEADS],
                            xs.reshape(nb, CONV_W, slots, DK), p["conv_w"].reshape(CONV_W, slots, DK),
                            proj[:, CONV_DIM:].reshape(nb, V_HEADS, DV), p["o_gain"], state_delta[0])
    x1 = _outproj(og.reshape(nb, VAL_DIM).astype(BF16), p["w_out_gdn"], x, tm=nb, name="gdn_out_s")
    uz = _norm_matmul(x1, p["norm_ssm"], p["w_in_ssm"], n=2 * SSM_WIDTH, tm=nb, tn=1024, name="ssm_in_s")
    gp = SSM_GROUPS * SSM_STATE
    y, h_re, h_im = _ssm_decode(uz, p["bblk"], p["cblk"], p["pw_r"], p["pw_i"], p["d_skip"],
                                state_re[0].reshape(nb, gp), state_im[0].reshape(nb, gp))
    out = _ssm_tail(p, x1, y, uz, tm=nb, tm_glu=nb)
    return (out.reshape(nb, 1, d), xs[:, 1:][None], delta[None],
            h_re.reshape(1, nb, SSM_GROUPS, SSM_STATE), h_im.reshape(1, nb, SSM_GROUPS, SSM_STATE))


def kernel(x_prompt, x_sample, state_gdn_conv, state_gdn_delta, state_ssm_re, state_ssm_im, norm_gdn, w_in_gdn, conv_gdn, a_log_gdn, dt_bias_gdn, onorm_gdn, w_out_gdn, norm_ssm, w_in_ssm, lam_re, lam_im, b_re, b_im, c_re, c_im, d_ssm, log_dt_ssm, w_glu_ssm, b_glu_ssm, w_out_ssm, norm_final):
    p = _prep_weights(norm_gdn, w_in_gdn, conv_gdn, a_log_gdn, dt_bias_gdn, onorm_gdn, w_out_gdn,
                      norm_ssm, w_in_ssm, lam_re, lam_im, b_re, b_im, c_re, c_im, d_ssm, log_dt_ssm,
                      w_glu_ssm, b_glu_ssm, w_out_ssm, norm_final)
    y_p, conv_p, delta_p, re_p, im_p = _prompt_path(p, x_prompt)
    y_s, conv_s, delta_s, re_s, im_s = _sample_path(p, x_sample, state_gdn_conv, state_gdn_delta,
                                                    state_ssm_re, state_ssm_im)
    return (y_p, y_s, conv_p, delta_p, re_p, im_p, conv_s, delta_s, re_s, im_s)
```

```python
import functools
import math

import jax
import jax.numpy as jnp
from jax import lax
from jax.experimental import pallas as pl
from jax.experimental.pallas import tpu as pltpu

F32 = jnp.float32
BF16 = jnp.bfloat16

RMS_EPS = 1e-6
L2_EPS = 1e-6

D_MODEL = 2048
QK_HEADS = 16
V_HEADS = 32
DK = 128
DV = 128
KEY_DIM = QK_HEADS * DK
VAL_DIM = V_HEADS * DV
CONV_DIM = 2 * KEY_DIM + VAL_DIM
CONV_W = 4
MAIN_DIM = CONV_DIM + VAL_DIM

SSM_WIDTH = 4096
SSM_GROUP = 16
SSM_GROUPS = SSM_WIDTH // SSM_GROUP
SSM_STATE = 64
GROUPS_PER_STEP = 16
CH_PER_STEP = GROUPS_PER_STEP * SSM_GROUP
ST_PER_STEP = GROUPS_PER_STEP * SSM_STATE
N_GSTEPS = SSM_GROUPS // GROUPS_PER_STEP

CHUNK = 128
SCAN_BLOCK = 2048
SCAN_SEG = SCAN_BLOCK // 8
POW_ROWS = 8
PROJ_ROWS = 1024
OUT_TILES = 8

VMEM_LIMIT = 56 * 1024 * 1024


def _params(sem):
    return pltpu.CompilerParams(dimension_semantics=sem, vmem_limit_bytes=VMEM_LIMIT)


def _silu(x):
    return x * jax.nn.sigmoid(x)


def _rms(x, g):
    ms = jnp.mean(x * x, axis=-1, keepdims=True)
    return x * lax.rsqrt(ms + RMS_EPS) * g


def _norm_matmul_body(x_ref, g_ref, w_ref, *rest, has_aux, w_transposed):
    if has_aux:
        waux_ref, o_ref, aux_ref, h_scr = rest
    else:
        o_ref, h_scr = rest
    mm = _dot_nt if w_transposed else _dotb

    @pl.when(pl.program_id(1) == 0)
    def _():
        h = _rms(x_ref[...], g_ref[...]).astype(BF16)
        h_scr[...] = h
        if has_aux:
            aux_ref[...] = mm(h, waux_ref[...])

    o_ref[...] = mm(h_scr[...], w_ref[...].astype(BF16))


def _norm_matmul(x, g, w, w_aux=None, *, n, tm, tn, name, w_transposed=False):
    m, d = x.shape
    has_aux = w_aux is not None
    in_specs = [
        pl.BlockSpec((tm, d), lambda i, j: (i, 0), pipeline_mode=pl.Buffered(1)),
        pl.BlockSpec((1, d), lambda i, j: (0, 0)),
        pl.BlockSpec((tn, d), lambda i, j: (j, 0)) if w_transposed else pl.BlockSpec((d, tn), lambda i, j: (0, j)),
    ]
    out_shape = [jax.ShapeDtypeStruct((m, n), F32)]
    out_specs = [pl.BlockSpec((tm, tn), lambda i, j: (i, j))]
    args = [x, g.reshape(1, d), w]
    if has_aux:
        na = w_aux.shape[0] if w_transposed else w_aux.shape[1]
        in_specs.append(pl.BlockSpec(w_aux.shape, lambda i, j: (0, 0)))
        out_shape.append(jax.ShapeDtypeStruct((m, na), F32))
        out_specs.append(pl.BlockSpec((tm, na), lambda i, j: (i, 0)))
        args.append(w_aux)
    res = pl.pallas_call(
        functools.partial(_norm_matmul_body, has_aux=has_aux, w_transposed=w_transposed),
        out_shape=out_shape,
        grid=(m // tm, n // tn),
        in_specs=in_specs,
        out_specs=out_specs,
        scratch_shapes=[pltpu.VMEM((tm, d), BF16)],
        compiler_params=_params(("parallel", "arbitrary")),
        name=name,
    )(*args)
    return res if has_aux else res[0]


def _outproj_body(a_ref, w_ref, x_ref, *rest, final_norm):
    if final_norm:
        g_ref, o_ref = rest
    else:
        (o_ref,) = rest
    y = x_ref[...] + jnp.dot(a_ref[...], w_ref[...], preferred_element_type=F32)
    o_ref[...] = _rms(y, g_ref[...]) if final_norm else y


def _outproj(a, w, x, g=None, *, tm, name):
    m, kd = a.shape
    n = w.shape[1]
    final_norm = g is not None
    in_specs = [
        pl.BlockSpec((tm, kd), lambda i: (i, 0)),
        pl.BlockSpec((kd, n), lambda i: (0, 0), pipeline_mode=pl.Buffered(1)),
        pl.BlockSpec((tm, n), lambda i: (i, 0)),
    ]
    args = [a, w, x]
    if final_norm:
        in_specs.append(pl.BlockSpec((1, n), lambda i: (0, 0)))
        args.append(g.reshape(1, n))
    return pl.pallas_call(
        functools.partial(_outproj_body, final_norm=final_norm),
        out_shape=jax.ShapeDtypeStruct((m, n), F32),
        grid=(m // tm,),
        in_specs=in_specs,
        out_specs=pl.BlockSpec((tm, n), lambda i: (i, 0)),
        compiler_params=_params(("parallel",)),
        name=name,
    )(*args)


def _conv_silu_chunk(x_ref, cw, n):
    cc = x_ref.shape[1]
    r0 = pl.multiple_of(n * CHUNK, CHUNK)
    cur = x_ref[pl.ds(r0, CHUNK), :]
    p0 = pl.multiple_of(jnp.maximum(r0 - 8, 0), 8)
    prev = jnp.where(n > 0, x_ref[pl.ds(p0, 8), :], 0.0)
    row8 = lax.broadcasted_iota(jnp.int32, (8, cc), 0)
    y = cur * cw[CONV_W - 1:CONV_W, :]
    for s in range(1, CONV_W):
        sh = pltpu.roll(cur, s, 0)
        top = jnp.where(row8 < s, pltpu.roll(prev, s, 0), sh[0:8, :])
        sh = jnp.concatenate([top, sh[8:, :]], axis=0)
        y = y + sh * cw[CONV_W - 1 - s:CONV_W - s, :]
    return _silu(y)


INV_BASE_SHIFT = 3
V_PER_QK = V_HEADS // QK_HEADS
HQ_PER_STEP = 2
DEC_GROUP = 8
DEC_SAMPLES = 4
PH1_CHUNKS = 2
PIPE_STEPS = (1, 2, 1)


def _split3(x):
    hi = x.astype(BF16)
    r = x - hi.astype(F32)
    mid = r.astype(BF16)
    return hi, mid, (r - mid.astype(F32)).astype(BF16)


def _dotb(a, b):
    return jnp.dot(a, b, preferred_element_type=F32)


N_LEVELS = CHUNK.bit_length() - 1 - INV_BASE_SHIFT


def _chunk_tables():
    row = lax.broadcasted_iota(jnp.int32, (CHUNK, CHUNK), 0)
    col = lax.broadcasted_iota(jnp.int32, (CHUNK, CHUNK), 1)
    same = [(row >> s) == (col >> s) for s in range(INV_BASE_SHIFT, CHUNK.bit_length())]
    tab32 = jnp.stack([row == col, same[0]]).astype(F32)
    levels = [jnp.logical_and(hi, jnp.logical_not(lo)) for lo, hi in zip(same[:-1], same[1:])]
    tab16 = jnp.stack([row >= col] + levels).astype(BF16)
    return tab32, tab16


def _inv_unit_lower(mats, tab32_ref, tab16_ref, out):
    n = range(len(mats))
    eye = tab32_ref[0]
    d = [mats[i] * tab32_ref[1] for i in n]
    db = [d[i].astype(BF16) for i in n]
    t = [eye - d[i] for i in n]
    ab = [mats[i].astype(BF16) for i in n]
    p = [_dotb(db[i], db[i]) for i in n]
    yield
    t = [t[i] + _dotb(t[i].astype(BF16), p[i].astype(BF16)) for i in n]
    p = [_dotb(p[i].astype(BF16), p[i].astype(BF16)) for i in n]
    yield
    t = [t[i] + _dotb(t[i].astype(BF16), p[i].astype(BF16)) for i in n]
    yield
    for lvl in range(N_LEVELS):
        tb = [t[i].astype(BF16) for i in n]
        x = [_dotb(tb[i], ab[i] * tab16_ref[1 + lvl]) for i in n]
        yield
        t = [t[i] - _dotb(x[i].astype(BF16), tb[i]) for i in n]
        yield
    out.extend(t)


def _trace_interleaved(*gens, steps=None):
    live = list(zip(gens, steps or [1] * len(gens)))
    while live:
        for item in list(live):
            g, k = item
            try:
                for _ in range(k):
                    next(g)
            except StopIteration:
                live.remove(item)


def _gdn_prompt_body(q_ref, k_ref, v_ref, z_ref, ba_ref, cwq_ref, cwk_ref, cwv_ref, gp_ref, on_ref,
                     tab32_ref, tab16_ref, og_ref, s_ref, u_s, w_s, qd_s, at_s, kdt_s, gl_s, a_s, rhs_s):
    hq = pl.program_id(1)
    t_len = q_ref.shape[0]
    n_chunks = t_len // CHUNK
    row = lax.broadcasted_iota(jnp.int32, (CHUNK, CHUNK), 0)
    col = lax.broadcasted_iota(jnp.int32, (CHUNK, CHUNK), 1)
    causal = row >= col
    strict = row > col
    cwq = cwq_ref[...]
    cwk = cwk_ref[...]
    cwv = cwv_ref[...]
    neg_a = -jnp.exp(gp_ref[0:1, :])
    dt_bias = gp_ref[1:2, :]

    nv = HQ_PER_STEP * V_PER_QK
    n_iters = n_chunks // PH1_CHUNKS

    def chunk_prep(n, slot, base):
        r0 = pl.multiple_of(n * CHUNK, CHUNK)
        rows = pl.ds(r0, CHUNK)
        qc = _conv_silu_chunk(q_ref, cwq, n)
        yield
        kc = _conv_silu_chunk(k_ref, cwk, n)
        yield
        vc = _conv_silu_chunk(v_ref, cwv, n)
        yield
        ba = ba_ref[rows, :]
        beta_all = jax.nn.sigmoid(ba)
        xa = ba + dt_bias
        g_all = neg_a * (jnp.maximum(xa, 0.0) + jnp.log1p(jnp.exp(-jnp.abs(xa))))
        g_hi, g_mid, g_lo = _split3(g_all)
        tril = tab16_ref[0]
        gcum = _dotb(tril, g_hi) + (_dotb(tril, g_mid) + _dotb(tril, g_lo))
        gcum_t = gcum.T
        yield
        for q in range(HQ_PER_STEP):
            qh = qc[:, q * DK:(q + 1) * DK]
            kh = kc[:, q * DK:(q + 1) * DK]
            qn = qh * lax.rsqrt(jnp.sum(qh * qh, axis=-1, keepdims=True) + L2_EPS) * (DK ** -0.5)
            kn = kh * lax.rsqrt(jnp.sum(kh * kh, axis=-1, keepdims=True) + L2_EPS)
            kn_t = kn.T
            kn_tb = kn_t.astype(BF16)
            kk = _dotb(kn.astype(BF16), kn_tb)
            qk = _dotb(qn.astype(BF16), kn_tb)
            yield
            for j in range(V_PER_QK):
                h = q * V_PER_QK + j
                lane_b = (hq * HQ_PER_STEP + q) * V_PER_QK + j
                lane_a = V_HEADS + lane_b
                gcol = jnp.sum(jnp.where(col == lane_a, gcum, 0.0), axis=1, keepdims=True)
                bcol = jnp.sum(jnp.where(col == lane_b, beta_all, 0.0), axis=1, keepdims=True)
                grow = jnp.sum(jnp.where(row == lane_a, gcum_t, 0.0), axis=0, keepdims=True)
                glast = gcol[CHUNK - 1:CHUNK, :]
                decay = jnp.exp(jnp.where(causal, gcol - grow, -jnp.inf))
                a_mat = jnp.where(strict, kk * decay, 0.0) * bcol
                eg = jnp.exp(gcol)
                rhs = jnp.concatenate([vc[:, h * DV:(h + 1) * DV] * bcol, kn * (bcol * eg)], axis=1).astype(BF16)
                qd_s[h, rows, :] = (qn * eg).astype(BF16)
                at_s[h, rows, :] = (qk * decay).astype(BF16)
                kdt_s[h, rows, :] = (kn_t * jnp.exp(glast - grow)).astype(BF16)
                gl_s[h, pl.ds(n, 1), :] = jnp.broadcast_to(jnp.exp(glast), (1, DV))
                a_s[slot, base + h] = a_mat
                rhs_s[slot, base + h] = rhs
                yield

    def prep(it):
        it = jnp.asarray(it, jnp.int32)
        slot = it % 2
        for c in range(PH1_CHUNKS):
            yield from chunk_prep(PH1_CHUNKS * it + c, slot, c * nv)

    def solve(it):
        it = jnp.asarray(it, jnp.int32)
        slot = it % 2
        idx = range(PH1_CHUNKS * nv)
        t_mats = []
        yield from _inv_unit_lower([a_s[slot, i] for i in idx], tab32_ref, tab16_ref, t_mats)
        sols = [_dotb(t_mats[i].astype(BF16), rhs_s[slot, i]) for i in idx]
        yield
        for i in idx:
            rows = pl.ds(pl.multiple_of((PH1_CHUNKS * it + i // nv) * CHUNK, CHUNK), CHUNK)
            u_s[i % nv, rows, :] = sols[i][:, :DV]
            w_s[i % nv, rows, :] = sols[i][:, DV:].astype(BF16)
        yield

    gain = on_ref[...]
    heads = range(nv)

    def recur(it):
        it = jnp.asarray(it, jnp.int32)
        for c in range(PH1_CHUNKS):
            n = PH1_CHUNKS * it + c
            rows = pl.ds(pl.multiple_of(n * CHUNK, CHUNK), CHUNK)
            states = [s_ref[0, h] for h in heads]
            wq = [jnp.concatenate([w_s[h, rows, :], qd_s[h, rows, :]], axis=0) for h in heads]
            ws_qs = [_dotb(wq[h], states[h].astype(BF16)) for h in heads]
            yield
            v_new = [(u_s[h, rows, :] - ws_qs[h][:CHUNK]).astype(BF16) for h in heads]
            o = [ws_qs[h][CHUNK:] + _dotb(at_s[h, rows, :], v_new[h]) for h in heads]
            s_new = [states[h] * gl_s[h, pl.ds(n, 1), :] + _dotb(kdt_s[h, rows, :], v_new[h]) for h in heads]
            yield
            for h in heads:
                s_ref[0, h] = s_new[h]
                zg = z_ref[rows, h * DV:(h + 1) * DV]
                og = o[h] * lax.rsqrt(jnp.mean(o[h] * o[h], axis=-1, keepdims=True) + RMS_EPS) * gain * _silu(zg)
                og_ref[rows, h * DV:(h + 1) * DV] = og.astype(BF16)
            yield

    for h in heads:
        s_ref[0, h] = jnp.zeros((DK, DV), F32)
    _trace_interleaved(prep(0))
    _trace_interleaved(solve(0), prep(1), steps=PIPE_STEPS[:2])

    def steady(it, carry):
        _trace_interleaved(solve(it - 1), prep(it), recur(it - 2), steps=PIPE_STEPS)
        return carry

    lax.fori_loop(2, n_iters, steady, 0)
    _trace_interleaved(solve(n_iters - 1), recur(n_iters - 2), steps=(2, 1))
    _trace_interleaved(recur(n_iters - 1))


def _gdn_prompt(proj, ba, conv_w, gate_params, o_gain, *, batch, t_len):
    wq = HQ_PER_STEP * DK
    wv = HQ_PER_STEP * V_PER_QK * DV
    nv = HQ_PER_STEP * V_PER_QK
    in_specs = [
        pl.BlockSpec((t_len, wq), lambda b, h: (b, h)),
        pl.BlockSpec((t_len, wq), lambda b, h: (b, KEY_DIM // wq + h)),
        pl.BlockSpec((t_len, wv), lambda b, h: (b, (2 * KEY_DIM) // wv + h)),
        pl.BlockSpec((t_len, wv), lambda b, h: (b, CONV_DIM // wv + h)),
        pl.BlockSpec((t_len, 128), lambda b, h: (b, 0)),
        pl.BlockSpec((CONV_W, wq), lambda b, h: (0, h)),
        pl.BlockSpec((CONV_W, wq), lambda b, h: (0, KEY_DIM // wq + h)),
        pl.BlockSpec((CONV_W, wv), lambda b, h: (0, (2 * KEY_DIM) // wv + h)),
        pl.BlockSpec((2, 128), lambda b, h: (0, 0)),
        pl.BlockSpec((1, DV), lambda b, h: (0, 0)),
        pl.BlockSpec((2, CHUNK, CHUNK), lambda b, h: (0, 0, 0)),
        pl.BlockSpec((1 + N_LEVELS, CHUNK, CHUNK), lambda b, h: (0, 0, 0)),
    ]
    tab32, tab16 = _chunk_tables()
    out_shape = [
        jax.ShapeDtypeStruct((batch * t_len, VAL_DIM), BF16),
        jax.ShapeDtypeStruct((batch, V_HEADS, DK, DV), F32),
    ]
    out_specs = [
        pl.BlockSpec((t_len, wv), lambda b, h: (b, h)),
        pl.BlockSpec((1, nv, DK, DV), lambda b, h: (b, h, 0, 0)),
    ]
    scratch = [
        pltpu.VMEM((nv, t_len, DV), F32),
        pltpu.VMEM((nv, t_len, DK), BF16),
        pltpu.VMEM((nv, t_len, DK), BF16),
        pltpu.VMEM((nv, t_len, CHUNK), BF16),
        pltpu.VMEM((nv, t_len, CHUNK), BF16),
        pltpu.VMEM((nv, t_len // CHUNK, DV), F32),
        pltpu.VMEM((2, PH1_CHUNKS * nv, CHUNK, CHUNK), F32),
        pltpu.VMEM((2, PH1_CHUNKS * nv, CHUNK, DV + DK), BF16),
    ]
    return pl.pallas_call(
        _gdn_prompt_body,
        out_shape=out_shape,
        grid=(batch, QK_HEADS // HQ_PER_STEP),
        in_specs=in_specs,
        out_specs=out_specs,
        scratch_shapes=scratch,
        compiler_params=_params(("parallel", "arbitrary")),
        name="gdn_prompt",
    )(proj, proj, proj, proj, ba, conv_w, conv_w, conv_w, gate_params, o_gain.reshape(1, DV), tab32, tab16)


def _gdn_gates_body(ba_ref, gp_ref, beta_ref, eg_ref):
    ba = ba_ref[...]
    beta_ref[...] = jax.nn.sigmoid(ba)
    xa = ba + gp_ref[1:2, :]
    g = -jnp.exp(gp_ref[0:1, :]) * (jnp.maximum(xa, 0.0) + jnp.log1p(jnp.exp(-jnp.abs(xa))))
    eg_ref[...] = jnp.exp(g)


def _gdn_gates(ba, gate_params):
    m = ba.shape[0]
    return pl.pallas_call(
        _gdn_gates_body,
        out_shape=[jax.ShapeDtypeStruct((m, 128), F32)] * 2,
        name="gdn_gates",
    )(ba, gate_params)


def _gdn_decode_body(beta_ref, eg_ref, xs_ref, cw_ref, z_ref, on_ref, s_ref, so_ref, og_ref):
    for si in range(DEC_SAMPLES):
        b = pl.program_id(0) * DEC_SAMPLES + si
        y = xs_ref[si, 0] * cw_ref[0]
        for j in range(1, CONV_W):
            y = y + xs_ref[si, j] * cw_ref[j]
        y = _silu(y)
        q = y[0:QK_HEADS]
        k = y[QK_HEADS:2 * QK_HEADS]
        v = y[2 * QK_HEADS:]
        qn = q * lax.rsqrt(jnp.sum(q * q, axis=-1, keepdims=True) + L2_EPS) * (DK ** -0.5)
        kn = k * lax.rsqrt(jnp.sum(k * k, axis=-1, keepdims=True) + L2_EPS)
        qk_t = jnp.concatenate([qn, kn, jnp.zeros((DK - 2 * QK_HEADS, DK), F32)], axis=0).T
        for g0 in range(0, V_HEADS, DEC_GROUP):
            hs = range(g0, g0 + DEC_GROUP)
            qcol = {h: qk_t[:, h // V_PER_QK:h // V_PER_QK + 1] for h in hs}
            kcol = {h: qk_t[:, QK_HEADS + h // V_PER_QK:QK_HEADS + h // V_PER_QK + 1] for h in hs}
            sd = {h: s_ref[si, h] * eg_ref[b, h] for h in hs}
            ks = {h: jnp.sum(sd[h] * kcol[h], axis=0, keepdims=True) for h in hs}
            v_new = {h: beta_ref[b, h] * (v[h:h + 1, :] - ks[h]) for h in hs}
            sn = {h: sd[h] + kcol[h] * v_new[h] for h in hs}
            for h in hs:
                so_ref[si, h] = sn[h]
            o = {h: jnp.sum(sn[h] * qcol[h], axis=0, keepdims=True) for h in hs}
            for h in hs:
                og_ref[si, h:h + 1, :] = o[h]
        o_all = og_ref[si]
        og_ref[si] = (o_all * lax.rsqrt(jnp.mean(o_all * o_all, axis=-1, keepdims=True) + RMS_EPS)
                      * on_ref[...] * _silu(z_ref[si]))


def _gdn_decode(beta, eg, xs, conv_w3, z3, o_gain, state):
    nb = xs.shape[0]
    slots = CONV_DIM // DK
    smem = pl.BlockSpec(memory_space=pltpu.SMEM)
    return pl.pallas_call(
        _gdn_decode_body,
        out_shape=[
            jax.ShapeDtypeStruct((nb, V_HEADS, DK, DV), F32),
            jax.ShapeDtypeStruct((nb, V_HEADS, DV), F32),
        ],
        grid=(nb // DEC_SAMPLES,),
        in_specs=[
            smem,
            smem,
            pl.BlockSpec((DEC_SAMPLES, CONV_W, slots, DK), lambda b: (b, 0, 0, 0)),
            pl.BlockSpec((CONV_W, slots, DK), lambda b: (0, 0, 0)),
            pl.BlockSpec((DEC_SAMPLES, V_HEADS, DV), lambda b: (b, 0, 0)),
            pl.BlockSpec((1, DV), lambda b: (0, 0)),
            pl.BlockSpec((DEC_SAMPLES, V_HEADS, DK, DV), lambda b: (b, 0, 0, 0)),
        ],
        out_specs=[
            pl.BlockSpec((DEC_SAMPLES, V_HEADS, DK, DV), lambda b: (b, 0, 0, 0)),
            pl.BlockSpec((DEC_SAMPLES, V_HEADS, DV), lambda b: (b, 0, 0)),
        ],
        compiler_params=_params(("parallel",)),
        name="gdn_decode",
    )(beta, eg, xs, conv_w3, z3, o_gain.reshape(1, DV), state)


def _ssm_pow_body(lr_ref, li_ref, ldt_ref, pr_ref, pi_ref):
    n = jnp.where(lax.broadcasted_iota(jnp.int32, pr_ref.shape, 0) == 0, 1.0, float(SCAN_SEG))
    dt = jnp.exp(ldt_ref[...])
    lr = jnp.minimum(lr_ref[...], -1e-4) * dt
    li = li_ref[...] * dt
    mag = jnp.exp(n * lr)
    pr_ref[...] = mag * jnp.cos(n * li)
    pi_ref[...] = mag * jnp.sin(n * li)


def _ssm_pow(lr, li, ldt):
    gp = lr.shape[1]
    tl = 2048
    spec = pl.BlockSpec((1, tl), lambda i: (0, i))
    ospec = pl.BlockSpec((POW_ROWS, tl), lambda i: (0, i))
    return pl.pallas_call(
        _ssm_pow_body,
        out_shape=[jax.ShapeDtypeStruct((POW_ROWS, gp), F32)] * 2,
        grid=(gp // tl,),
        in_specs=[spec, spec, spec],
        out_specs=[ospec, ospec],
        name="ssm_pow",
    )(lr, li, ldt)


def _ssm_blocks_body(lr_ref, li_ref, ldt_ref, br_ref, bi_ref, cr_ref, ci_ref, bblk_ref, cblk_ref):
    dt = jnp.exp(ldt_ref[...])
    lr = jnp.minimum(lr_ref[...], -1e-4)
    li = li_ref[...]
    mag = jnp.exp(lr * dt)
    xr = mag * jnp.cos(li * dt) - 1.0
    xi = mag * jnp.sin(li * dt)
    den = 1.0 / (lr * lr + li * li)
    fr = (xr * lr + xi * li) * den
    fi = (xi * lr - xr * li) * den
    bblk_ref[...] = jnp.zeros_like(bblk_ref)
    cblk_ref[...] = jnp.zeros_like(cblk_ref)
    per_tile = 128 // SSM_STATE
    for g in range(GROUPS_PER_STEP):
        rows = slice(g * SSM_GROUP, (g + 1) * SSM_GROUP)
        off = (g // per_tile) * 256 + (g % per_tile) * SSM_STATE
        cols = slice(off, off + SSM_STATE)
        cols_im = slice(off + 128, off + 128 + SSM_STATE)
        br = br_ref[g]
        bi = bi_ref[g]
        frg = fr[g:g + 1, :]
        fig = fi[g:g + 1, :]
        bblk_ref[rows, cols] = (frg * br - fig * bi).astype(BF16)
        bblk_ref[rows, cols_im] = (frg * bi + fig * br).astype(BF16)
        cblk_ref[rows, cols] = cr_ref[g].astype(BF16)
        cblk_ref[rows, cols_im] = (-ci_ref[g]).astype(BF16)


def _ssm_blocks(lr, li, ldt, bt_re, bt_im, c_re, c_im):
    g, p = lr.shape
    k = GROUPS_PER_STEP
    s2 = pl.BlockSpec((k, p), lambda i: (i, 0))
    s1 = pl.BlockSpec((k, 1), lambda i: (i, 0))
    s3 = pl.BlockSpec((k, SSM_GROUP, p), lambda i: (i, 0, 0))
    blk = pl.BlockSpec((None, CH_PER_STEP, 2 * ST_PER_STEP), lambda i: (i, 0, 0))
    return pl.pallas_call(
        _ssm_blocks_body,
        out_shape=[jax.ShapeDtypeStruct((g // k, CH_PER_STEP, 2 * ST_PER_STEP), BF16)] * 2,
        grid=(g // k,),
        in_specs=[s2, s2, s1, s3, s3, s3, s3],
        out_specs=[blk, blk],
        name="ssm_blocks",
    )(lr, li, ldt, bt_re, bt_im, c_re, c_im)


def _dot_nt(a, b):
    return lax.dot_general(a, b, (((1,), (1,)), ((), ())), preferred_element_type=F32)


def _gelu_tanh(x):
    return 0.5 * x * (1.0 + jnp.tanh(math.sqrt(2.0 / math.pi) * (x + 0.044715 * (x * x * x))))


def _ssm_scan_body(ua_ref, ub_ref, b_ref, c_ref, pr_ref, pi_ref, d_ref, y_ref, hr_ref, hi_ref,
                   h_s, y_s, cr_s, ci_s):
    tb = pl.program_id(2)
    nt = ST_PER_STEP // 128
    u_perm = jnp.concatenate(
        [jnp.concatenate([ua_ref[pl.ds(i, 8, stride=SCAN_SEG), :], ub_ref[pl.ds(i, 8, stride=SCAN_SEG), :]], axis=1)
         for i in range(SCAN_SEG)], axis=0).astype(BF16)

    @pl.when(tb == 0)
    def _():
        cr_s[...] = jnp.zeros_like(cr_s)
        ci_s[...] = jnp.zeros_like(ci_s)

    row_slabs = [slice(r, r + PROJ_ROWS) for r in range(0, SCAN_BLOCK, PROJ_ROWS)]

    def b_proj(c):
        for rows in row_slabs:
            bu = jnp.dot(u_perm[rows], b_ref[:, c * 256:(c + 1) * 256], preferred_element_type=F32)
            h_s[c, rows, :] = bu[:, :128]
            h_s[nt + c, rows, :] = bu[:, 128:]

    row8 = lax.broadcasted_iota(jnp.int32, (8, 128), 0)
    b_proj(0)
    y = None
    for c in range(nt):
        if c + 1 < nt:
            b_proj(c + 1)
        lanes = slice(c * 128, (c + 1) * 128)
        ar = jnp.broadcast_to(pr_ref[0:1, lanes], (8, 128))
        ai = jnp.broadcast_to(pi_ref[0:1, lanes], (8, 128))
        hr = jnp.zeros((8, 128), F32)
        hi = jnp.zeros((8, 128), F32)
        for i in range(SCAN_SEG):
            xr = h_s[c, i * 8:(i + 1) * 8, :]
            xi = h_s[nt + c, i * 8:(i + 1) * 8, :]
            hr, hi = ar * hr - ai * hi + xr, ar * hi + ai * hr + xi

        asr = pr_ref[1:2, lanes]
        asi = pi_ref[1:2, lanes]
        c_r = cr_s[:, lanes]
        c_i = ci_s[:, lanes]
        cin_r = jnp.zeros((8, 128), F32)
        cin_i = jnp.zeros((8, 128), F32)
        for s in range(8):
            cin_r = jnp.where(row8 == s, c_r, cin_r)
            cin_i = jnp.where(row8 == s, c_i, cin_i)
            e_r = hr[s:s + 1, :]
            e_i = hi[s:s + 1, :]
            c_r, c_i = e_r + asr * c_r - asi * c_i, e_i + asr * c_i + asi * c_r
        cr_s[:, lanes] = c_r
        ci_s[:, lanes] = c_i

        hr, hi = cin_r, cin_i
        for i in range(SCAN_SEG):
            xr = h_s[c, i * 8:(i + 1) * 8, :]
            xi = h_s[nt + c, i * 8:(i + 1) * 8, :]
            hr, hi = ar * hr - ai * hi + xr, ar * hi + ai * hr + xi
            h_s[c, i * 8:(i + 1) * 8, :] = hr
            h_s[nt + c, i * 8:(i + 1) * 8, :] = hi

        for rows in row_slabs:
            h_cat = jnp.concatenate([h_s[c, rows, :], h_s[nt + c, rows, :]], axis=1).astype(BF16)
            y_c = _dot_nt(h_cat, c_ref[:, c * 256:(c + 1) * 256])
            if c == 0:
                y_s[0, rows, :] = y_c[:, :128]
                y_s[1, rows, :] = y_c[:, 128:]
            else:
                y_s[0, rows, :] += y_c[:, :128]
                y_s[1, rows, :] += y_c[:, 128:]
    per_seg = SCAN_SEG // 8
    d_skip = d_ref[...]
    for j0 in range(0, SCAN_BLOCK // 8, OUT_TILES):
        y_nat = jnp.concatenate(
            [jnp.concatenate([y_s[h, pl.ds((j % per_seg) * 64 + j // per_seg, 8, stride=8), :] for h in range(2)],
                             axis=1) for j in range(j0, j0 + OUT_TILES)], axis=0)
        rows = slice(j0 * 8, (j0 + OUT_TILES) * 8)
        u = jnp.concatenate([ua_ref[rows, :], ub_ref[rows, :]], axis=1)
        y_ref[rows, :] = _gelu_tanh(y_nat + d_skip * u)

    @pl.when(tb == pl.num_programs(2) - 1)
    def _():
        hr_ref[...] = cr_s[...]
        hi_ref[...] = ci_s[...]


def _ssm_scan(uz, bblk, cblk, pw_r, pw_i, d_skip, *, batch, t_len):
    nt = t_len // SCAN_BLOCK
    return pl.pallas_call(
        _ssm_scan_body,
        out_shape=[
            jax.ShapeDtypeStruct((batch * t_len, SSM_WIDTH), F32),
            jax.ShapeDtypeStruct((batch, 1, SSM_GROUPS * SSM_STATE), F32),
            jax.ShapeDtypeStruct((batch, 1, SSM_GROUPS * SSM_STATE), F32),
        ],
        grid=(batch, N_GSTEPS, nt),
        in_specs=[
            pl.BlockSpec((SCAN_BLOCK, 128), lambda b, g, t: (b * nt + t, 2 * g)),
            pl.BlockSpec((SCAN_BLOCK, 128), lambda b, g, t: (b * nt + t, 2 * g + 1)),
            pl.BlockSpec((None, CH_PER_STEP, 2 * ST_PER_STEP), lambda b, g, t: (g, 0, 0)),
            pl.BlockSpec((None, CH_PER_STEP, 2 * ST_PER_STEP), lambda b, g, t: (g, 0, 0)),
            pl.BlockSpec((POW_ROWS, ST_PER_STEP), lambda b, g, t: (0, g)),
            pl.BlockSpec((POW_ROWS, ST_PER_STEP), lambda b, g, t: (0, g)),
            pl.BlockSpec((1, CH_PER_STEP), lambda b, g, t: (0, g)),
        ],
        out_specs=[
            pl.BlockSpec((SCAN_BLOCK, CH_PER_STEP), lambda b, g, t: (b * nt + t, g)),
            pl.BlockSpec((None, 1, ST_PER_STEP), lambda b, g, t: (b, 0, g)),
            pl.BlockSpec((None, 1, ST_PER_STEP), lambda b, g, t: (b, 0, g)),
        ],
        scratch_shapes=[
            pltpu.VMEM((2 * ST_PER_STEP // 128, SCAN_BLOCK, 128), F32),
            pltpu.VMEM((CH_PER_STEP // 128, SCAN_BLOCK, 128), F32),
            pltpu.VMEM((1, ST_PER_STEP), F32),
            pltpu.VMEM((1, ST_PER_STEP), F32),
        ],
        compiler_params=_params(("parallel", "parallel", "arbitrary")),
        name="ssm_scan",
    )(uz, uz, bblk, cblk, pw_r, pw_i, d_skip.reshape(1, SSM_WIDTH))


def _ssm_decode_body(u_ref, b_ref, c_ref, pr_ref, pi_ref, d_ref, h0r_ref, h0i_ref, y_ref, hr_ref, hi_ref):
    nt = ST_PER_STEP // 128
    u = u_ref[...]
    bu = jnp.dot(u.astype(BF16), b_ref[...], preferred_element_type=F32)
    h_cat = []
    for c in range(nt):
        lanes = slice(c * 128, (c + 1) * 128)
        ar = pr_ref[0:1, lanes]
        ai = pi_ref[0:1, lanes]
        h0r = h0r_ref[:, lanes]
        h0i = h0i_ref[:, lanes]
        hr = bu[:, c * 256:c * 256 + 128] + ar * h0r - ai * h0i
        hi = bu[:, c * 256 + 128:(c + 1) * 256] + ar * h0i + ai * h0r
        hr_ref[:, lanes] = hr
        hi_ref[:, lanes] = hi
        h_cat += [hr, hi]
    y = _dot_nt(jnp.concatenate(h_cat, axis=1).astype(BF16), c_ref[...])
    y_ref[...] = _gelu_tanh(y + d_ref[...] * u)


def _ssm_decode(uz, bblk, cblk, pw_r, pw_i, d_skip, h0r, h0i):
    nb = uz.shape[0]
    return pl.pallas_call(
        _ssm_decode_body,
        out_shape=[
            jax.ShapeDtypeStruct((nb, SSM_WIDTH), F32),
            jax.ShapeDtypeStruct((nb, SSM_GROUPS * SSM_STATE), F32),
            jax.ShapeDtypeStruct((nb, SSM_GROUPS * SSM_STATE), F32),
        ],
        grid=(N_GSTEPS,),
        in_specs=[
            pl.BlockSpec((nb, CH_PER_STEP), lambda g: (0, g)),
            pl.BlockSpec((None, CH_PER_STEP, 2 * ST_PER_STEP), lambda g: (g, 0, 0)),
            pl.BlockSpec((None, CH_PER_STEP, 2 * ST_PER_STEP), lambda g: (g, 0, 0)),
            pl.BlockSpec((POW_ROWS, ST_PER_STEP), lambda g: (0, g)),
            pl.BlockSpec((POW_ROWS, ST_PER_STEP), lambda g: (0, g)),
            pl.BlockSpec((1, CH_PER_STEP), lambda g: (0, g)),
            pl.BlockSpec((nb, ST_PER_STEP), lambda g: (0, g)),
            pl.BlockSpec((nb, ST_PER_STEP), lambda g: (0, g)),
        ],
        out_specs=[
            pl.BlockSpec((nb, CH_PER_STEP), lambda g: (0, g)),
            pl.BlockSpec((nb, ST_PER_STEP), lambda g: (0, g)),
            pl.BlockSpec((nb, ST_PER_STEP), lambda g: (0, g)),
        ],
        compiler_params=_params(("parallel",)),
        name="ssm_decode",
    )(uz, bblk, cblk, pw_r, pw_i, d_skip.reshape(1, SSM_WIDTH), h0r, h0i)


GLU_SLAB = 256


def _glu_body(y_ref, w_ref, b_ref, z_ref, o_ref, yb_scr, *, tn):
    j = pl.program_id(1)

    @pl.when(j == 0)
    def _():
        yb_scr[...] = y_ref[...].astype(BF16)

    for c in range(tn // GLU_SLAB):
        cols = slice(c * GLU_SLAB, (c + 1) * GLU_SLAB)
        t = jnp.dot(yb_scr[...], w_ref[:, cols], preferred_element_type=F32) + b_ref[:, cols]
        y = y_ref[:, pl.ds(pl.multiple_of(j * tn + c * GLU_SLAB, GLU_SLAB), GLU_SLAB)]
        z = z_ref[:, cols]
        o_ref[:, cols] = ((y * z) / ((1.0 + jnp.exp(-t)) * (1.0 + jnp.exp(-z)))).astype(BF16)


def _glu(y, w, b, uz, *, tm, tn):
    m, d = y.shape
    zoff = SSM_WIDTH // tn
    return pl.pallas_call(
        functools.partial(_glu_body, tn=tn),
        out_shape=jax.ShapeDtypeStruct((m, d), BF16),
        grid=(m // tm, d // tn),
        in_specs=[
            pl.BlockSpec((tm, d), lambda i, j: (i, 0)),
            pl.BlockSpec((d, tn), lambda i, j: (0, j)),
            pl.BlockSpec((1, tn), lambda i, j: (0, j)),
            pl.BlockSpec((tm, tn), lambda i, j: (i, zoff + j)),
        ],
        out_specs=pl.BlockSpec((tm, tn), lambda i, j: (i, j)),
        scratch_shapes=[pltpu.VMEM((tm, d), BF16)],
        compiler_params=_params(("parallel", "arbitrary")),
        name="ssm_glu",
    )(y, w, b.reshape(1, d), uz)


def _prep_weights(norm_gdn, w_in_gdn, conv_gdn, a_log_gdn, dt_bias_gdn, onorm_gdn, w_out_gdn,
                  norm_ssm, w_in_ssm, lam_re, lam_im, b_re, b_im, c_re, c_im, d_ssm, log_dt_ssm,
                  w_glu_ssm, b_glu_ssm, w_out_ssm, norm_final):
    w_in_t = jnp.swapaxes(w_in_gdn[0], 0, 1)
    w_ba = jnp.pad(w_in_t[MAIN_DIM:], ((0, 128 - 2 * V_HEADS), (0, 0))).astype(BF16)
    zeros = jnp.zeros((V_HEADS,), F32)
    pad = jnp.zeros((128 - 2 * V_HEADS,), F32)
    gate_params = jnp.stack([jnp.concatenate([zeros, a_log_gdn[0], pad]),
                             jnp.concatenate([zeros, dt_bias_gdn[0], pad])])

    lr, li, ldt = lam_re[0], lam_im[0], log_dt_ssm[0]
    gp = SSM_GROUPS * SSM_STATE
    pw_r, pw_i = _ssm_pow(lr.reshape(1, gp), li.reshape(1, gp),
                          jnp.repeat(ldt, SSM_STATE).reshape(1, gp))
    bblk, cblk = _ssm_blocks(lr, li, ldt.reshape(SSM_GROUPS, 1),
                                 jnp.swapaxes(b_re[0], 1, 2), jnp.swapaxes(b_im[0], 1, 2), c_re[0], c_im[0])
    return dict(
        norm_gdn=norm_gdn[0], w_main=w_in_t, w_ba=w_ba, conv_w=conv_gdn[0], gate_params=gate_params,
        o_gain=onorm_gdn[0], w_out_gdn=w_out_gdn[0].astype(BF16),
        norm_ssm=norm_ssm[0], w_in_ssm=w_in_ssm[0], pw_r=pw_r, pw_i=pw_i, bblk=bblk, cblk=cblk,
        d_skip=d_ssm[0], w_glu=w_glu_ssm[0].astype(BF16), b_glu=b_glu_ssm[0], w_out_ssm=w_out_ssm[0].astype(BF16),
        norm_final=norm_final)


def _ssm_tail(p, x1, y, uz, *, tm, tm_glu):
    y3 = _glu(y, p["w_glu"], p["b_glu"], uz, tm=tm_glu, tn=1024)
    return _outproj(y3, p["w_out_ssm"], x1, p["norm_final"], tm=tm, name="ssm_out")


def _prompt_path(p, x_prompt):
    batch, t_len, d = x_prompt.shape
    x = x_prompt.reshape(batch * t_len, d)
    proj, ba = _norm_matmul(x, p["norm_gdn"], p["w_main"], p["w_ba"], n=MAIN_DIM, tm=2048, tn=512, name="gdn_in",
                            w_transposed=True)
    og, delta = _gdn_prompt(proj, ba, p["conv_w"], p["gate_params"], p["o_gain"], batch=batch, t_len=t_len)
    conv_state = proj.reshape(batch, t_len, MAIN_DIM)[:, t_len - (CONV_W - 1):, :CONV_DIM]
    x1 = _outproj(og, p["w_out_gdn"], x, tm=512, name="gdn_out")
    uz = _norm_matmul(x1, p["norm_ssm"], p["w_in_ssm"], n=2 * SSM_WIDTH, tm=2048, tn=512, name="ssm_in")
    y, h_re, h_im = _ssm_scan(uz, p["bblk"], p["cblk"], p["pw_r"], p["pw_i"], p["d_skip"],
                              batch=batch, t_len=t_len)
    out = _ssm_tail(p, x1, y, uz, tm=512, tm_glu=512)
    return (out.reshape(batch, t_len, d), conv_state[None], delta[None],
            h_re.reshape(1, batch, SSM_GROUPS, SSM_STATE), h_im.reshape(1, batch, SSM_GROUPS, SSM_STATE))


def _sample_path(p, x_sample, state_conv, state_delta, state_re, state_im):
    nb, _, d = x_sample.shape
    x = x_sample.reshape(nb, d)
    proj, ba = _norm_matmul(x, p["norm_gdn"], p["w_main"], p["w_ba"], n=MAIN_DIM, tm=nb, tn=1024, name="gdn_in_s",
                            w_transposed=True)
    beta, eg = _gdn_gates(ba, p["gate_params"])
    xs = jnp.concatenate([state_conv[0], proj[:, None, :CONV_DIM]], axis=1)
    slots = CONV_DIM // DK
    delta, og = _gdn_decode(beta[:, :V_HEADS], eg[:, V_HEADS:2 * V_HEADS],
                            xs.reshape(nb, CONV_W, slots, DK), p["conv_w"].reshape(CONV_W, slots, DK),
                            proj[:, CONV_DIM:].reshape(nb, V_HEADS, DV), p["o_gain"], state_delta[0])
    x1 = _outproj(og.reshape(nb, VAL_DIM).astype(BF16), p["w_out_gdn"], x, tm=nb, name="gdn_out_s")
    uz = _norm_matmul(x1, p["norm_ssm"], p["w_in_ssm"], n=2 * SSM_WIDTH, tm=nb, tn=1024, name="ssm_in_s")
    gp = SSM_GROUPS * SSM_STATE
    y, h_re, h_im = _ssm_decode(uz, p["bblk"], p["cblk"], p["pw_r"], p["pw_i"], p["d_skip"],
                                state_re[0].reshape(nb, gp), state_im[0].reshape(nb, gp))
    out = _ssm_tail(p, x1, y, uz, tm=nb, tm_glu=nb)
    return (out.reshape(nb, 1, d), xs[:, 1:][None], delta[None],
            h_re.reshape(1, nb, SSM_GROUPS, SSM_STATE), h_im.reshape(1, nb, SSM_GROUPS, SSM_STATE))


def kernel(x_prompt, x_sample, state_gdn_conv, state_gdn_delta, state_ssm_re, state_ssm_im, norm_gdn, w_in_gdn, conv_gdn, a_log_gdn, dt_bias_gdn, onorm_gdn, w_out_gdn, norm_ssm, w_in_ssm, lam_re, lam_im, b_re, b_im, c_re, c_im, d_ssm, log_dt_ssm, w_glu_ssm, b_glu_ssm, w_out_ssm, norm_final):
    p = _prep_weights(norm_gdn, w_in_gdn, conv_gdn, a_log_gdn, dt_bias_gdn, onorm_gdn, w_out_gdn,
                      norm_ssm, w_in_ssm, lam_re, lam_im, b_re, b_im, c_re, c_im, d_ssm, log_dt_ssm,
                      w_glu_ssm, b_glu_ssm, w_out_ssm, norm_final)
    y_p, conv_p, delta_p, re_p, im_p = _prompt_path(p, x_prompt)
    y_s, conv_s, delta_s, re_s, im_s = _sample_path(p, x_sample, state_gdn_conv, state_gdn_delta,
                                                    state_ssm_re, state_ssm_im)
    return (y_p, y_s, conv_p, delta_p, re_p, im_p, conv_s, delta_s, re_s, im_s)
```
